```python
import jax, jax.numpy as jnp
from jax import lax
import numpy as np

D_MODEL = 1024
BATCH = 8
SEQ = 2048
DEPTH = 2
DEC_BATCH = 128
DEC_SEQ = 4
PAST_LEN = 8192
PAGE_SIZE = 128

N_HEADS = 16
HEAD_DIM = 64
KV_HEADS = 4
GROUP = N_HEADS // KV_HEADS
Q_DIM = N_HEADS * HEAD_DIM
KV_DIM = KV_HEADS * HEAD_DIM
QKV_DIM = Q_DIM + 2 * KV_DIM
ATTN_SCALE = HEAD_DIM ** -0.5
ROT_DIM = HEAD_DIM // 4
ROPE_THETA = 500000.0
MOBA_BLOCK = 256
MOBA_TOPK = 3
Q_CHUNK = 64
WINDOW = 128
SWA_BLOCK = WINDOW
N_EXPERTS = 256
TOP_K = 8
N_EXPERT_GROUPS = 8
TOPK_GROUPS = 4
D_EXPERT = 256
D_SHARED = 256
ROUTED_SCALE = 2.5
EXPERT_BLOCK = 64
N_MOD = 6
ALPHA = (2 * DEPTH) ** 0.25
BETA = (8 * DEPTH) ** -0.25
LN_EPS = 1e-5
N_MOBA_LAYERS = (DEPTH + 1) // 2
N_SWA_LAYERS = DEPTH // 2

kernel_name = 'hybrid_moba_swa_sink_moe_deepnorm_adaln_step'


def layer_norm(x, g, b):
    xf = x.astype(jnp.float32)
    mu = jnp.mean(xf, -1, keepdims=True)
    var = jnp.mean(jnp.square(xf - mu), -1, keepdims=True)
    return ((xf - mu) * lax.rsqrt(var + LN_EPS)).astype(x.dtype) * g + b


def rope_partial(x, pos):
    half = ROT_DIM // 2
    inv_freq = 1.0 / (ROPE_THETA ** (jnp.arange(0, ROT_DIM, 2, dtype=jnp.float32) / ROT_DIM))
    ang = pos.astype(jnp.float32)[:, None] * inv_freq[None, :]
    cos = jnp.cos(ang)[None, :, None, :]
    sin = jnp.sin(ang)[None, :, None, :]
    x1 = x[..., :half].astype(jnp.float32)
    x2 = x[..., half:ROT_DIM].astype(jnp.float32)
    rot = jnp.concatenate([x1 * cos - x2 * sin, x2 * cos + x1 * sin], -1).astype(x.dtype)
    return jnp.concatenate([rot, x[..., ROT_DIM:]], -1)


def ada_mod(c, w, b):
    m = jax.nn.silu(c) @ w + b
    return jnp.split(m[:, None, :], N_MOD, axis=-1)


def modulate(x, shift, scale):
    return x * (1.0 + scale) + shift


def post_norm(x, gate, out, g, b):
    return layer_norm(ALPHA * x + gate * out, g, b)


def project_qkv(u, w, b, pos):
    B, T, _ = u.shape
    qkv = u @ w + b
    q, k, v = jnp.split(qkv, [Q_DIM, Q_DIM + KV_DIM], axis=-1)
    q = rope_partial(q.reshape(B, T, N_HEADS, HEAD_DIM), pos)
    k = rope_partial(k.reshape(B, T, KV_HEADS, HEAD_DIM), pos)
    return q, k, v.reshape(B, T, KV_HEADS, HEAD_DIM)


def moba_attend(q, q_pos, k_blocks, v_blocks, k_means):
    B, Tq = q.shape[:2]
    nb = k_blocks.shape[1]
    qg = q.reshape(B, Tq, KV_HEADS, GROUP, HEAD_DIM)
    own = (q_pos // MOBA_BLOCK).astype(jnp.int32)
    own_b = jnp.broadcast_to(own[None, :, None, None], (B, Tq, KV_HEADS, 1))
    n_sel = min(MOBA_TOPK, nb - 1)
    if n_sel > 0:
        gate = jnp.einsum('btkgd,bnkd->btkn', qg.astype(jnp.float32), k_means)
        past = jnp.arange(nb)[None, None, None, :] < own[None, :, None, None]
        gate = jnp.where(past, gate, -jnp.inf)
        _, sel = lax.top_k(gate, n_sel)
        blk_idx = jnp.concatenate([sel, own_b], -1)
        slot_ok = jnp.concatenate([sel < own[None, :, None, None], jnp.ones_like(own_b, dtype=bool)], -1)
    else:
        blk_idx = own_b
        slot_ok = jnp.ones_like(own_b, dtype=bool)
    b_idx = jnp.arange(B)[:, None, None, None]
    h_idx = jnp.arange(KV_HEADS)[None, None, :, None]
    kg = k_blocks[b_idx, blk_idx, :, h_idx]
    vg = v_blocks[b_idx, blk_idx, :, h_idx]
    s = jnp.einsum('btkgd,btkjsd->btkgjs', qg, kg).astype(jnp.float32) * ATTN_SCALE
    k_pos = blk_idx[..., None] * MOBA_BLOCK + jnp.arange(MOBA_BLOCK)
    mask = slot_ok[..., None] & (k_pos <= q_pos[None, :, None, None, None])
    s = jnp.where(mask[:, :, :, None], s, -jnp.inf)
    p = jax.nn.softmax(s, axis=(-2, -1))
    o = jnp.einsum('btkgjs,btkjsd->btkgd', p.astype(vg.dtype), vg)
    return o.reshape(B, Tq, Q_DIM)


def moba_prompt(q, k, v):
    B, T = q.shape[:2]
    nb = -(-T // MOBA_BLOCK)
    pad = nb * MOBA_BLOCK - T
    padw = ((0, 0), (0, pad), (0, 0), (0, 0))
    kb = jnp.pad(k, padw).reshape(B, nb, MOBA_BLOCK, KV_HEADS, HEAD_DIM)
    vb = jnp.pad(v, padw).reshape(B, nb, MOBA_BLOCK, KV_HEADS, HEAD_DIM)
    means = jnp.mean(kb, axis=2, dtype=jnp.float32)
    nc = T // Q_CHUNK
    qc = q.reshape(B, nc, Q_CHUNK, N_HEADS, HEAD_DIM).swapaxes(0, 1)
    pc = jnp.arange(T, dtype=jnp.int32).reshape(nc, Q_CHUNK)
    out = lax.map(lambda a: moba_attend(a[0], a[1], kb, vb, means), (qc, pc))
    return out.swapaxes(0, 1).reshape(B, T, Q_DIM)


def moba_sample(q, k_new, v_new, pool_k, pool_v, page_table):
    DB, DS = q.shape[:2]
    past_len = page_table.shape[1] * PAGE_SIZE
    total = past_len + DS
    nb = -(-total // MOBA_BLOCK)
    pad = nb * MOBA_BLOCK - total

    def blocks(pool, new):
        past = pool[page_table].reshape(DB, past_len, KV_HEADS, HEAD_DIM)
        tail = jnp.zeros((DB, pad, KV_HEADS, HEAD_DIM), new.dtype)
        return jnp.concatenate([past, new, tail], axis=1).reshape(DB, nb, MOBA_BLOCK, KV_HEADS, HEAD_DIM)

    kb, vb = blocks(pool_k, k_new), blocks(pool_v, v_new)
    means = jnp.mean(kb, axis=2, dtype=jnp.float32)
    q_pos = past_len + jnp.arange(DS, dtype=jnp.int32)
    return moba_attend(q, q_pos, kb, vb, means)


def softmax_with_sink(s, sinks):
    sk = jnp.broadcast_to(sinks.astype(jnp.float32).reshape(KV_HEADS, GROUP, 1, 1), s.shape[:-1] + (1,))
    return jax.nn.softmax(jnp.concatenate([s, sk], -1), axis=-1)[..., :-1]


def swa_prompt(q, k, v, sinks):
    B, T = q.shape[:2]
    nb = T // SWA_BLOCK

    def band(a):
        ap = jnp.pad(a, ((0, 0), (SWA_BLOCK, 0), (0, 0), (0, 0))).reshape(B, nb + 1, SWA_BLOCK, KV_HEADS, HEAD_DIM)
        return jnp.concatenate([ap[:, :-1], ap[:, 1:]], axis=2)

    kb, vb = band(k), band(v)
    qb = q.reshape(B, nb, SWA_BLOCK, KV_HEADS, GROUP, HEAD_DIM)
    s = jnp.einsum('bnqkgd,bnskd->bnkgqs', qb, kb).astype(jnp.float32) * ATTN_SCALE
    qi = jnp.arange(SWA_BLOCK)[:, None]
    si = jnp.arange(2 * SWA_BLOCK)[None, :]
    dist = SWA_BLOCK + qi - si
    k_pos = (jnp.arange(nb)[:, None, None] - 1) * SWA_BLOCK + si
    mask = (dist >= 0) & (dist <= WINDOW) & (k_pos >= 0)
    s = jnp.where(mask[None, :, None, None], s, -jnp.inf)
    p = softmax_with_sink(s, sinks)
    o = jnp.einsum('bnkgqs,bnskd->bnqkgd', p.astype(vb.dtype), vb)
    return o.reshape(B, T, Q_DIM)


def swa_sample(q, k_new, v_new, buf_k, buf_v, sinks, past_len):
    DB, DS = q.shape[:2]
    wb = buf_k.shape[1]
    kc = jnp.concatenate([buf_k, k_new], axis=1)
    vc = jnp.concatenate([buf_v, v_new], axis=1)
    qg = q.reshape(DB, DS, KV_HEADS, GROUP, HEAD_DIM)
    s = jnp.einsum('bqkgd,bskd->bkgqs', qg, kc).astype(jnp.float32) * ATTN_SCALE
    dist = (past_len + jnp.arange(DS)[:, None]) - (past_len - wb + jnp.arange(wb + DS)[None, :])
    mask = (dist >= 0) & (dist <= WINDOW)
    s = jnp.where(mask, s, -jnp.inf)
    p = softmax_with_sink(s, sinks)
    o = jnp.einsum('bkgqs,bskd->bqkgd', p.astype(vc.dtype), vc).reshape(DB, DS, Q_DIM)
    return o, kc[:, -wb:], vc[:, -wb:]


def route(h, w_router, b_router):
    scores = jax.nn.sigmoid((h @ w_router).astype(jnp.float32))
    biased = scores + b_router.astype(jnp.float32)
    per_group = N_EXPERTS // N_EXPERT_GROUPS
    grp_score = lax.top_k(biased.reshape(-1, N_EXPERT_GROUPS, per_group), 2)[0].sum(-1)
    _, top_grp = lax.top_k(grp_score, TOPK_GROUPS)
    grp_keep = jax.nn.one_hot(top_grp, N_EXPERT_GROUPS, dtype=jnp.float32).sum(-2) > 0
    masked = jnp.where(jnp.repeat(grp_keep, per_group, axis=-1), biased, -jnp.inf)
    _, idx = lax.top_k(masked, TOP_K)
    w = jnp.take_along_axis(scores, idx, -1)
    w = w / jnp.sum(w, -1, keepdims=True) * ROUTED_SCALE
    return idx, w


def routed_experts(h, idx, wts, w_gate, w_up, w_down):
    n, d = h.shape
    a = n * TOP_K
    flat_e = idx.reshape(a)
    order = jnp.argsort(flat_e)
    e_sorted = flat_e[order]
    tok_sorted = order // TOP_K
    w_sorted = wts.reshape(a)[order].astype(h.dtype)
    counts = jnp.bincount(flat_e, length=N_EXPERTS)
    padded = (counts + EXPERT_BLOCK - 1) // EXPERT_BLOCK * EXPERT_BLOCK
    pad_end = jnp.cumsum(padded)
    pad_start = pad_end - padded
    start = jnp.cumsum(counts) - counts
    dest = pad_start[e_sorted] + jnp.arange(a) - start[e_sorted]
    n_blocks = (a + N_EXPERTS * (EXPERT_BLOCK - 1) + EXPERT_BLOCK - 1) // EXPERT_BLOCK
    rows = n_blocks * EXPERT_BLOCK
    buf = jnp.zeros((rows, d), h.dtype).at[dest].set(h[tok_sorted])
    blk_e = jnp.minimum(jnp.searchsorted(pad_end, jnp.arange(n_blocks) * EXPERT_BLOCK, side='right'), N_EXPERTS - 1)

    def expert_block(args):
        xb, e = args
        return (jax.nn.silu(xb @ w_gate[e]) * (xb @ w_up[e])) @ w_down[e]

    out = lax.map(expert_block, (buf.reshape(n_blocks, EXPERT_BLOCK, d), blk_e)).reshape(rows, d)
    return jax.ops.segment_sum(out[dest] * w_sorted[:, None], tok_sorted, num_segments=n)


def moe_ffn(u, w_router, b_router, w_eg, w_eu, w_ed, w_sg, w_su, w_sd):
    B, T, d = u.shape
    h = u.reshape(B * T, d)
    idx, wts = route(h, w_router, b_router)
    shared = (jax.nn.silu(h @ w_sg) * (h @ w_su)) @ w_sd
    return (routed_experts(h, idx, wts, w_eg, w_eu, w_ed) + shared).reshape(B, T, d)


def setup_inputs(seed: int = 0) -> dict:
    key = jax.random.key(seed)
    ks = jax.random.split(key, 32)
    f32 = jnp.float32
    d = D_MODEL

    def nrm(k, shape, s=1.0):
        return jax.random.normal(k, shape, f32) * s

    n_pages = PAST_LEN // PAGE_SIZE
    n_used = DEC_BATCH * n_pages
    n_pool = n_used + max(1, n_used // 4)
    win_buf = min(WINDOW, PAST_LEN)
    page_table = jax.random.permutation(ks[8], n_pool)[:n_used].reshape(DEC_BATCH, n_pages).astype(jnp.int32)
    gate_offset = jnp.concatenate([jnp.zeros((2 * d,), f32), jnp.ones((d,), f32), jnp.zeros((2 * d,), f32), jnp.ones((d,), f32)])
    v_scale = jnp.concatenate([jnp.ones((Q_DIM + KV_DIM,), f32), jnp.full((KV_DIM,), BETA, f32)])
    return {
        'x_prompt': nrm(ks[0], (BATCH, SEQ, d)),
        'x_sample': nrm(ks[1], (DEC_BATCH, DEC_SEQ, d)),
        'c_prompt': nrm(ks[2], (BATCH, d)),
        'c_sample': nrm(ks[3], (DEC_BATCH, d)),
        'cache_moba_k': nrm(ks[4], (N_MOBA_LAYERS, n_pool, PAGE_SIZE, KV_HEADS, HEAD_DIM)),
        'cache_moba_v': nrm(ks[5], (N_MOBA_LAYERS, n_pool, PAGE_SIZE, KV_HEADS, HEAD_DIM)),
        'state_swa_k': nrm(ks[6], (N_SWA_LAYERS, DEC_BATCH, win_buf, KV_HEADS, HEAD_DIM)),
        'state_swa_v': nrm(ks[7], (N_SWA_LAYERS, DEC_BATCH, win_buf, KV_HEADS, HEAD_DIM)),
        'page_table': page_table,
        'w_ada': nrm(ks[9], (DEPTH, d, N_MOD * d), 0.1 * d ** -0.5),
        'b_ada': nrm(ks[10], (DEPTH, N_MOD * d), 0.02) + gate_offset,
        'w_qkv': nrm(ks[11], (DEPTH, d, QKV_DIM), d ** -0.5) * v_scale,
        'b_qkv': nrm(ks[12], (DEPTH, QKV_DIM), 0.02),
        'attn_sinks': nrm(ks[13], (N_SWA_LAYERS, N_HEADS), 0.5),
        'w_o': nrm(ks[14], (DEPTH, Q_DIM, d), BETA * Q_DIM ** -0.5),
        'b_o': nrm(ks[15], (DEPTH, d), 0.02),
        'ln_attn_g': 1.0 + nrm(ks[16], (DEPTH, d), 0.02),
        'ln_attn_b': nrm(ks[17], (DEPTH, d), 0.02),
        'w_router': nrm(ks[18], (DEPTH, d, N_EXPERTS), d ** -0.5),
        'b_router': nrm(ks[19], (DEPTH, N_EXPERTS), 0.01),
        'w_exp_gate': nrm(ks[20], (DEPTH, N_EXPERTS, d, D_EXPERT), d ** -0.5),
        'w_exp_up': nrm(ks[21], (DEPTH, N_EXPERTS, d, D_EXPERT), d ** -0.5),
        'w_exp_down': nrm(ks[22], (DEPTH, N_EXPERTS, D_EXPERT, d), BETA * D_EXPERT ** -0.5),
        'w_sh_gate': nrm(ks[23], (DEPTH, d, D_SHARED), d ** -0.5),
        'w_sh_up': nrm(ks[24], (DEPTH, d, D_SHARED), d ** -0.5),
        'w_sh_down': nrm(ks[25], (DEPTH, D_SHARED, d), BETA * D_SHARED ** -0.5),
        'ln_ffn_g': 1.0 + nrm(ks[26], (DEPTH, d), 0.02),
        'ln_ffn_b': nrm(ks[27], (DEPTH, d), 0.02),
    }


def reference(x_prompt, x_sample, c_prompt, c_sample, cache_moba_k, cache_moba_v, state_swa_k, state_swa_v,
              page_table, w_ada, b_ada, w_qkv, b_qkv, attn_sinks, w_o, b_o, ln_attn_g, ln_attn_b,
              w_router, b_router, w_exp_gate, w_exp_up, w_exp_down, w_sh_gate, w_sh_up, w_sh_down,
              ln_ffn_g, ln_ffn_b):
    past_len = page_table.shape[1] * PAGE_SIZE
    pos_p = jnp.arange(x_prompt.shape[1], dtype=jnp.int32)
    pos_s = past_len + jnp.arange(x_sample.shape[1], dtype=jnp.int32)
    xp, xs = x_prompt, x_sample
    moba_kp, moba_vp, moba_ks, moba_vs = [], [], [], []
    swa_kp, swa_vp, swa_ks, swa_vs = [], [], [], []
    for i in range(DEPTH):
        j = i // 2
        mp = ada_mod(c_prompt, w_ada[i], b_ada[i])
        ms = ada_mod(c_sample, w_ada[i], b_ada[i])
        qp, kp, vp = project_qkv(modulate(xp, mp[0], mp[1]), w_qkv[i], b_qkv[i], pos_p)
        qs, ks, vs = project_qkv(modulate(xs, ms[0], ms[1]), w_qkv[i], b_qkv[i], pos_s)
        if i % 2 == 0:
            ap = moba_prompt(qp, kp, vp)
            a_s = moba_sample(qs, ks, vs, cache_moba_k[j], cache_moba_v[j], page_table)
            moba_kp.append(kp)
            moba_vp.append(vp)
            moba_ks.append(ks)
            moba_vs.append(vs)
        else:
            ap = swa_prompt(qp, kp, vp, attn_sinks[j])
            a_s, nbk, nbv = swa_sample(qs, ks, vs, state_swa_k[j], state_swa_v[j], attn_sinks[j], past_len)
            wbp = min(WINDOW, kp.shape[1])
            swa_kp.append(kp[:, -wbp:])
            swa_vp.append(vp[:, -wbp:])
            swa_ks.append(nbk)
            swa_vs.append(nbv)
        xp = post_norm(xp, mp[2], ap @ w_o[i] + b_o[i], ln_attn_g[i], ln_attn_b[i])
        xs = post_norm(xs, ms[2], a_s @ w_o[i] + b_o[i], ln_attn_g[i], ln_attn_b[i])
        fp = moe_ffn(modulate(xp, mp[3], mp[4]), w_router[i], b_router[i], w_exp_gate[i], w_exp_up[i],
                     w_exp_down[i], w_sh_gate[i], w_sh_up[i], w_sh_down[i])
        fs = moe_ffn(modulate(xs, ms[3], ms[4]), w_router[i], b_router[i], w_exp_gate[i], w_exp_up[i],
                     w_exp_down[i], w_sh_gate[i], w_sh_up[i], w_sh_down[i])
        xp = post_norm(xp, mp[5], fp, ln_ffn_g[i], ln_ffn_b[i])
        xs = post_norm(xs, ms[5], fs, ln_ffn_g[i], ln_ffn_b[i])
    return (xp, xs, jnp.stack(moba_kp), jnp.stack(moba_vp), jnp.stack(moba_ks), jnp.stack(moba_vs),
            jnp.stack(swa_kp), jnp.stack(swa_vp), jnp.stack(swa_ks), jnp.stack(swa_vs))
```

```python
import functools

import jax
import jax.numpy as jnp
from jax import lax
from jax.experimental import pallas as pl
from jax.experimental.pallas import tpu as pltpu

F32 = jnp.float32
BF16 = jnp.bfloat16
I32 = jnp.int32

D_MODEL = 1024
N_HEADS = 16
HEAD_DIM = 64
KV_HEADS = 4
GROUP = N_HEADS // KV_HEADS
Q_DIM = N_HEADS * HEAD_DIM
KV_DIM = KV_HEADS * HEAD_DIM
QKV_DIM = Q_DIM + 2 * KV_DIM
ATTN_SCALE = HEAD_DIM ** -0.5
ROT_DIM = HEAD_DIM // 4
ROPE_THETA = 500000.0
PAGE_SIZE = 128
MOBA_BLOCK = 256
MOBA_TOPK = 3
WINDOW = 128
N_EXPERTS = 256
TOP_K = 8
N_EXPERT_GROUPS = 8
TOPK_GROUPS = 4
PER_GROUP = N_EXPERTS // N_EXPERT_GROUPS
D_EXPERT = 256
ROUTED_SCALE = 2.5
N_MOD = 6
DEPTH = 2
ALPHA = (2 * DEPTH) ** 0.25
LN_EPS = 1e-5

LANES = 128
VMEM_LIMIT = 48 * 1024 * 1024
MOE_ROWS = 256
NEG_INF = float("-inf")
BIG_INDEX = 1e9

_NT = (((1,), (1,)), ((), ()))


def _dot(a, b):
    return jnp.dot(a, b, preferred_element_type=F32)


def _dot_nt(a, b):
    return lax.dot_general(a, b, _NT, preferred_element_type=F32)


def _split_bf16(x):
    hi = x.astype(BF16)
    lo = (x - hi.astype(F32)).astype(BF16)
    return hi, lo


def _sigmoid(x):
    return 1.0 / (1.0 + jnp.exp(-x))


def _params(*sem):
    return pltpu.CompilerParams(dimension_semantics=sem, vmem_limit_bytes=VMEM_LIMIT)


def _layer_norm(y, g, b):
    mu = jnp.mean(y, axis=-1, keepdims=True)
    yc = y - mu
    var = jnp.mean(yc * yc, axis=-1, keepdims=True)
    return yc * lax.rsqrt(var + LN_EPS) * g + b


def _ada_kernel(c_ref, w_ref, b_ref, o_ref):
    c = c_ref[...]
    a_hi, a_lo = _split_bf16(c * _sigmoid(c))
    w_hi, w_lo = _split_bf16(w_ref[0])
    o_ref[0] = _dot(a_hi, w_hi) + _dot(a_lo, w_hi) + _dot(a_hi, w_lo) + b_ref[0]


def _ada_mod(c_all, w_ada, b_ada):
    rows = c_all.shape[0]
    depth = w_ada.shape[0]
    return pl.pallas_call(
        _ada_kernel,
        grid=(depth, N_MOD),
        in_specs=[
            pl.BlockSpec((rows, D_MODEL), lambda l, j: (0, 0)),
            pl.BlockSpec((1, D_MODEL, D_MODEL), lambda l, j: (l, 0, j)),
            pl.BlockSpec((1, 1, D_MODEL), lambda l, j: (l, 0, j)),
        ],
        out_specs=pl.BlockSpec((1, rows, D_MODEL), lambda l, j: (l, 0, j)),
        out_shape=jax.ShapeDtypeStruct((depth, rows, N_MOD * D_MODEL), F32),
        compiler_params=_params("arbitrary", "arbitrary"),
        name="ada_mod",
    )(c_all, w_ada, b_ada.reshape(depth, 1, N_MOD * D_MODEL))


def _qkv_kernel(x_ref, sh_ref, sc_ref, w_ref, b_ref, c_ref, s1_ref, s2_ref,
                q_ref, k_ref, v_ref, kb_ref, vb_ref, *qs_ref):
    u = x_ref[...] * (1.0 + sc_ref[0]) + sh_ref[0]
    qkv = _dot(u.astype(BF16), w_ref[...]) + b_ref[...]
    cos, s1, s2 = c_ref[...], s1_ref[...], s2_ref[...]
    rots = []
    for j in range((Q_DIM + KV_DIM) // LANES):
        blk = qkv[:, j * LANES:(j + 1) * LANES]
        rots.append(blk * cos + pltpu.roll(blk, LANES - ROT_DIM // 2, 1) * s1
                    + pltpu.roll(blk, ROT_DIM // 2, 1) * s2)
    nq = Q_DIM // LANES
    for j in range(nq):
        q_ref[:, j * LANES:(j + 1) * LANES] = (rots[j] * ATTN_SCALE).astype(BF16)
    for j in range(KV_DIM // LANES):
        k_ref[:, j * LANES:(j + 1) * LANES] = rots[nq + j]
        kb_ref[:, j * LANES:(j + 1) * LANES] = rots[nq + j].astype(BF16)
    v = qkv[:, Q_DIM + KV_DIM:]
    v_ref[...] = v
    vb_ref[...] = v.astype(BF16)
    if qs_ref:
        lane = lax.broadcasted_iota(I32, rots[0].shape, 1)
        halves = []
        for kv in range(KV_HEADS):
            t = rots[2 * kv] + rots[2 * kv + 1]
            halves.append(t + pltpu.roll(t, HEAD_DIM, 1))
        for j in range(KV_DIM // LANES):
            qs_ref[0][:, j * LANES:(j + 1) * LANES] = jnp.where(
                lane < HEAD_DIM, halves[2 * j], halves[2 * j + 1])


def _qkv_project(x, shift, scale, w_bf, b, tabs, tm, with_qsum):
    n = x.shape[0]
    n_tab = tabs[0].shape[0] // tm
    per_mod = n // tm // shift.shape[0]
    mod_rows = shift.shape[1]
    row_spec = lambda width: pl.BlockSpec((tm, width), lambda i: (i, 0))
    mod_spec = pl.BlockSpec((1, mod_rows, D_MODEL), lambda i: (i // per_mod, 0, 0))
    tab_spec = pl.BlockSpec((tm, LANES), lambda i: (i % n_tab, 0))
    out_shape = [
        jax.ShapeDtypeStruct((n, Q_DIM), BF16),
        jax.ShapeDtypeStruct((n, KV_DIM), F32),
        jax.ShapeDtypeStruct((n, KV_DIM), F32),
        jax.ShapeDtypeStruct((n, KV_DIM), BF16),
        jax.ShapeDtypeStruct((n, KV_DIM), BF16),
    ]
    out_specs = [row_spec(Q_DIM)] + [row_spec(KV_DIM)] * 4
    if with_qsum:
        out_shape.append(jax.ShapeDtypeStruct((n, KV_DIM), F32))
        out_specs.append(row_spec(KV_DIM))
    return pl.pallas_call(
        _qkv_kernel,
        grid=(n // tm,),
        in_specs=[
            row_spec(D_MODEL), mod_spec, mod_spec,
            pl.BlockSpec((D_MODEL, QKV_DIM), lambda i: (0, 0)),
            pl.BlockSpec((1, QKV_DIM), lambda i: (0, 0)),
            tab_spec, tab_spec, tab_spec,
        ],
        out_specs=out_specs,
        out_shape=out_shape,
        compiler_params=_params("arbitrary"),
        name="qkv_rope",
    )(x, shift, scale, w_bf, b.reshape(1, QKV_DIM), *tabs)


def _rope_tables(pos):
    half = ROT_DIM // 2
    inv_freq = 1.0 / (ROPE_THETA ** (jnp.arange(0, ROT_DIM, 2, dtype=F32) / ROT_DIM))
    ang = pos.astype(F32)[:, None] * inv_freq[None, :]
    cos, sin = jnp.cos(ang), jnp.sin(ang)
    rest = HEAD_DIM - ROT_DIM
    ones = jnp.ones((pos.shape[0], rest), F32)
    zeros = jnp.zeros((pos.shape[0], rest), F32)
    zh = jnp.zeros_like(sin)
    c = jnp.concatenate([cos, cos, ones], -1)
    s1 = jnp.concatenate([-sin, zh, zeros], -1)
    s2 = jnp.concatenate([zh, sin, zeros], -1)
    reps = LANES // HEAD_DIM
    return tuple(jnp.tile(t, (1, reps)) for t in (c, s1, s2))


def _moba_select_kernel(qs_ref, k_ref, sel_ref):
    t_len = k_ref.shape[1]
    nb = t_len // MOBA_BLOCK
    k = k_ref[0]
    nb_pad = 16
    means = jnp.concatenate(
        [jnp.sum(k[n * MOBA_BLOCK:(n + 1) * MOBA_BLOCK], axis=0, keepdims=True) / MOBA_BLOCK
         for n in range(nb)] + [jnp.zeros((nb_pad - nb, KV_DIM), F32)], axis=0)
    lane_head = lax.broadcasted_iota(I32, (nb_pad, KV_DIM), 1) // HEAD_DIM
    q_hi, q_lo = _split_bf16(qs_ref[0])
    pos_blk = lax.broadcasted_iota(I32, (nb, t_len), 1) // MOBA_BLOCK
    blk = lax.broadcasted_iota(I32, (nb, t_len), 0)
    past = blk < pos_blk
    for kv in range(KV_HEADS):
        m_hi, m_lo = _split_bf16(jnp.where(lane_head == kv, means, 0.0))
        gate = (_dot_nt(m_hi, q_hi) + _dot_nt(m_lo, q_hi) + _dot_nt(m_hi, q_lo))[:nb]
        gate = jnp.where(past, gate, NEG_INF)
        rank = jnp.zeros((nb, t_len), F32)
        for m in range(nb):
            other = gate[m:m + 1, :]
            ahead = (other > gate) | ((other == gate) & (blk > m))
            rank = rank + jnp.where(ahead, 1.0, 0.0)
        sel_ref[0, kv] = jnp.where(past & (rank < MOBA_TOPK), 1.0, 0.0)


def _moba_select(qsum, k):
    b, t, _ = k.shape
    nb = t // MOBA_BLOCK
    spec = pl.BlockSpec((1, t, KV_DIM), lambda i: (i, 0, 0))
    return pl.pallas_call(
        _moba_select_kernel,
        grid=(b,),
        in_specs=[spec, spec],
        out_specs=pl.BlockSpec((1, KV_HEADS, nb, t), lambda i: (i, 0, 0, 0)),
        out_shape=jax.ShapeDtypeStruct((b, KV_HEADS, nb, t), F32),
        compiler_params=_params("arbitrary"),
        name="moba_select",
    )(qsum, k)


def _moba_prompt_kernel(q_ref, k_ref, v_ref, sel_ref, o_ref):
    qi = pl.program_id(2)
    tq = MOBA_BLOCK
    rows = GROUP * tq
    q = q_ref[0].reshape(rows, HEAD_DIM)
    sel = sel_ref[0, 0]

    def scores(n):
        kb = k_ref[0, 0, pl.ds(pl.multiple_of(n * tq, tq), tq), :]
        return _dot_nt(q, kb)

    def values(n):
        return v_ref[0, 0, pl.ds(pl.multiple_of(n * tq, tq), tq), :]

    r = lax.broadcasted_iota(I32, (rows, tq), 0) & (tq - 1)
    c = lax.broadcasted_iota(I32, (rows, tq), 1)
    s = jnp.where(c <= r, scores(qi), NEG_INF)
    m0 = jnp.max(s, axis=-1, keepdims=True)
    p = jnp.exp(s - m0)
    l0 = jnp.sum(p, axis=-1, keepdims=True)
    acc0 = _dot(p.astype(BF16), values(qi))

    nb_lane = lax.broadcasted_iota(I32, sel.shape, 1)

    def body(n, carry):
        m, l, acc = carry
        col = jnp.sum(jnp.where(nb_lane == n, sel, 0.0), axis=-1, keepdims=True)
        keep = jnp.concatenate([col] * GROUP, axis=0) > 0.0
        s = jnp.where(keep, scores(n), NEG_INF)
        m_new = jnp.maximum(m, jnp.max(s, axis=-1, keepdims=True))
        alpha = jnp.exp(m - m_new)
        p = jnp.exp(s - m_new)
        l = alpha * l + jnp.sum(p, axis=-1, keepdims=True)
        acc = alpha * acc + _dot(p.astype(BF16), values(n))
        return m_new, l, acc

    _, l, acc = lax.fori_loop(0, qi, body, (m0, l0, acc0))
    o_ref[0] = (acc / l).astype(BF16).reshape(GROUP, tq, HEAD_DIM)


def _moba_prompt(q_hm, k_hm, v_hm, sel_t):
    b, _, t, _ = q_hm.shape
    nb = t // MOBA_BLOCK
    kv_spec = pl.BlockSpec((1, 1, t, HEAD_DIM), lambda i, j, n: (i, j, 0, 0))
    q_spec = pl.BlockSpec((1, GROUP, MOBA_BLOCK, HEAD_DIM), lambda i, j, n: (i, j, n, 0))
    return pl.pallas_call(
        _moba_prompt_kernel,
        grid=(b, KV_HEADS, nb),
        in_specs=[q_spec, kv_spec, kv_spec,
                  pl.BlockSpec((1, 1, MOBA_BLOCK, nb), lambda i, j, n: (i, j, n, 0))],
        out_specs=q_spec,
        out_shape=jax.ShapeDtypeStruct(q_hm.shape, BF16),
        compiler_params=_params("arbitrary", "arbitrary", "arbitrary"),
        name="moba_prompt",
    )(q_hm, k_hm, v_hm, sel_t)


def _swa_prompt_kernel(sink_ref, q_ref, kp_ref, k_ref, vp_ref, v_ref, o_ref):
    j = pl.program_id(1)
    n = pl.program_id(2)
    tq = WINDOW
    r = lax.broadcasted_iota(I32, (tq, tq), 0)
    c = lax.broadcasted_iota(I32, (tq, tq), 1)
    prev_ok = c >= r
    own_ok = c <= r
    first_block = jnp.where(n > 0, 0.0, NEG_INF)
    kp, k, vp, v = kp_ref[0, 0], k_ref[0, 0], vp_ref[0, 0], v_ref[0, 0]
    for g in range(GROUP):
        q = q_ref[0, g]
        sink = sink_ref[j * GROUP + g]
        sp = jnp.where(prev_ok, _dot_nt(q, kp) + first_block, NEG_INF)
        so = jnp.where(own_ok, _dot_nt(q, k), NEG_INF)
        m = jnp.maximum(jnp.maximum(jnp.max(sp, axis=-1, keepdims=True),
                                    jnp.max(so, axis=-1, keepdims=True)), sink)
        pp = jnp.exp(sp - m)
        po = jnp.exp(so - m)
        den = (jnp.sum(pp, axis=-1, keepdims=True) + jnp.sum(po, axis=-1, keepdims=True)
               + jnp.exp(sink - m))
        o = _dot(pp.astype(BF16), vp) + _dot(po.astype(BF16), v)
        o_ref[0, g] = (o / den).astype(BF16)


def _swa_prompt(q_hm, k_hm, v_hm, sinks):
    b, _, t, _ = q_hm.shape
    nb = t // WINDOW
    q_spec = pl.BlockSpec((1, GROUP, WINDOW, HEAD_DIM), lambda i, j, n, s: (i, j, n, 0))
    own = pl.BlockSpec((1, 1, WINDOW, HEAD_DIM), lambda i, j, n, s: (i, j, n, 0))
    prev = pl.BlockSpec((1, 1, WINDOW, HEAD_DIM), lambda i, j, n, s: (i, j, jnp.maximum(n - 1, 0), 0))
    return pl.pallas_call(
        _swa_prompt_kernel,
        grid_spec=pltpu.PrefetchScalarGridSpec(
            num_scalar_prefetch=1,
            grid=(b, KV_HEADS, nb),
            in_specs=[q_spec, prev, own, prev, own],
            out_specs=q_spec,
        ),
        out_shape=jax.ShapeDtypeStruct(q_hm.shape, BF16),
        compiler_params=_params("arbitrary", "arbitrary", "arbitrary"),
        name="swa_prompt",
    )(sinks, q_hm, k_hm, k_hm, v_hm, v_hm)


def _fold_heads(acc):
    rows = acc.shape[0]
    lane_head = lax.broadcasted_iota(I32, (rows, KV_DIM), 1) // HEAD_DIM
    row_head = (lax.broadcasted_iota(I32, (rows, KV_DIM), 0) // GROUP) % KV_HEADS
    a = jnp.where(lane_head == row_head, acc, 0.0)
    a = a[:, :LANES] + a[:, LANES:]
    a = a + pltpu.roll(a, HEAD_DIM, 1)
    return a[:, :HEAD_DIM]


def _moba_sample_kernel(pt_ref, qbd_ref, qs_ref, kn_ref, vn_ref, kpool, vpool, o_ref,
                        buf, sem, s_scr, mean_scr, stat_scr, acc_scr, *, n_pages, chunk_pages):
    b = pl.program_id(0)
    c = pl.program_id(1)
    n_seq = pl.num_programs(0)
    n_chunks = n_pages // chunk_pages
    steps = 2 * n_chunks
    g = b * steps + c
    slot = g % 2
    rows = qbd_ref.shape[1]
    blk_pages = MOBA_BLOCK // PAGE_SIZE
    blocks_per_chunk = chunk_pages // blk_pages
    nb = n_pages // blk_pages

    def page_copy(pool, page, slot_, p):
        return pltpu.make_async_copy(pool.at[page], buf.at[slot_, p], sem.at[slot_])

    def start(step, seq, slot_):
        base = jnp.where(step >= n_chunks, step - n_chunks, step) * chunk_pages
        pages = [pt_ref[seq, base + p] for p in range(chunk_pages)]

        @pl.when(step < n_chunks)
        def _():
            for p in range(chunk_pages):
                page_copy(kpool, pages[p], slot_, p).start()

        @pl.when(step >= n_chunks)
        def _():
            for p in range(chunk_pages):
                page_copy(vpool, pages[p], slot_, p).start()

    @pl.when(g == 0)
    def _():
        start(c, b, slot)

    nxt = g + 1

    @pl.when(nxt < n_seq * steps)
    def _():
        start(nxt % steps, nxt // steps, nxt % 2)

    for p in range(chunk_pages):
        page_copy(kpool, 0, slot, p).wait()

    qbd = qbd_ref[0]

    @pl.when(g == 0)
    def _():
        mean_scr[...] = jnp.zeros(mean_scr.shape, F32)

    @pl.when(c < n_chunks)
    def _():
        def body(i, _):
            kb = buf[slot, pl.ds(i * blk_pages, blk_pages)].reshape(MOBA_BLOCK, KV_DIM)
            n = c * blocks_per_chunk + i
            s_scr[n] = _dot_nt(qbd, kb.astype(BF16))
            mean_scr[pl.ds(n, 1), :] = jnp.sum(kb, axis=0, keepdims=True) / MOBA_BLOCK
            return 0
        lax.fori_loop(0, blocks_per_chunk, body, 0)

    @pl.when(c == n_chunks - 1)
    def _():
        q_hi, q_lo = _split_bf16(qs_ref[0])
        m_hi, m_lo = _split_bf16(mean_scr[...])
        gate = _dot_nt(q_hi, m_hi) + _dot_nt(q_lo, m_hi) + _dot_nt(q_hi, m_lo)
        lane = lax.broadcasted_iota(I32, gate.shape, 1).astype(F32)
        gate = jnp.where(lane < nb, gate, NEG_INF)
        sel = jnp.zeros(gate.shape, F32)
        for _ in range(min(MOBA_TOPK, nb)):
            top = jnp.max(gate, axis=-1, keepdims=True)
            first = jnp.min(jnp.where(gate == top, lane, BIG_INDEX), axis=-1, keepdims=True)
            hit = lane == first
            sel = jnp.where(hit, 1.0, sel)
            gate = jnp.where(hit, NEG_INF, gate)
        sel_rows = jnp.concatenate(
            [jnp.broadcast_to(sel[r:r + 1], (GROUP, LANES)) for r in range(sel.shape[0])], axis=0)
        stat_scr[:, 0:LANES] = sel_rows
        s_new = _dot_nt(qbd, kn_ref[0].astype(BF16))
        qrow = lax.broadcasted_iota(I32, s_new.shape, 0) // N_HEADS
        s_new = jnp.where(lax.broadcasted_iota(I32, s_new.shape, 1) <= qrow, s_new, NEG_INF)
        m_run = jnp.max(s_new, axis=-1, keepdims=True)
        nb_lane = lax.broadcasted_iota(I32, sel_rows.shape, 1)

        def max_body(n, m_run):
            col = jnp.sum(jnp.where(nb_lane == n, sel_rows, 0.0), axis=-1, keepdims=True)
            s = jnp.where(col > 0.0, s_scr[n], NEG_INF)
            return jnp.maximum(m_run, jnp.max(s, axis=-1, keepdims=True))
        m_run = lax.fori_loop(0, nb, max_body, m_run)
        p_new = jnp.exp(s_new - m_run)
        stat_scr[:, LANES:LANES + 1] = m_run
        stat_scr[:, LANES + 1:LANES + 2] = jnp.sum(p_new, axis=-1, keepdims=True)
        acc_scr[...] = _dot(p_new.astype(BF16), vn_ref[0].astype(BF16))

    @pl.when(c >= n_chunks)
    def _():
        sel_rows = stat_scr[:, 0:LANES]
        m_run = stat_scr[:, LANES:LANES + 1]
        nb_lane = lax.broadcasted_iota(I32, sel_rows.shape, 1)

        def body(i, carry):
            l, acc = carry
            n = (c - n_chunks) * blocks_per_chunk + i
            vb = buf[slot, pl.ds(i * blk_pages, blk_pages)].reshape(MOBA_BLOCK, KV_DIM)
            col = jnp.sum(jnp.where(nb_lane == n, sel_rows, 0.0), axis=-1, keepdims=True)
            p = jnp.exp(jnp.where(col > 0.0, s_scr[n], NEG_INF) - m_run)
            return (l + jnp.sum(p, axis=-1, keepdims=True),
                    acc + _dot(p.astype(BF16), vb.astype(BF16)))
        l, acc = lax.fori_loop(0, blocks_per_chunk, body,
                               (stat_scr[:, LANES + 1:LANES + 2], acc_scr[...]))
        stat_scr[:, LANES + 1:LANES + 2] = l
        acc_scr[...] = acc

    @pl.when(c == steps - 1)
    def _():
        o_ref[0] = _fold_heads(acc_scr[...] / stat_scr[:, LANES + 1:LANES + 2])


def _moba_sample(page_table, qbd, qs_bd, k_new, v_new, kpool, vpool):
    db, rows, _ = qbd.shape
    ds = k_new.shape[1]
    n_pages = page_table.shape[1]
    chunk_pages = n_pages // 2
    nb = n_pages * PAGE_SIZE // MOBA_BLOCK
    assert nb <= LANES and chunk_pages % (MOBA_BLOCK // PAGE_SIZE) == 0
    steps = 2 * (n_pages // chunk_pages)
    seq3 = lambda width_rows: pl.BlockSpec((1, width_rows, KV_DIM), lambda i, c, pt: (i, 0, 0))
    kern = functools.partial(_moba_sample_kernel, n_pages=n_pages, chunk_pages=chunk_pages)
    return pl.pallas_call(
        kern,
        grid_spec=pltpu.PrefetchScalarGridSpec(
            num_scalar_prefetch=1,
            grid=(db, steps),
            in_specs=[seq3(rows), seq3(qs_bd.shape[1]), seq3(ds), seq3(ds),
                      pl.BlockSpec(memory_space=pl.ANY), pl.BlockSpec(memory_space=pl.ANY)],
            out_specs=pl.BlockSpec((1, rows, HEAD_DIM), lambda i, c, pt: (i, 0, 0)),
            scratch_shapes=[
                pltpu.VMEM((2, chunk_pages, PAGE_SIZE, KV_DIM), F32),
                pltpu.SemaphoreType.DMA((2,)),
                pltpu.VMEM((nb, rows, MOBA_BLOCK), F32),
                pltpu.VMEM((LANES, KV_DIM), F32),
                pltpu.VMEM((rows, 2 * LANES), F32),
                pltpu.VMEM((rows, KV_DIM), F32),
            ],
        ),
        out_shape=jax.ShapeDtypeStruct((db, rows, HEAD_DIM), F32),
        compiler_params=_params("arbitrary", "arbitrary"),
        name="moba_sample",
    )(page_table, qbd, qs_bd, k_new, v_new, kpool, vpool)


def _swa_sample_kernel(qbd_ref, sink_ref, kb_ref, vb_ref, kn_ref, vn_ref, o_ref):
    rows = qbd_ref.shape[1]
    wb = kb_ref.shape[1]
    ds = kn_ref.shape[1]
    sink = sink_ref[...]
    qrow_b = lax.broadcasted_iota(I32, (rows, wb), 0) // N_HEADS
    buf_ok = lax.broadcasted_iota(I32, (rows, wb), 1) >= qrow_b
    qrow_n = lax.broadcasted_iota(I32, (rows, ds), 0) // N_HEADS
    new_ok = lax.broadcasted_iota(I32, (rows, ds), 1) <= qrow_n
    for s in range(qbd_ref.shape[0]):
        qbd = qbd_ref[s]
        sb = jnp.where(buf_ok, _dot_nt(qbd, kb_ref[s].astype(BF16)), NEG_INF)
        sn = jnp.where(new_ok, _dot_nt(qbd, kn_ref[s].astype(BF16)), NEG_INF)
        m = jnp.maximum(jnp.maximum(jnp.max(sb, axis=-1, keepdims=True),
                                    jnp.max(sn, axis=-1, keepdims=True)), sink)
        pb = jnp.exp(sb - m)
        pn = jnp.exp(sn - m)
        den = (jnp.sum(pb, axis=-1, keepdims=True) + jnp.sum(pn, axis=-1, keepdims=True)
               + jnp.exp(sink - m))
        acc = _dot(pb.astype(BF16), vb_ref[s].astype(BF16)) + _dot(pn.astype(BF16), vn_ref[s].astype(BF16))
        o_ref[s] = _fold_heads(acc / den)


def _swa_sample(qbd, sink_rows, buf_k, buf_v, k_new, v_new, seqs_per_step=8):
    db, rows, _ = qbd.shape
    wb = buf_k.shape[1]
    ds = k_new.shape[1]
    spec = lambda r: pl.BlockSpec((seqs_per_step, r, KV_DIM), lambda i: (i, 0, 0))
    return pl.pallas_call(
        _swa_sample_kernel,
        grid=(db // seqs_per_step,),
        in_specs=[spec(rows), pl.BlockSpec((rows, 1), lambda i: (0, 0)),
                  spec(wb), spec(wb), spec(ds), spec(ds)],
        out_specs=pl.BlockSpec((seqs_per_step, rows, HEAD_DIM), lambda i: (i, 0, 0)),
        out_shape=jax.ShapeDtypeStruct((db, rows, HEAD_DIM), F32),
        compiler_params=_params("arbitrary"),
        name="swa_sample",
    )(qbd, sink_rows, buf_k, buf_v, k_new, v_new)


def _attn_out_kernel(a_ref, x_ref, g_ref, sh_ref, sc_ref, wo_ref, bo_ref, lg_ref, lb_ref,
                     wrh_ref, wrl_ref, xn_ref, h_ref, lt_ref):
    o = _dot(a_ref[...], wo_ref[...]) + bo_ref[...]
    xn = _layer_norm(ALPHA * x_ref[...] + g_ref[0] * o, lg_ref[...], lb_ref[...])
    xn_ref[...] = xn
    h = xn * (1.0 + sc_ref[0]) + sh_ref[0]
    h_hi, h_lo = _split_bf16(h)
    h_ref[...] = h_hi
    wrh = wrh_ref[...]
    lt_ref[...] = _dot_nt(wrh, h_hi) + _dot_nt(wrh, h_lo) + _dot_nt(wrl_ref[...], h_hi)


def _attn_out(a, x, gate, shift, scale, wo_bf, bo, ln_g, ln_b, wr_hi, wr_lo, tm):
    n = x.shape[0]
    per_mod = n // tm // gate.shape[0]
    mod_rows = gate.shape[1]
    row = pl.BlockSpec((tm, D_MODEL), lambda i: (i, 0))
    mod = pl.BlockSpec((1, mod_rows, D_MODEL), lambda i: (i // per_mod, 0, 0))
    vec = pl.BlockSpec((1, D_MODEL), lambda i: (0, 0))
    full = lambda r, c: pl.BlockSpec((r, c), lambda i: (0, 0))
    return pl.pallas_call(
        _attn_out_kernel,
        grid=(n // tm,),
        in_specs=[row, row, mod, mod, mod, full(Q_DIM, D_MODEL), vec, vec, vec,
                  full(N_EXPERTS, D_MODEL), full(N_EXPERTS, D_MODEL)],
        out_specs=[row, row, pl.BlockSpec((N_EXPERTS, tm), lambda i: (0, i))],
        out_shape=[jax.ShapeDtypeStruct((n, D_MODEL), F32),
                   jax.ShapeDtypeStruct((n, D_MODEL), BF16),
                   jax.ShapeDtypeStruct((N_EXPERTS, n), F32)],
        compiler_params=_params("arbitrary"),
        name="attn_out_norm",
    )(a, x, gate, shift, scale, wo_bf, bo.reshape(1, -1), ln_g.reshape(1, -1), ln_b.reshape(1, -1),
      wr_hi, wr_lo)


def _first_max(x, idx):
    top = jnp.max(x, axis=0, keepdims=True)
    first = jnp.min(jnp.where(x == top, idx, BIG_INDEX), axis=0, keepdims=True)
    return top, first


def _route_kernel(lt_ref, b_ref, idx_ref, w_ref, pos_ref, cnt_ref):
    i = pl.program_id(0)
    tn = lt_ref.shape[1]
    scores = _sigmoid(lt_ref[...])
    biased = scores + b_ref[...]
    e_idx = lax.broadcasted_iota(I32, (N_EXPERTS, tn), 0).astype(F32)
    l_idx = lax.broadcasted_iota(I32, (PER_GROUP, tn), 0).astype(F32)
    g_score = []
    for g in range(N_EXPERT_GROUPS):
        x = biased[g * PER_GROUP:(g + 1) * PER_GROUP]
        top1, first = _first_max(x, l_idx)
        top2 = jnp.max(jnp.where(l_idx == first, NEG_INF, x), axis=0, keepdims=True)
        g_score.append(top1 + top2)
    pieces = []
    for g in range(N_EXPERT_GROUPS):
        ahead = jnp.zeros((1, tn), F32)
        for o in range(N_EXPERT_GROUPS):
            if o != g:
                wins = (g_score[o] >= g_score[g]) if o < g else (g_score[o] > g_score[g])
                ahead = ahead + jnp.where(wins, 1.0, 0.0)
        pieces.append(jnp.where(ahead < TOPK_GROUPS, biased[g * PER_GROUP:(g + 1) * PER_GROUP], NEG_INF))
    masked = jnp.concatenate(pieces, axis=0)
    chosen = jnp.zeros((N_EXPERTS, tn), F32)
    firsts, picked = [], []
    for _ in range(TOP_K):
        _, first = _first_max(masked, e_idx)
        hit = e_idx == first
        firsts.append(first)
        picked.append(jnp.sum(jnp.where(hit, scores, 0.0), axis=0, keepdims=True))
        chosen = jnp.where(hit, 1.0, chosen)
        masked = jnp.where(hit, NEG_INF, masked)
    total = picked[0]
    for s in picked[1:]:
        total = total + s
    for r in range(TOP_K):
        idx_ref[r:r + 1, :] = firsts[r].astype(I32)
        w_ref[r:r + 1, :] = picked[r] / total * ROUTED_SCALE

    @pl.when(i == 0)
    def _():
        cnt_ref[...] = jnp.zeros(cnt_ref.shape, F32)
    r = lax.broadcasted_iota(I32, (tn, tn), 0)
    c = lax.broadcasted_iota(I32, (tn, tn), 1)
    before = jnp.where(r < c, 1.0, 0.0).astype(BF16)
    prior = _dot(chosen.astype(BF16), before) + cnt_ref[...]
    for r in range(TOP_K):
        pos_ref[r:r + 1, :] = jnp.sum(jnp.where(e_idx == firsts[r], prior, 0.0), axis=0,
                                      keepdims=True).astype(I32)
    cnt_ref[...] = cnt_ref[...] + jnp.sum(chosen, axis=1, keepdims=True)


def _route(logits_t, b_router, tn=512):
    n = logits_t.shape[1]
    slot = pl.BlockSpec((TOP_K, tn), lambda i: (0, i))
    return pl.pallas_call(
        _route_kernel,
        grid=(n // tn,),
        in_specs=[pl.BlockSpec((N_EXPERTS, tn), lambda i: (0, i)),
                  pl.BlockSpec((N_EXPERTS, 1), lambda i: (0, 0))],
        out_specs=[slot, slot, slot, pl.BlockSpec((N_EXPERTS, 1), lambda i: (0, 0))],
        out_shape=[jax.ShapeDtypeStruct((TOP_K, n), I32),
                   jax.ShapeDtypeStruct((TOP_K, n), F32),
                   jax.ShapeDtypeStruct((TOP_K, n), I32),
                   jax.ShapeDtypeStruct((N_EXPERTS, 1), F32)],
        compiler_params=_params("arbitrary"),
        name="route",
    )(logits_t, b_router.reshape(N_EXPERTS, 1))


def _moe_kernel(be_ref, nu_ref, x_ref, wg_ref, wu_ref, wd_ref, o_ref, wg_bf, wu_bf, wd_bf):
    i = pl.program_id(0)
    used = i < nu_ref[0]
    changed = (i == 0) | (be_ref[i] != be_ref[jnp.maximum(i - 1, 0)])

    @pl.when(used & changed)
    def _():
        wg_bf[...] = wg_ref[0].astype(BF16)
        wu_bf[...] = wu_ref[0].astype(BF16)
        wd_bf[...] = wd_ref[0].astype(BF16)

    @pl.when(used)
    def _():
        x = x_ref[...]
        gate = _dot(x, wg_bf[...])
        up = _dot(x, wu_bf[...])
        act = gate * _sigmoid(gate) * up
        o_ref[...] = _dot(act.astype(BF16), wd_bf[...])

    @pl.when(jnp.logical_not(used))
    def _():
        o_ref[...] = jnp.zeros(o_ref.shape, F32)


def _moe_experts(blk_e, n_used, x_sorted, w_gate, w_up, w_down):
    rows = x_sorted.shape[0]
    n_blocks = rows // MOE_ROWS
    x_map = lambda i, be, nu: (jnp.minimum(i, nu[0] - 1), 0)
    w_map = lambda i, be, nu: (be[i], 0, 0)
    return pl.pallas_call(
        _moe_kernel,
        grid_spec=pltpu.PrefetchScalarGridSpec(
            num_scalar_prefetch=2,
            grid=(n_blocks,),
            in_specs=[pl.BlockSpec((MOE_ROWS, D_MODEL), x_map),
                      pl.BlockSpec((1, D_MODEL, D_EXPERT), w_map),
                      pl.BlockSpec((1, D_MODEL, D_EXPERT), w_map),
                      pl.BlockSpec((1, D_EXPERT, D_MODEL), w_map)],
            out_specs=pl.BlockSpec((MOE_ROWS, D_MODEL), lambda i, be, nu: (i, 0)),
            scratch_shapes=[pltpu.VMEM((D_MODEL, D_EXPERT), BF16),
                            pltpu.VMEM((D_MODEL, D_EXPERT), BF16),
                            pltpu.VMEM((D_EXPERT, D_MODEL), BF16)],
        ),
        out_shape=jax.ShapeDtypeStruct((rows, D_MODEL), F32),
        compiler_params=_params("arbitrary"),
        name="moe_experts",
    )(blk_e, n_used, x_sorted, w_gate, w_up, w_down)


def _ffn_out_kernel(h_ref, r_ref, x_ref, g_ref, wsg_ref, wsu_ref, wsd_ref, lg_ref, lb_ref, o_ref):
    h = h_ref[...]
    gate = _dot(h, wsg_ref[...])
    up = _dot(h, wsu_ref[...])
    shared = _dot((gate * _sigmoid(gate) * up).astype(BF16), wsd_ref[...])
    y = ALPHA * x_ref[...] + g_ref[0] * (r_ref[...] + shared)
    o_ref[...] = _layer_norm(y, lg_ref[...], lb_ref[...])


def _ffn_out(h, routed, x, gate, wsg, wsu, wsd, ln_g, ln_b, tm):
    n = x.shape[0]
    per_mod = n // tm // gate.shape[0]
    row = pl.BlockSpec((tm, D_MODEL), lambda i: (i, 0))
    mod = pl.BlockSpec((1, gate.shape[1], D_MODEL), lambda i: (i // per_mod, 0, 0))
    vec = pl.BlockSpec((1, D_MODEL), lambda i: (0, 0))
    full = lambda a: pl.BlockSpec(a.shape, lambda i: (0, 0))
    return pl.pallas_call(
        _ffn_out_kernel,
        grid=(n // tm,),
        in_specs=[row, row, row, mod, full(wsg), full(wsu), full(wsd), vec, vec],
        out_specs=row,
        out_shape=jax.ShapeDtypeStruct((n, D_MODEL), F32),
        compiler_params=_params("arbitrary"),
        name="ffn_out_norm",
    )(h, routed, x, gate, wsg, wsu, wsd, ln_g.reshape(1, -1), ln_b.reshape(1, -1))


def _dispatch(idx_t, pos_t, counts):
    n = idx_t.shape[1]
    counts = counts.reshape(N_EXPERTS).astype(I32)
    padded = (counts + MOE_ROWS - 1) // MOE_ROWS * MOE_ROWS
    pad_end = jnp.cumsum(padded)
    pad_start = pad_end - padded
    dest = pad_start[idx_t] + pos_t
    n_blocks = (n * TOP_K + N_EXPERTS * (MOE_ROWS - 1)) // MOE_ROWS
    rows = n_blocks * MOE_ROWS
    tok = jnp.broadcast_to(jnp.arange(n, dtype=I32)[None, :], dest.shape)
    row_tok = jnp.zeros((rows,), I32).at[dest.reshape(-1)].set(tok.reshape(-1), unique_indices=True)
    n_used = pad_end[-1] // MOE_ROWS
    blk = jnp.minimum(jnp.arange(n_blocks, dtype=I32), n_used - 1) * MOE_ROWS
    blk_e = jnp.minimum(jnp.searchsorted(pad_end, blk, side="right"), N_EXPERTS - 1).astype(I32)
    return dest, row_tok, blk_e, n_used.reshape(1).astype(I32)


def _head_major(x, heads):
    b, t, _ = x.shape
    return x.reshape(b, t, heads, HEAD_DIM).transpose(0, 2, 1, 3)


def _block_diag_rows(x, heads_per_kv):
    db, ds, heads, _ = x.shape
    kv_of_head = jnp.arange(heads) // heads_per_kv
    onehot = (kv_of_head[:, None] == jnp.arange(KV_HEADS)[None, :]).astype(x.dtype)
    out = x[:, :, :, None, :] * onehot[None, None, :, :, None]
    return out.reshape(db, ds * heads, KV_DIM)


def kernel(x_prompt, x_sample, c_prompt, c_sample, cache_moba_k, cache_moba_v, state_swa_k, state_swa_v,
           page_table, w_ada, b_ada, w_qkv, b_qkv, attn_sinks, w_o, b_o, ln_attn_g, ln_attn_b,
           w_router, b_router, w_exp_gate, w_exp_up, w_exp_down, w_sh_gate, w_sh_up, w_sh_down,
           ln_ffn_g, ln_ffn_b):
    bsz, seq, d = x_prompt.shape
    db, ds, _ = x_sample.shape
    n_p, n_s = bsz * seq, db * ds
    past_len = page_table.shape[1] * PAGE_SIZE
    tm_p, tm_s = 512, n_s
    depth = w_ada.shape[0]

    mods = _ada_mod(jnp.concatenate([c_prompt, c_sample], 0), w_ada, b_ada)
    tabs_p = _rope_tables(jnp.arange(seq, dtype=I32))
    tabs_s = _rope_tables(jnp.tile(past_len + jnp.arange(ds, dtype=I32), db))

    xp = x_prompt.reshape(n_p, d)
    xs = x_sample.reshape(n_s, d)
    outs = {k: [] for k in ("mkp", "mvp", "mks", "mvs", "skp", "svp", "sks", "svs")}
    for i in range(depth):
        j = i // 2
        m = mods[i].reshape(bsz + db, N_MOD, d)
        mp = [m[:bsz, t][:, None, :] for t in range(N_MOD)]
        ms = [jnp.repeat(m[bsz:, t], ds, axis=0)[None] for t in range(N_MOD)]
        w_qkv_bf = w_qkv[i].astype(BF16)
        moba = i % 2 == 0
        res_p = _qkv_project(xp, mp[0], mp[1], w_qkv_bf, b_qkv[i], tabs_p, tm_p, moba)
        res_s = _qkv_project(xs, ms[0], ms[1], w_qkv_bf, b_qkv[i], tabs_s, tm_s, moba)
        qp, kp, vp, kp_bf, vp_bf = res_p[:5]
        qs, ks, vs = res_s[:3]
        q_hm = _head_major(qp.reshape(bsz, seq, Q_DIM), N_HEADS)
        k_hm = _head_major(kp_bf.reshape(bsz, seq, KV_DIM), KV_HEADS)
        v_hm = _head_major(vp_bf.reshape(bsz, seq, KV_DIM), KV_HEADS)
        qbd = _block_diag_rows(qs.reshape(db, ds, N_HEADS, HEAD_DIM), GROUP)
        pad_rows = ((0, 0), (0, LANES - ds), (0, 0))
        ks3 = jnp.pad(ks.reshape(db, ds, KV_DIM), pad_rows)
        vs3 = jnp.pad(vs.reshape(db, ds, KV_DIM), pad_rows)
        kp5 = kp.reshape(bsz, seq, KV_HEADS, HEAD_DIM)
        vp5 = vp.reshape(bsz, seq, KV_HEADS, HEAD_DIM)
        ks5 = ks.reshape(db, ds, KV_HEADS, HEAD_DIM)
        vs5 = vs.reshape(db, ds, KV_HEADS, HEAD_DIM)
        if moba:
            sel = _moba_select(res_p[5].reshape(bsz, seq, KV_DIM), kp.reshape(bsz, seq, KV_DIM))
            a_hm = _moba_prompt(q_hm, k_hm, v_hm, sel.transpose(0, 1, 3, 2))
            qs_bd = _block_diag_rows(res_s[5].reshape(db, ds, KV_HEADS, HEAD_DIM), 1)
            n_pool = cache_moba_k.shape[1]
            a_s = _moba_sample(page_table, qbd, qs_bd, ks3, vs3,
                               cache_moba_k[j].reshape(n_pool, PAGE_SIZE, KV_DIM),
                               cache_moba_v[j].reshape(n_pool, PAGE_SIZE, KV_DIM))
            outs["mkp"].append(kp5)
            outs["mvp"].append(vp5)
            outs["mks"].append(ks5)
            outs["mvs"].append(vs5)
        else:
            a_hm = _swa_prompt(q_hm, k_hm, v_hm, attn_sinks[j])
            wb = state_swa_k.shape[2]
            buf_k = state_swa_k[j].reshape(db, wb, KV_DIM)
            buf_v = state_swa_v[j].reshape(db, wb, KV_DIM)
            sink_rows = jnp.tile(attn_sinks[j], ds).reshape(ds * N_HEADS, 1)
            a_s = _swa_sample(qbd, sink_rows, buf_k, buf_v, ks3, vs3)
            wbp = min(WINDOW, seq)
            outs["skp"].append(kp5[:, -wbp:])
            outs["svp"].append(vp5[:, -wbp:])
            outs["sks"].append(jnp.concatenate([state_swa_k[j], ks5], axis=1)[:, -wb:])
            outs["svs"].append(jnp.concatenate([state_swa_v[j], vs5], axis=1)[:, -wb:])
        a_p = a_hm.transpose(0, 2, 1, 3).reshape(n_p, Q_DIM)
        a_s = a_s.reshape(n_s, Q_DIM).astype(BF16)

        wo_bf = w_o[i].astype(BF16)
        wr_hi, wr_lo = _split_bf16(w_router[i].T)
        args = (wo_bf, b_o[i], ln_attn_g[i], ln_attn_b[i], wr_hi, wr_lo)
        xp, hp, ltp = _attn_out(a_p, xp, mp[2], mp[3], mp[4], *args, 256)
        xs, hs, lts = _attn_out(a_s, xs, ms[2], ms[3], ms[4], *args, n_s)

        h_all = jnp.concatenate([hp, hs], axis=0)
        idx_t, w_t, pos_t, counts = _route(jnp.concatenate([ltp, lts], axis=1), b_router[i])
        dest, row_tok, blk_e, n_used = _dispatch(idx_t, pos_t, counts)
        y_sorted = _moe_experts(blk_e, n_used, h_all[row_tok], w_exp_gate[i], w_exp_up[i], w_exp_down[i])
        routed = jnp.sum(y_sorted[dest] * w_t[:, :, None], axis=0)

        ws = (w_sh_gate[i].astype(BF16), w_sh_up[i].astype(BF16), w_sh_down[i].astype(BF16))
        xp = _ffn_out(hp, routed[:n_p], xp, mp[5], *ws, ln_ffn_g[i], ln_ffn_b[i], 256)
        xs = _ffn_out(hs, routed[n_p:], xs, ms[5], *ws, ln_ffn_g[i], ln_ffn_b[i], n_s)

    st = lambda key: jnp.stack(outs[key])
    return (xp.reshape(bsz, seq, d), xs.reshape(db, ds, d), st("mkp"), st("mvp"), st("mks"), st("mvs"),
            st("skp"), st("svp"), st("sks"), st("svs"))
```

```python
import functools

import jax
import jax.numpy as jnp
from jax import lax
from jax.experimental import pallas as pl
from jax.experimental.pallas import tpu as pltpu

F32 = jnp.float32
BF16 = jnp.bfloat16
I32 = jnp.int32

D_MODEL = 1024
N_HEADS = 16
HEAD_DIM = 64
KV_HEADS = 4
GROUP = N_HEADS // KV_HEADS
Q_DIM = N_HEADS * HEAD_DIM
KV_DIM = KV_HEADS * HEAD_DIM
QKV_DIM = Q_DIM + 2 * KV_DIM
ATTN_SCALE = HEAD_DIM ** -0.5
ROT_DIM = HEAD_DIM // 4
ROPE_THETA = 500000.0
PAGE_SIZE = 128
MOBA_BLOCK = 256
MOBA_TOPK = 3
WINDOW = 128
N_EXPERTS = 256
TOP_K = 8
N_EXPERT_GROUPS = 8
TOPK_GROUPS = 4
PER_GROUP = N_EXPERTS // N_EXPERT_GROUPS
D_EXPERT = 256
ROUTED_SCALE = 2.5
N_MOD = 6
DEPTH = 2
ALPHA = (2 * DEPTH) ** 0.25
LN_EPS = 1e-5

LANES = 128
VMEM_LIMIT = 48 * 1024 * 1024
MOE_ROWS = 256
NEG_INF = float("-inf")
BIG_INDEX = 1e9

_NT = (((1,), (1,)), ((), ()))


def _dot(a, b):
    return jnp.dot(a, b, preferred_element_type=F32)


def _dot_nt(a, b):
    return lax.dot_general(a, b, _NT, preferred_element_type=F32)


def _split_bf16(x):
    hi = x.astype(BF16)
    lo = (x - hi.astype(F32)).astype(BF16)
    return hi, lo


def _sigmoid(x):
    return 1.0 / (1.0 + jnp.exp(-x))


def _params(*sem):
    return pltpu.CompilerParams(dimension_semantics=sem, vmem_limit_bytes=VMEM_LIMIT)


def _layer_norm(y, g, b):
    mu = jnp.mean(y, axis=-1, keepdims=True)
    yc = y - mu
    var = jnp.mean(yc * yc, axis=-1, keepdims=True)
    return yc * lax.rsqrt(var + LN_EPS) * g + b


def _ada_kernel(c_ref, w_ref, b_ref, o_ref):
    c = c_ref[...]
    a_hi, a_lo = _split_bf16(c * _sigmoid(c))
    w_hi, w_lo = _split_bf16(w_ref[0])
    o_ref[0] = _dot(a_hi, w_hi) + _dot(a_lo, w_hi) + _dot(a_hi, w_lo) + b_ref[0]


def _ada_mod(c_all, w_ada, b_ada):
    rows = c_all.shape[0]
    depth = w_ada.shape[0]
    return pl.pallas_call(
        _ada_kernel,
        grid=(depth, N_MOD),
        in_specs=[
            pl.BlockSpec((rows, D_MODEL), lambda l, j: (0, 0)),
            pl.BlockSpec((1, D_MODEL, D_MODEL), lambda l, j: (l, 0, j)),
            pl.BlockSpec((1, 1, D_MODEL), lambda l, j: (l, 0, j)),
        ],
        out_specs=pl.BlockSpec((1, rows, D_MODEL), lambda l, j: (l, 0, j)),
        out_shape=jax.ShapeDtypeStruct((depth, rows, N_MOD * D_MODEL), F32),
        compiler_params=_params("arbitrary", "arbitrary"),
        name="ada_mod",
    )(c_all, w_ada, b_ada.reshape(depth, 1, N_MOD * D_MODEL))


def _qkv_kernel(x_ref, sh_ref, sc_ref, w_ref, b_ref, c_ref, s1_ref, s2_ref,
                q_ref, k_ref, v_ref, kb_ref, vb_ref, *qs_ref):
    u = x_ref[...] * (1.0 + sc_ref[0]) + sh_ref[0]
    qkv = _dot(u.astype(BF16), w_ref[...]) + b_ref[...]
    cos, s1, s2 = c_ref[...], s1_ref[...], s2_ref[...]
    rots = []
    for j in range((Q_DIM + KV_DIM) // LANES):
        blk = qkv[:, j * LANES:(j + 1) * LANES]
        rots.append(blk * cos + pltpu.roll(blk, LANES - ROT_DIM // 2, 1) * s1
                    + pltpu.roll(blk, ROT_DIM // 2, 1) * s2)
    nq = Q_DIM // LANES
    for j in range(nq):
        q_ref[:, j * LANES:(j + 1) * LANES] = (rots[j] * ATTN_SCALE).astype(BF16)
    for j in range(KV_DIM // LANES):
        k_ref[:, j * LANES:(j + 1) * LANES] = rots[nq + j]
        kb_ref[:, j * LANES:(j + 1) * LANES] = rots[nq + j].astype(BF16)
    v = qkv[:, Q_DIM + KV_DIM:]
    v_ref[...] = v
    vb_ref[...] = v.astype(BF16)
    if qs_ref:
        lane = lax.broadcasted_iota(I32, rots[0].shape, 1)
        halves = []
        for kv in range(KV_HEADS):
            t = rots[2 * kv] + rots[2 * kv + 1]
            halves.append(t + pltpu.roll(t, HEAD_DIM, 1))
        for j in range(KV_DIM // LANES):
            qs_ref[0][:, j * LANES:(j + 1) * LANES] = jnp.where(
                lane < HEAD_DIM, halves[2 * j], halves[2 * j + 1])


def _qkv_project(x, shift, scale, w_bf, b, tabs, tm, with_qsum):
    n = x.shape[0]
    n_tab = tabs[0].shape[0] // tm
    per_mod = n // tm // shift.shape[0]
    mod_rows = shift.shape[1]
    row_spec = lambda width: pl.BlockSpec((tm, width), lambda i: (i, 0))
    mod_spec = pl.BlockSpec((1, mod_rows, D_MODEL), lambda i: (i // per_mod, 0, 0))
    tab_spec = pl.BlockSpec((tm, LANES), lambda i: (i % n_tab, 0))
    out_shape = [
        jax.ShapeDtypeStruct((n, Q_DIM), BF16),
        jax.ShapeDtypeStruct((n, KV_DIM), F32),
        jax.ShapeDtypeStruct((n, KV_DIM), F32),
        jax.ShapeDtypeStruct((n, KV_DIM), BF16),
        jax.ShapeDtypeStruct((n, KV_DIM), BF16),
    ]
    out_specs = [row_spec(Q_DIM)] + [row_spec(KV_DIM)] * 4
    if with_qsum:
        out_shape.append(jax.ShapeDtypeStruct((n, KV_DIM), F32))
        out_specs.append(row_spec(KV_DIM))
    return pl.pallas_call(
        _qkv_kernel,
        grid=(n // tm,),
        in_specs=[
            row_spec(D_MODEL), mod_spec, mod_spec,
            pl.BlockSpec((D_MODEL, QKV_DIM), lambda i: (0, 0)),
            pl.BlockSpec((1, QKV_DIM), lambda i: (0, 0)),
            tab_spec, tab_spec, tab_spec,
        ],
        out_specs=out_specs,
        out_shape=out_shape,
        compiler_params=_params("arbitrary"),
        name="qkv_rope",
    )(x, shift, scale, w_bf, b.reshape(1, QKV_DIM), *tabs)


def _rope_tables(pos):
    half = ROT_DIM // 2
    inv_freq = 1.0 / (ROPE_THETA ** (jnp.arange(0, ROT_DIM, 2, dtype=F32) / ROT_DIM))
    ang = pos.astype(F32)[:, None] * inv_freq[None, :]
    cos, sin = jnp.cos(ang), jnp.sin(ang)
    rest = HEAD_DIM - ROT_DIM
    ones = jnp.ones((pos.shape[0], rest), F32)
    zeros = jnp.zeros((pos.shape[0], rest), F32)
    zh = jnp.zeros_like(sin)
    c = jnp.concatenate([cos, cos, ones], -1)
    s1 = jnp.concatenate([-sin, zh, zeros], -1)
    s2 = jnp.concatenate([zh, sin, zeros], -1)
    reps = LANES // HEAD_DIM
    return tuple(jnp.tile(t, (1, reps)) for t in (c, s1, s2))


def _moba_select_kernel(qs_ref, k_ref, sel_ref):
    t_len = k_ref.shape[1]
    nb = t_len // MOBA_BLOCK
    k = k_ref[0]
    nb_pad = 16
    means = jnp.concatenate(
        [jnp.sum(k[n * MOBA_BLOCK:(n + 1) * MOBA_BLOCK], axis=0, keepdims=True) / MOBA_BLOCK
         for n in range(nb)] + [jnp.zeros((nb_pad - nb, KV_DIM), F32)], axis=0)
    lane_head = lax.broadcasted_iota(I32, (nb_pad, KV_DIM), 1) // HEAD_DIM
    q_hi, q_lo = _split_bf16(qs_ref[0])
    pos_blk = lax.broadcasted_iota(I32, (nb, t_len), 1) // MOBA_BLOCK
    blk = lax.broadcasted_iota(I32, (nb, t_len), 0)
    past = blk < pos_blk
    for kv in range(KV_HEADS):
        m_hi, m_lo = _split_bf16(jnp.where(lane_head == kv, means, 0.0))
        gate = (_dot_nt(m_hi, q_hi) + _dot_nt(m_lo, q_hi) + _dot_nt(m_hi, q_lo))[:nb]
        gate = jnp.where(past, gate, NEG_INF)
        rank = jnp.zeros((nb, t_len), F32)
        for m in range(nb):
            other = gate[m:m + 1, :]
            ahead = (other > gate) | ((other == gate) & (blk > m))
            rank = rank + jnp.where(ahead, 1.0, 0.0)
        sel_ref[0, kv] = jnp.where(past & (rank < MOBA_TOPK), 1.0, 0.0)


def _moba_select(qsum, k):
    b, t, _ = k.shape
    nb = t // MOBA_BLOCK
    spec = pl.BlockSpec((1, t, KV_DIM), lambda i: (i, 0, 0))
    return pl.pallas_call(
        _moba_select_kernel,
        grid=(b,),
        in_specs=[spec, spec],
        out_specs=pl.BlockSpec((1, KV_HEADS, nb, t), lambda i: (i, 0, 0, 0)),
        out_shape=jax.ShapeDtypeStruct((b, KV_HEADS, nb, t), F32),
        compiler_params=_params("arbitrary"),
        name="moba_select",
    )(qsum, k)


def _moba_prompt_kernel(q_ref, k_ref, v_ref, sel_ref, o_ref):
    qi = pl.program_id(2)
    tq = MOBA_BLOCK
    rows = GROUP * tq
    q = q_ref[0].reshape(rows, HEAD_DIM)
    sel = sel_ref[0, 0]

    def scores(n):
        kb = k_ref[0, 0, pl.ds(pl.multiple_of(n * tq, tq), tq), :]
        return _dot_nt(q, kb)

    def values(n):
        return v_ref[0, 0, pl.ds(pl.multiple_of(n * tq, tq), tq), :]

    r = lax.broadcasted_iota(I32, (rows, tq), 0) & (tq - 1)
    c = lax.broadcasted_iota(I32, (rows, tq), 1)
    s = jnp.where(c <= r, scores(qi), NEG_INF)
    m0 = jnp.max(s, axis=-1, keepdims=True)
    p = jnp.exp(s - m0)
    l0 = jnp.sum(p, axis=-1, keepdims=True)
    acc0 = _dot(p.astype(BF16), values(qi))

    nb_lane = lax.broadcasted_iota(I32, sel.shape, 1)

    def body(n, carry):
        m, l, acc = carry
        col = jnp.sum(jnp.where(nb_lane == n, sel, 0.0), axis=-1, keepdims=True)
        keep = jnp.concatenate([col] * GROUP, axis=0) > 0.0
        s = jnp.where(keep, scores(n), NEG_INF)
        m_new = jnp.maximum(m, jnp.max(s, axis=-1, keepdims=True))
        alpha = jnp.exp(m - m_new)
        p = jnp.exp(s - m_new)
        l = alpha * l + jnp.sum(p, axis=-1, keepdims=True)
        acc = alpha * acc + _dot(p.astype(BF16), values(n))
        return m_new, l, acc

    _, l, acc = lax.fori_loop(0, qi, body, (m0, l0, acc0))
    o_ref[0] = (acc / l).astype(BF16).reshape(GROUP, tq, HEAD_DIM)


def _moba_prompt(q_hm, k_hm, v_hm, sel_t):
    b, _, t, _ = q_hm.shape
    nb = t // MOBA_BLOCK
    kv_spec = pl.BlockSpec((1, 1, t, HEAD_DIM), lambda i, j, n: (i, j, 0, 0))
    q_spec = pl.BlockSpec((1, GROUP, MOBA_BLOCK, HEAD_DIM), lambda i, j, n: (i, j, n, 0))
    return pl.pallas_call(
        _moba_prompt_kernel,
        grid=(b, KV_HEADS, nb),
        in_specs=[q_spec, kv_spec, kv_spec,
                  pl.BlockSpec((1, 1, MOBA_BLOCK, nb), lambda i, j, n: (i, j, n, 0))],
        out_specs=q_spec,
        out_shape=jax.ShapeDtypeStruct(q_hm.shape, BF16),
        compiler_params=_params("arbitrary", "arbitrary", "arbitrary"),
        name="moba_prompt",
    )(q_hm, k_hm, v_hm, sel_t)


def _swa_prompt_kernel(sink_ref, q_ref, kp_ref, k_ref, vp_ref, v_ref, o_ref):
    j = pl.program_id(1)
    n = pl.program_id(2)
    tq = WINDOW
    r = lax.broadcasted_iota(I32, (tq, tq), 0)
    c = lax.broadcasted_iota(I32, (tq, tq), 1)
    prev_ok = c >= r
    own_ok = c <= r
    first_block = jnp.where(n > 0, 0.0, NEG_INF)
    kp, k, vp, v = kp_ref[0, 0], k_ref[0, 0], vp_ref[0, 0], v_ref[0, 0]
    for g in range(GROUP):
        q = q_ref[0, g]
        sink = sink_ref[j * GROUP + g]
        sp = jnp.where(prev_ok, _dot_nt(q, kp) + first_block, NEG_INF)
        so = jnp.where(own_ok, _dot_nt(q, k), NEG_INF)
        m = jnp.maximum(jnp.maximum(jnp.max(sp, axis=-1, keepdims=True),
                                    jnp.max(so, axis=-1, keepdims=True)), sink)
        pp = jnp.exp(sp - m)
        po = jnp.exp(so - m)
        den = (jnp.sum(pp, axis=-1, keepdims=True) + jnp.sum(po, axis=-1, keepdims=True)
               + jnp.exp(sink - m))
        o = _dot(pp.astype(BF16), vp) + _dot(po.astype(BF16), v)
        o_ref[0, g] = (o / den).astype(BF16)


def _swa_prompt(q_hm, k_hm, v_hm, sinks):
    b, _, t, _ = q_hm.shape
    nb = t // WINDOW
    q_spec = pl.BlockSpec((1, GROUP, WINDOW, HEAD_DIM), lambda i, j, n, s: (i, j, n, 0))
    own = pl.BlockSpec((1, 1, WINDOW, HEAD_DIM), lambda i, j, n, s: (i, j, n, 0))
    prev = pl.BlockSpec((1, 1, WINDOW, HEAD_DIM), lambda i, j, n, s: (i, j, jnp.maximum(n - 1, 0), 0))
    return pl.pallas_call(
        _swa_prompt_kernel,
        grid_spec=pltpu.PrefetchScalarGridSpec(
            num_scalar_prefetch=1,
            grid=(b, KV_HEADS, nb),
            in_specs=[q_spec, prev, own, prev, own],
            out_specs=q_spec,
        ),
        out_shape=jax.ShapeDtypeStruct(q_hm.shape, BF16),
        compiler_params=_params("arbitrary", "arbitrary", "arbitrary"),
        name="swa_prompt",
    )(sinks, q_hm, k_hm, k_hm, v_hm, v_hm)


def _fold_heads(acc):
    rows = acc.shape[0]
    lane_head = lax.broadcasted_iota(I32, (rows, KV_DIM), 1) // HEAD_DIM
    row_head = (lax.broadcasted_iota(I32, (rows, KV_DIM), 0) // GROUP) % KV_HEADS
    a = jnp.where(lane_head == row_head, acc, 0.0)
    a = a[:, :LANES] + a[:, LANES:]
    a = a + pltpu.roll(a, HEAD_DIM, 1)
    return a[:, :HEAD_DIM]


def _moba_sample_kernel(pt_ref, lhs_ref, kn_ref, vn_ref, kpool, vpool, o_ref,
                        buf, sem, s_scr, g_scr, stat_scr, l_scr, acc_scr, *, n_pages, chunk_pages):
    b = pl.program_id(0)
    c = pl.program_id(1)
    n_seq = pl.num_programs(0)
    n_chunks = n_pages // chunk_pages
    steps = 2 * n_chunks
    g = b * steps + c
    slot = g % 2
    rows = o_ref.shape[1]
    gate_rows = (lhs_ref.shape[1] - rows) // 2
    blk_pages = MOBA_BLOCK // PAGE_SIZE
    nb = n_pages // blk_pages

    def page_copy(pool, page, slot_, p):
        return pltpu.make_async_copy(pool.at[page], buf.at[slot_, p], sem.at[slot_])

    def start(step, seq, slot_):
        base = jnp.where(step >= n_chunks, step - n_chunks, step) * chunk_pages
        pages = [pt_ref[seq, base + p] for p in range(chunk_pages)]

        @pl.when(step < n_chunks)
        def _():
            for p in range(chunk_pages):
                page_copy(kpool, pages[p], slot_, p).start()

        @pl.when(step >= n_chunks)
        def _():
            for p in range(chunk_pages):
                page_copy(vpool, pages[p], slot_, p).start()

    @pl.when(g == 0)
    def _():
        start(c, b, slot)

    nxt = g + 1

    @pl.when(nxt < n_seq * steps)
    def _():
        start(nxt % steps, nxt // steps, nxt % 2)

    for p in range(chunk_pages):
        page_copy(kpool, 0, slot, p).wait()

    lhs = lhs_ref[0]
    qbd = lhs[:rows]

    @pl.when(c < n_chunks)
    def _():
        for p in range(chunk_pages):
            res = _dot(lhs, buf[slot, p].astype(BF16))
            s_scr[c * chunk_pages + p] = res[:rows]
            g_scr[c * chunk_pages + p] = res[rows:]

    @pl.when(c == n_chunks - 1)
    def _():
        lane = lax.broadcasted_iota(I32, (gate_rows, LANES), 1).astype(F32)
        gate = jnp.full((gate_rows, LANES), NEG_INF, F32)
        for n in range(nb):
            part = g_scr[n * blk_pages]
            for j in range(1, blk_pages):
                part = part + g_scr[n * blk_pages + j]
            col = jnp.sum(part[:gate_rows] + part[gate_rows:], axis=-1, keepdims=True) / MOBA_BLOCK
            gate = jnp.where(lane == n, col, gate)
        sel = jnp.zeros(gate.shape, F32)
        for _ in range(min(MOBA_TOPK, nb)):
            top = jnp.max(gate, axis=-1, keepdims=True)
            first = jnp.min(jnp.where(gate == top, lane, BIG_INDEX), axis=-1, keepdims=True)
            hit = lane == first
            sel = jnp.where(hit, 1.0, sel)
            gate = jnp.where(hit, NEG_INF, gate)
        sel_rows = jnp.concatenate(
            [jnp.broadcast_to(sel[r:r + 1], (GROUP, LANES)) for r in range(sel.shape[0])], axis=0)
        s_new = _dot_nt(qbd, kn_ref[0].astype(BF16))
        qrow = lax.broadcasted_iota(I32, s_new.shape, 0) // N_HEADS
        s_new = jnp.where(lax.broadcasted_iota(I32, s_new.shape, 1) <= qrow, s_new, NEG_INF)
        top = s_new
        for n in range(nb):
            keep = sel_rows[:, n:n + 1] > 0.0
            for j in range(blk_pages):
                s = jnp.where(keep, s_scr[n * blk_pages + j], NEG_INF)
                s_scr[n * blk_pages + j] = s
                top = jnp.maximum(top, s)
        m_run = jnp.max(top, axis=-1, keepdims=True)
        p_new = jnp.exp(s_new - m_run)
        stat_scr[:, 0:1] = m_run
        stat_scr[:, 1:2] = jnp.sum(p_new, axis=-1, keepdims=True)
        l_scr[...] = jnp.zeros(l_scr.shape, F32)
        acc_scr[...] = _dot(p_new.astype(BF16), vn_ref[0].astype(BF16))

    @pl.when(c >= n_chunks)
    def _():
        m_run = stat_scr[:, 0:1]
        l = l_scr[...]
        acc = acc_scr[...]
        for p in range(chunk_pages):
            prob = jnp.exp(s_scr[(c - n_chunks) * chunk_pages + p] - m_run)
            l = l + prob
            acc = acc + _dot_nt(prob.astype(BF16), buf[slot, p].astype(BF16))
        l_scr[...] = l
        acc_scr[...] = acc

    @pl.when(c == steps - 1)
    def _():
        den = jnp.sum(l_scr[...], axis=-1, keepdims=True) + stat_scr[:, 1:2]
        o_ref[0] = _fold_heads(acc_scr[...] / den)


def _moba_sample(page_table, lhs, k_new, v_new, kpool_t, vpool_t):
    db = lhs.shape[0]
    rows = lhs.shape[1] * N_HEADS // (N_HEADS + 2 * KV_HEADS)
    n_pages = page_table.shape[1]
    chunk_pages = n_pages // 2
    nb = n_pages * PAGE_SIZE // MOBA_BLOCK
    assert nb <= LANES and chunk_pages % (MOBA_BLOCK // PAGE_SIZE) == 0
    steps = 2 * (n_pages // chunk_pages)
    seq3 = lambda a: pl.BlockSpec((1,) + a.shape[1:], lambda i, c, pt: (i, 0, 0))
    kern = functools.partial(_moba_sample_kernel, n_pages=n_pages, chunk_pages=chunk_pages)
    return pl.pallas_call(
        kern,
        grid_spec=pltpu.PrefetchScalarGridSpec(
            num_scalar_prefetch=1,
            grid=(db, steps),
            in_specs=[seq3(lhs), seq3(k_new), seq3(v_new),
                      pl.BlockSpec(memory_space=pl.ANY), pl.BlockSpec(memory_space=pl.ANY)],
            out_specs=pl.BlockSpec((1, rows, HEAD_DIM), lambda i, c, pt: (i, 0, 0)),
            scratch_shapes=[
                pltpu.VMEM((2, chunk_pages, KV_DIM, PAGE_SIZE), F32),
                pltpu.SemaphoreType.DMA((2,)),
                pltpu.VMEM((n_pages, rows, PAGE_SIZE), F32),
                pltpu.VMEM((n_pages, lhs.shape[1] - rows, PAGE_SIZE), F32),
                pltpu.VMEM((rows, LANES), F32),
                pltpu.VMEM((rows, LANES), F32),
                pltpu.VMEM((rows, KV_DIM), F32),
            ],
        ),
        out_shape=jax.ShapeDtypeStruct((db, rows, HEAD_DIM), F32),
        compiler_params=_params("arbitrary", "arbitrary"),
        name="moba_sample",
    )(page_table, lhs, k_new, v_new, kpool_t, vpool_t)


def _swa_sample_kernel(qbd_ref, sink_ref, kb_ref, vb_ref, kn_ref, vn_ref, o_ref):
    rows = qbd_ref.shape[1]
    wb = kb_ref.shape[1]
    ds = kn_ref.shape[1]
    sink = sink_ref[...]
    qrow_b = lax.broadcasted_iota(I32, (rows, wb), 0) // N_HEADS
    buf_ok = lax.broadcasted_iota(I32, (rows, wb), 1) >= qrow_b
    qrow_n = lax.broadcasted_iota(I32, (rows, ds), 0) // N_HEADS
    new_ok = lax.broadcasted_iota(I32, (rows, ds), 1) <= qrow_n
    for s in range(qbd_ref.shape[0]):
        qbd = qbd_ref[s]
        sb = jnp.where(buf_ok, _dot_nt(qbd, kb_ref[s].astype(BF16)), NEG_INF)
        sn = jnp.where(new_ok, _dot_nt(qbd, kn_ref[s].astype(BF16)), NEG_INF)
        m = jnp.maximum(jnp.maximum(jnp.max(sb, axis=-1, keepdims=True),
                                    jnp.max(sn, axis=-1, keepdims=True)), sink)
        pb = jnp.exp(sb - m)
        pn = jnp.exp(sn - m)
        den = (jnp.sum(pb, axis=-1, keepdims=True) + jnp.sum(pn, axis=-1, keepdims=True)
               + jnp.exp(sink - m))
        acc = _dot(pb.astype(BF16), vb_ref[s].astype(BF16)) + _dot(pn.astype(BF16), vn_ref[s].astype(BF16))
        o_ref[s] = _fold_heads(acc / den)


def _swa_sample(qbd, sink_rows, buf_k, buf_v, k_new, v_new, seqs_per_step=8):
    db, rows, _ = qbd.shape
    wb = buf_k.shape[1]
    ds = k_new.shape[1]
    spec = lambda r: pl.BlockSpec((seqs_per_step, r, KV_DIM), lambda i: (i, 0, 0))
    return pl.pallas_call(
        _swa_sample_kernel,
        grid=(db // seqs_per_step,),
        in_specs=[spec(rows), pl.BlockSpec((rows, 1), lambda i: (0, 0)),
                  spec(wb), spec(wb), spec(ds), spec(ds)],
        out_specs=pl.BlockSpec((seqs_per_step, rows, HEAD_DIM), lambda i: (i, 0, 0)),
        out_shape=jax.ShapeDtypeStruct((db, rows, HEAD_DIM), F32),
        compiler_params=_params("arbitrary"),
        name="swa_sample",
    )(qbd, sink_rows, buf_k, buf_v, k_new, v_new)


def _attn_out_kernel(a_ref, x_ref, g_ref, sh_ref, sc_ref, wo_ref, bo_ref, lg_ref, lb_ref,
                     wrh_ref, wrl_ref, xn_ref, h_ref, lt_ref):
    o = _dot(a_ref[...], wo_ref[...]) + bo_ref[...]
    xn = _layer_norm(ALPHA * x_ref[...] + g_ref[0] * o, lg_ref[...], lb_ref[...])
    xn_ref[...] = xn
    h = xn * (1.0 + sc_ref[0]) + sh_ref[0]
    h_ref[...] = h
    h_hi, h_lo = _split_bf16(h)
    wrh = wrh_ref[...]
    lt_ref[...] = _dot_nt(wrh, h_hi) + _dot_nt(wrh, h_lo) + _dot_nt(wrl_ref[...], h_hi)


def _attn_out(a, x, gate, shift, scale, wo_bf, bo, ln_g, ln_b, wr_hi, wr_lo, tm):
    n = x.shape[0]
    per_mod = n // tm // gate.shape[0]
    mod_rows = gate.shape[1]
    row = pl.BlockSpec((tm, D_MODEL), lambda i: (i, 0))
    mod = pl.BlockSpec((1, mod_rows, D_MODEL), lambda i: (i // per_mod, 0, 0))
    vec = pl.BlockSpec((1, D_MODEL), lambda i: (0, 0))
    full = lambda r, c: pl.BlockSpec((r, c), lambda i: (0, 0))
    return pl.pallas_call(
        _attn_out_kernel,
        grid=(n // tm,),
        in_specs=[row, row, mod, mod, mod, full(Q_DIM, D_MODEL), vec, vec, vec,
                  full(N_EXPERTS, D_MODEL), full(N_EXPERTS, D_MODEL)],
        out_specs=[row, row, pl.BlockSpec((N_EXPERTS, tm), lambda i: (0, i))],
        out_shape=[jax.ShapeDtypeStruct((n, D_MODEL), F32),
                   jax.ShapeDtypeStruct((n, D_MODEL), F32),
                   jax.ShapeDtypeStruct((N_EXPERTS, n), F32)],
        compiler_params=_params("arbitrary"),
        name="attn_out_norm",
    )(a, x, gate, shift, scale, wo_bf, bo.reshape(1, -1), ln_g.reshape(1, -1), ln_b.reshape(1, -1),
      wr_hi, wr_lo)


def _first_max(x, idx):
    top = jnp.max(x, axis=0, keepdims=True)
    first = jnp.min(jnp.where(x == top, idx, BIG_INDEX), axis=0, keepdims=True)
    return top, first


def _route_kernel(lt_ref, b_ref, idx_ref, w_ref, pos_ref, cnt_ref):
    i = pl.program_id(0)
    tn = lt_ref.shape[1]
    scores = _sigmoid(lt_ref[...])
    biased = scores + b_ref[...]
    e_idx = lax.broadcasted_iota(I32, (N_EXPERTS, tn), 0).astype(F32)
    l_idx = lax.broadcasted_iota(I32, (PER_GROUP, tn), 0).astype(F32)
    g_score = []
    for g in range(N_EXPERT_GROUPS):
        x = biased[g * PER_GROUP:(g + 1) * PER_GROUP]
        top1, first = _first_max(x, l_idx)
        top2 = jnp.max(jnp.where(l_idx == first, NEG_INF, x), axis=0, keepdims=True)
        g_score.append(top1 + top2)
    pieces = []
    for g in range(N_EXPERT_GROUPS):
        ahead = jnp.zeros((1, tn), F32)
        for o in range(N_EXPERT_GROUPS):
            if o != g:
                wins = (g_score[o] >= g_score[g]) if o < g else (g_score[o] > g_score[g])
                ahead = ahead + jnp.where(wins, 1.0, 0.0)
        pieces.append(jnp.where(ahead < TOPK_GROUPS, biased[g * PER_GROUP:(g + 1) * PER_GROUP], NEG_INF))
    masked = jnp.concatenate(pieces, axis=0)
    chosen = jnp.zeros((N_EXPERTS, tn), F32)
    firsts, picked = [], []
    for _ in range(TOP_K):
        _, first = _first_max(masked, e_idx)
        hit = e_idx == first
        firsts.append(first)
        picked.append(jnp.sum(jnp.where(hit, scores, 0.0), axis=0, keepdims=True))
        chosen = jnp.where(hit, 1.0, chosen)
        masked = jnp.where(hit, NEG_INF, masked)
    total = picked[0]
    for s in picked[1:]:
        total = total + s
    for r in range(TOP_K):
        idx_ref[r:r + 1, :] = firsts[r].astype(I32)
        w_ref[r:r + 1, :] = picked[r] / total * ROUTED_SCALE

    @pl.when(i == 0)
    def _():
        cnt_ref[...] = jnp.zeros(cnt_ref.shape, F32)
    r = lax.broadcasted_iota(I32, (tn, tn), 0)
    c = lax.broadcasted_iota(I32, (tn, tn), 1)
    before = jnp.where(r < c, 1.0, 0.0).astype(BF16)
    prior = _dot(chosen.astype(BF16), before) + cnt_ref[...]
    for r in range(TOP_K):
        pos_ref[r:r + 1, :] = jnp.sum(jnp.where(e_idx == firsts[r], prior, 0.0), axis=0,
                                      keepdims=True).astype(I32)
    cnt_ref[...] = cnt_ref[...] + jnp.sum(chosen, axis=1, keepdims=True)


def _route(logits_t, b_router, tn=512):
    n = logits_t.shape[1]
    slot = pl.BlockSpec((TOP_K, tn), lambda i: (0, i))
    return pl.pallas_call(
        _route_kernel,
        grid=(n // tn,),
        in_specs=[pl.BlockSpec((N_EXPERTS, tn), lambda i: (0, i)),
                  pl.BlockSpec((N_EXPERTS, 1), lambda i: (0, 0))],
        out_specs=[slot, slot, slot, pl.BlockSpec((N_EXPERTS, 1), lambda i: (0, 0))],
        out_shape=[jax.ShapeDtypeStruct((TOP_K, n), I32),
                   jax.ShapeDtypeStruct((TOP_K, n), F32),
                   jax.ShapeDtypeStruct((TOP_K, n), I32),
                   jax.ShapeDtypeStruct((N_EXPERTS, 1), F32)],
        compiler_params=_params("arbitrary"),
        name="route",
    )(logits_t, b_router.reshape(N_EXPERTS, 1))


def _dest_kernel(idx_ref, pos_ref, start_ref, dest_ref):
    tn = idx_ref.shape[1]
    e_idx = lax.broadcasted_iota(I32, (N_EXPERTS, tn), 0)
    start = start_ref[...]
    for r in range(TOP_K):
        first = jnp.sum(jnp.where(e_idx == idx_ref[r:r + 1, :], start, 0.0), axis=0, keepdims=True)
        dest_ref[r:r + 1, :] = first.astype(I32) + pos_ref[r:r + 1, :]


def _dest_rows(idx_t, pos_t, pad_start, tn=512):
    n = idx_t.shape[1]
    slot = pl.BlockSpec((TOP_K, tn), lambda i: (0, i))
    return pl.pallas_call(
        _dest_kernel,
        grid=(n // tn,),
        in_specs=[slot, slot, pl.BlockSpec((N_EXPERTS, 1), lambda i: (0, 0))],
        out_specs=slot,
        out_shape=jax.ShapeDtypeStruct((TOP_K, n), I32),
        compiler_params=_params("arbitrary"),
        name="dest_rows",
    )(idx_t, pos_t, pad_start.astype(F32).reshape(N_EXPERTS, 1))


def _moe_kernel(be_ref, nu_ref, x_ref, wg_ref, wu_ref, wd_ref, o_ref, wg_bf, wu_bf, wd_bf):
    i = pl.program_id(0)
    used = i < nu_ref[0]
    changed = (i == 0) | (be_ref[i] != be_ref[jnp.maximum(i - 1, 0)])

    @pl.when(used & changed)
    def _():
        wg_bf[...] = wg_ref[0, 0].astype(BF16)
        wu_bf[...] = wu_ref[0, 0].astype(BF16)
        wd_bf[...] = wd_ref[0, 0].astype(BF16)

    @pl.when(used)
    def _():
        x = x_ref[...].astype(BF16)
        gate = _dot(x, wg_bf[...])
        up = _dot(x, wu_bf[...])
        act = gate * _sigmoid(gate) * up
        o_ref[...] = _dot(act.astype(BF16), wd_bf[...])

    @pl.when(jnp.logical_not(used))
    def _():
        o_ref[...] = jnp.zeros(o_ref.shape, F32)


def _moe_experts(blk_e, n_used, x_sorted, w_gate, w_up, w_down, layer):
    rows = x_sorted.shape[0]
    n_blocks = rows // MOE_ROWS
    x_map = lambda i, be, nu: (jnp.minimum(i, nu[0] - 1), 0)
    w_map = lambda i, be, nu: (layer, be[i], 0, 0)
    return pl.pallas_call(
        _moe_kernel,
        grid_spec=pltpu.PrefetchScalarGridSpec(
            num_scalar_prefetch=2,
            grid=(n_blocks,),
            in_specs=[pl.BlockSpec((MOE_ROWS, D_MODEL), x_map),
                      pl.BlockSpec((1, 1, D_MODEL, D_EXPERT), w_map),
                      pl.BlockSpec((1, 1, D_MODEL, D_EXPERT), w_map),
                      pl.BlockSpec((1, 1, D_EXPERT, D_MODEL), w_map)],
            out_specs=pl.BlockSpec((MOE_ROWS, D_MODEL), lambda i, be, nu: (i, 0)),
            scratch_shapes=[pltpu.VMEM((D_MODEL, D_EXPERT), BF16),
                            pltpu.VMEM((D_MODEL, D_EXPERT), BF16),
                            pltpu.VMEM((D_EXPERT, D_MODEL), BF16)],
        ),
        out_shape=jax.ShapeDtypeStruct((rows, D_MODEL), F32),
        compiler_params=_params("arbitrary"),
        name="moe_experts",
    )(blk_e, n_used, x_sorted, w_gate, w_up, w_down)


def _ffn_out_kernel(h_ref, r_ref, x_ref, g_ref, wsg_ref, wsu_ref, wsd_ref, lg_ref, lb_ref, o_ref):
    h = h_ref[...].astype(BF16)
    gate = _dot(h, wsg_ref[...])
    up = _dot(h, wsu_ref[...])
    shared = _dot((gate * _sigmoid(gate) * up).astype(BF16), wsd_ref[...])
    y = ALPHA * x_ref[...] + g_ref[0] * (r_ref[...] + shared)
    o_ref[...] = _layer_norm(y, lg_ref[...], lb_ref[...])


def _ffn_out(h, routed, x, gate, wsg, wsu, wsd, ln_g, ln_b, tm):
    n = x.shape[0]
    per_mod = n // tm // gate.shape[0]
    row = pl.BlockSpec((tm, D_MODEL), lambda i: (i, 0))
    mod = pl.BlockSpec((1, gate.shape[1], D_MODEL), lambda i: (i // per_mod, 0, 0))
    vec = pl.BlockSpec((1, D_MODEL), lambda i: (0, 0))
    full = lambda a: pl.BlockSpec(a.shape, lambda i: (0, 0))
    return pl.pallas_call(
        _ffn_out_kernel,
        grid=(n // tm,),
        in_specs=[row, row, row, mod, full(wsg), full(wsu), full(wsd), vec, vec],
        out_specs=row,
        out_shape=jax.ShapeDtypeStruct((n, D_MODEL), F32),
        compiler_params=_params("arbitrary"),
        name="ffn_out_norm",
    )(h, routed, x, gate, wsg, wsu, wsd, ln_g.reshape(1, -1), ln_b.reshape(1, -1))


def _dispatch(idx_t, pos_t, counts):
    n = idx_t.shape[1]
    counts = counts.reshape(N_EXPERTS).astype(I32)
    padded = (counts + MOE_ROWS - 1) // MOE_ROWS * MOE_ROWS
    pad_end = jnp.cumsum(padded)
    pad_start = pad_end - padded
    dest = _dest_rows(idx_t, pos_t, pad_start)
    n_blocks = (n * TOP_K + N_EXPERTS * (MOE_ROWS - 1)) // MOE_ROWS
    rows = n_blocks * MOE_ROWS
    tok = jnp.broadcast_to(jnp.arange(n, dtype=I32)[None, :], dest.shape)
    row_tok = jnp.zeros((rows,), I32).at[dest.reshape(-1)].set(tok.reshape(-1), unique_indices=True)
    n_used = pad_end[-1] // MOE_ROWS
    blk = jnp.minimum(jnp.arange(n_blocks, dtype=I32), n_used - 1) * MOE_ROWS
    blk_e = jnp.minimum(jnp.searchsorted(pad_end, blk, side="right"), N_EXPERTS - 1).astype(I32)
    return dest, row_tok, blk_e, n_used.reshape(1).astype(I32)


def _head_major(x, heads):
    b, t, _ = x.shape
    return x.reshape(b, t, heads, HEAD_DIM).transpose(0, 2, 1, 3)


def _block_diag_rows(x, heads_per_kv):
    db, ds, heads, _ = x.shape
    kv_of_head = jnp.arange(heads) // heads_per_kv
    onehot = (kv_of_head[:, None] == jnp.arange(KV_HEADS)[None, :]).astype(x.dtype)
    out = x[:, :, :, None, :] * onehot[None, None, :, :, None]
    return out.reshape(db, ds * heads, KV_DIM)


def kernel(x_prompt, x_sample, c_prompt, c_sample, cache_moba_k, cache_moba_v, state_swa_k, state_swa_v,
           page_table, w_ada, b_ada, w_qkv, b_qkv, attn_sinks, w_o, b_o, ln_attn_g, ln_attn_b,
           w_router, b_router, w_exp_gate, w_exp_up, w_exp_down, w_sh_gate, w_sh_up, w_sh_down,
           ln_ffn_g, ln_ffn_b):
    bsz, seq, d = x_prompt.shape
    db, ds, _ = x_sample.shape
    n_p, n_s = bsz * seq, db * ds
    past_len = page_table.shape[1] * PAGE_SIZE
    tm_p, tm_s = 512, n_s
    depth = w_ada.shape[0]

    mods = _ada_mod(jnp.concatenate([c_prompt, c_sample], 0), w_ada, b_ada)
    tabs_p = _rope_tables(jnp.arange(seq, dtype=I32))
    tabs_s = _rope_tables(jnp.tile(past_len + jnp.arange(ds, dtype=I32), db))

    xp = x_prompt.reshape(n_p, d)
    xs = x_sample.reshape(n_s, d)
    outs = {k: [] for k in ("mkp", "mvp", "mks", "mvs", "skp", "svp", "sks", "svs")}
    for i in range(depth):
        j = i // 2
        m = mods[i].reshape(bsz + db, N_MOD, d)
        mp = [m[:bsz, t][:, None, :] for t in range(N_MOD)]
        ms = [jnp.repeat(m[bsz:, t], ds, axis=0)[None] for t in range(N_MOD)]
        w_qkv_bf = w_qkv[i].astype(BF16)
        moba = i % 2 == 0
        res_p = _qkv_project(xp, mp[0], mp[1], w_qkv_bf, b_qkv[i], tabs_p, tm_p, moba)
        res_s = _qkv_project(xs, ms[0], ms[1], w_qkv_bf, b_qkv[i], tabs_s, tm_s, moba)
        qp, kp, vp, kp_bf, vp_bf = res_p[:5]
        qs, ks, vs = res_s[:3]
        q_hm = _head_major(qp.reshape(bsz, seq, Q_DIM), N_HEADS)
        k_hm = _head_major(kp_bf.reshape(bsz, seq, KV_DIM), KV_HEADS)
        v_hm = _head_major(vp_bf.reshape(bsz, seq, KV_DIM), KV_HEADS)
        qbd = _block_diag_rows(qs.reshape(db, ds, N_HEADS, HEAD_DIM), GROUP)
        pad_rows = ((0, 0), (0, LANES - ds), (0, 0))
        ks3 = jnp.pad(ks.reshape(db, ds, KV_DIM), pad_rows)
        vs3 = jnp.pad(vs.reshape(db, ds, KV_DIM), pad_rows)
        kp5 = kp.reshape(bsz, seq, KV_HEADS, HEAD_DIM)
        vp5 = vp.reshape(bsz, seq, KV_HEADS, HEAD_DIM)
        ks5 = ks.reshape(db, ds, KV_HEADS, HEAD_DIM)
        vs5 = vs.reshape(db, ds, KV_HEADS, HEAD_DIM)
        if moba:
            sel = _moba_select(res_p[5].reshape(bsz, seq, KV_DIM), kp.reshape(bsz, seq, KV_DIM))
            a_hm = _moba_prompt(q_hm, k_hm, v_hm, sel.transpose(0, 1, 3, 2))
            qs_hi, qs_lo = _split_bf16(_block_diag_rows(res_s[5].reshape(db, ds, KV_HEADS, HEAD_DIM), 1))
            n_pool = cache_moba_k.shape[1]
            pool_t = lambda pool: pool[j].transpose(0, 2, 3, 1).reshape(n_pool, KV_DIM, PAGE_SIZE)
            a_s = _moba_sample(page_table, jnp.concatenate([qbd, qs_hi, qs_lo], axis=1), ks3, vs3,
                               pool_t(cache_moba_k), pool_t(cache_moba_v))
            outs["mkp"].append(kp5)
            outs["mvp"].append(vp5)
            outs["mks"].append(ks5)
            outs["mvs"].append(vs5)
        else:
            a_hm = _swa_prompt(q_hm, k_hm, v_hm, attn_sinks[j])
            wb = state_swa_k.shape[2]
            buf_k = state_swa_k[j].reshape(db, wb, KV_DIM)
            buf_v = state_swa_v[j].reshape(db, wb, KV_DIM)
            sink_rows = jnp.tile(attn_sinks[j], ds).reshape(ds * N_HEADS, 1)
            a_s = _swa_sample(qbd, sink_rows, buf_k, buf_v, ks3, vs3)
            wbp = min(WINDOW, seq)
            outs["skp"].append(kp5[:, -wbp:])
            outs["svp"].append(vp5[:, -wbp:])
            outs["sks"].append(jnp.concatenate([state_swa_k[j], ks5], axis=1)[:, -wb:])
            outs["svs"].append(jnp.concatenate([state_swa_v[j], vs5], axis=1)[:, -wb:])
        a_p = a_hm.transpose(0, 2, 1, 3).reshape(n_p, Q_DIM)
        a_s = a_s.reshape(n_s, Q_DIM).astype(BF16)

        wo_bf = w_o[i].astype(BF16)
        wr_hi, wr_lo = _split_bf16(w_router[i].T)
        args = (wo_bf, b_o[i], ln_attn_g[i], ln_attn_b[i], wr_hi, wr_lo)
        xp, hp, ltp = _attn_out(a_p, xp, mp[2], mp[3], mp[4], *args, 256)
        xs, hs, lts = _attn_out(a_s, xs, ms[2], ms[3], ms[4], *args, n_s)

        h_all = jnp.concatenate([hp, hs], axis=0)
        idx_t, w_t, pos_t, counts = _route(jnp.concatenate([ltp, lts], axis=1), b_router[i])
        dest, row_tok, blk_e, n_used = _dispatch(idx_t, pos_t, counts)
        y_sorted = _moe_experts(blk_e, n_used, h_all[row_tok], w_exp_gate, w_exp_up, w_exp_down, i)
        routed = jnp.sum(y_sorted[dest] * w_t[:, :, None], axis=0)

        ws = (w_sh_gate[i].astype(BF16), w_sh_up[i].astype(BF16), w_sh_down[i].astype(BF16))
        xp = _ffn_out(hp, routed[:n_p], xp, mp[5], *ws, ln_ffn_g[i], ln_ffn_b[i], 256)
        xs = _ffn_out(hs, routed[n_p:], xs, ms[5], *ws, ln_ffn_g[i], ln_ffn_b[i], n_s)

    st = lambda key: jnp.stack(outs[key])
    return (xp.reshape(bsz, seq, d), xs.reshape(db, ds, d), st("mkp"), st("mvp"), st("mks"), st("mvs"),
            st("skp"), st("svp"), st("sks"), st("svs"))
```

```python
import functools

import jax
import jax.numpy as jnp
from jax import lax
from jax.experimental import pallas as pl
from jax.experimental.pallas import tpu as pltpu

F32 = jnp.float32
BF16 = jnp.bfloat16
I32 = jnp.int32

D_MODEL = 1024
N_HEADS = 16
HEAD_DIM = 64
KV_HEADS = 4
GROUP = N_HEADS // KV_HEADS
Q_DIM = N_HEADS * HEAD_DIM
KV_DIM = KV_HEADS * HEAD_DIM
QKV_DIM = Q_DIM + 2 * KV_DIM
ATTN_SCALE = HEAD_DIM ** -0.5
ROT_DIM = HEAD_DIM // 4
ROPE_THETA = 500000.0
PAGE_SIZE = 128
MOBA_BLOCK = 256
MOBA_TOPK = 3
WINDOW = 128
N_EXPERTS = 256
TOP_K = 8
N_EXPERT_GROUPS = 8
TOPK_GROUPS = 4
PER_GROUP = N_EXPERTS // N_EXPERT_GROUPS
D_EXPERT = 256
ROUTED_SCALE = 2.5
N_MOD = 6
DEPTH = 2
ALPHA = (2 * DEPTH) ** 0.25
LN_EPS = 1e-5

LANES = 128
VMEM_LIMIT = 48 * 1024 * 1024
MOE_ROWS = 256
NEG_INF = float("-inf")
BIG_INDEX = 1e9

_NT = (((1,), (1,)), ((), ()))


def _dot(a, b):
    return jnp.dot(a, b, preferred_element_type=F32)


def _dot_nt(a, b):
    return lax.dot_general(a, b, _NT, preferred_element_type=F32)


def _split_bf16(x):
    hi = x.astype(BF16)
    lo = (x - hi.astype(F32)).astype(BF16)
    return hi, lo


def _sigmoid(x):
    return 1.0 / (1.0 + jnp.exp(-x))


PACK_DTYPE = jnp.uint32
PACK_WIDTH = D_MODEL // 2


def _pack_rows(x):
    half = x.shape[1] // 2
    hi = lax.bitcast_convert_type(x[:, :half].astype(BF16).astype(F32), jnp.uint32)
    lo = lax.bitcast_convert_type(x[:, half:].astype(BF16).astype(F32), jnp.uint32)
    return hi | (lo >> 16)


def _unpack_rows(u):
    hi = lax.bitcast_convert_type(u & jnp.uint32(0xFFFF0000), F32)
    lo = lax.bitcast_convert_type(u << 16, F32)
    return hi, lo


def _params(*sem):
    return pltpu.CompilerParams(dimension_semantics=sem, vmem_limit_bytes=VMEM_LIMIT)


def _layer_norm(y, g, b):
    mu = jnp.mean(y, axis=-1, keepdims=True)
    yc = y - mu
    var = jnp.mean(yc * yc, axis=-1, keepdims=True)
    return yc * lax.rsqrt(var + LN_EPS) * g + b


def _ada_kernel(c_ref, w_ref, b_ref, o_ref):
    c = c_ref[...]
    a_hi, a_lo = _split_bf16(c * _sigmoid(c))
    w_hi, w_lo = _split_bf16(w_ref[0])
    o_ref[0] = _dot(a_hi, w_hi) + _dot(a_lo, w_hi) + _dot(a_hi, w_lo) + b_ref[0]


def _ada_mod(c_all, w_ada, b_ada):
    rows = c_all.shape[0]
    depth = w_ada.shape[0]
    return pl.pallas_call(
        _ada_kernel,
        grid=(depth, N_MOD),
        in_specs=[
            pl.BlockSpec((rows, D_MODEL), lambda l, j: (0, 0)),
            pl.BlockSpec((1, D_MODEL, D_MODEL), lambda l, j: (l, 0, j)),
            pl.BlockSpec((1, 1, D_MODEL), lambda l, j: (l, 0, j)),
        ],
        out_specs=pl.BlockSpec((1, rows, D_MODEL), lambda l, j: (l, 0, j)),
        out_shape=jax.ShapeDtypeStruct((depth, rows, N_MOD * D_MODEL), F32),
        compiler_params=_params("arbitrary", "arbitrary"),
        name="ada_mod",
    )(c_all, w_ada, b_ada.reshape(depth, 1, N_MOD * D_MODEL))


def _qkv_kernel(x_ref, sh_ref, sc_ref, w_ref, b_ref, c_ref, s1_ref, s2_ref,
                q_ref, k_ref, v_ref, kb_ref, vb_ref, *qs_ref):
    u = x_ref[...] * (1.0 + sc_ref[0]) + sh_ref[0]
    qkv = _dot(u.astype(BF16), w_ref[...]) + b_ref[...]
    cos, s1, s2 = c_ref[...], s1_ref[...], s2_ref[...]
    rots = []
    for j in range((Q_DIM + KV_DIM) // LANES):
        blk = qkv[:, j * LANES:(j + 1) * LANES]
        rots.append(blk * cos + pltpu.roll(blk, LANES - ROT_DIM // 2, 1) * s1
                    + pltpu.roll(blk, ROT_DIM // 2, 1) * s2)
    nq = Q_DIM // LANES
    for j in range(nq):
        q_ref[:, j * LANES:(j + 1) * LANES] = (rots[j] * ATTN_SCALE).astype(BF16)
    for j in range(KV_DIM // LANES):
        k_ref[:, j * LANES:(j + 1) * LANES] = rots[nq + j]
        kb_ref[:, j * LANES:(j + 1) * LANES] = rots[nq + j].astype(BF16)
    v = qkv[:, Q_DIM + KV_DIM:]
    v_ref[...] = v
    vb_ref[...] = v.astype(BF16)
    if qs_ref:
        lane = lax.broadcasted_iota(I32, rots[0].shape, 1)
        halves = []
        for kv in range(KV_HEADS):
            t = rots[2 * kv] + rots[2 * kv + 1]
            halves.append(t + pltpu.roll(t, HEAD_DIM, 1))
        for j in range(KV_DIM // LANES):
            qs_ref[0][:, j * LANES:(j + 1) * LANES] = jnp.where(
                lane < HEAD_DIM, halves[2 * j], halves[2 * j + 1])


def _qkv_project(x, shift, scale, w_bf, b, tabs, tm, with_qsum):
    n = x.shape[0]
    n_tab = tabs[0].shape[0] // tm
    per_mod = n // tm // shift.shape[0]
    mod_rows = shift.shape[1]
    row_spec = lambda width: pl.BlockSpec((tm, width), lambda i: (i, 0))
    mod_spec = pl.BlockSpec((1, mod_rows, D_MODEL), lambda i: (i // per_mod, 0, 0))
    tab_spec = pl.BlockSpec((tm, LANES), lambda i: (i % n_tab, 0))
    out_shape = [
        jax.ShapeDtypeStruct((n, Q_DIM), BF16),
        jax.ShapeDtypeStruct((n, KV_DIM), F32),
        jax.ShapeDtypeStruct((n, KV_DIM), F32),
        jax.ShapeDtypeStruct((n, KV_DIM), BF16),
        jax.ShapeDtypeStruct((n, KV_DIM), BF16),
    ]
    out_specs = [row_spec(Q_DIM)] + [row_spec(KV_DIM)] * 4
    if with_qsum:
        out_shape.append(jax.ShapeDtypeStruct((n, KV_DIM), F32))
        out_specs.append(row_spec(KV_DIM))
    return pl.pallas_call(
        _qkv_kernel,
        grid=(n // tm,),
        in_specs=[
            row_spec(D_MODEL), mod_spec, mod_spec,
            pl.BlockSpec((D_MODEL, QKV_DIM), lambda i: (0, 0)),
            pl.BlockSpec((1, QKV_DIM), lambda i: (0, 0)),
            tab_spec, tab_spec, tab_spec,
        ],
        out_specs=out_specs,
        out_shape=out_shape,
        compiler_params=_params("arbitrary"),
        name="qkv_rope",
    )(x, shift, scale, w_bf, b.reshape(1, QKV_DIM), *tabs)


def _rope_tables(pos):
    half = ROT_DIM // 2
    inv_freq = 1.0 / (ROPE_THETA ** (jnp.arange(0, ROT_DIM, 2, dtype=F32) / ROT_DIM))
    ang = pos.astype(F32)[:, None] * inv_freq[None, :]
    cos, sin = jnp.cos(ang), jnp.sin(ang)
    rest = HEAD_DIM - ROT_DIM
    ones = jnp.ones((pos.shape[0], rest), F32)
    zeros = jnp.zeros((pos.shape[0], rest), F32)
    zh = jnp.zeros_like(sin)
    c = jnp.concatenate([cos, cos, ones], -1)
    s1 = jnp.concatenate([-sin, zh, zeros], -1)
    s2 = jnp.concatenate([zh, sin, zeros], -1)
    reps = LANES // HEAD_DIM
    return tuple(jnp.tile(t, (1, reps)) for t in (c, s1, s2))


def _moba_select_kernel(qs_ref, k_ref, sel_ref):
    t_len = k_ref.shape[1]
    nb = t_len // MOBA_BLOCK
    k = k_ref[0]
    nb_pad = 16
    means = jnp.concatenate(
        [jnp.sum(k[n * MOBA_BLOCK:(n + 1) * MOBA_BLOCK], axis=0, keepdims=True) / MOBA_BLOCK
         for n in range(nb)] + [jnp.zeros((nb_pad - nb, KV_DIM), F32)], axis=0)
    lane_head = lax.broadcasted_iota(I32, (nb_pad, KV_DIM), 1) // HEAD_DIM
    q_hi, q_lo = _split_bf16(qs_ref[0])
    pos_blk = lax.broadcasted_iota(I32, (nb, t_len), 1) // MOBA_BLOCK
    blk = lax.broadcasted_iota(I32, (nb, t_len), 0)
    past = blk < pos_blk
    for kv in range(KV_HEADS):
        m_hi, m_lo = _split_bf16(jnp.where(lane_head == kv, means, 0.0))
        gate = (_dot_nt(m_hi, q_hi) + _dot_nt(m_lo, q_hi) + _dot_nt(m_hi, q_lo))[:nb]
        gate = jnp.where(past, gate, NEG_INF)
        rank = jnp.zeros((nb, t_len), F32)
        for m in range(nb):
            other = gate[m:m + 1, :]
            ahead = (other > gate) | ((other == gate) & (blk > m))
            rank = rank + jnp.where(ahead, 1.0, 0.0)
        sel_ref[0, kv] = jnp.where(past & (rank < MOBA_TOPK), 1.0, 0.0)


def _moba_select(qsum, k):
    b, t, _ = k.shape
    nb = t // MOBA_BLOCK
    spec = pl.BlockSpec((1, t, KV_DIM), lambda i: (i, 0, 0))
    return pl.pallas_call(
        _moba_select_kernel,
        grid=(b,),
        in_specs=[spec, spec],
        out_specs=pl.BlockSpec((1, KV_HEADS, nb, t), lambda i: (i, 0, 0, 0)),
        out_shape=jax.ShapeDtypeStruct((b, KV_HEADS, nb, t), F32),
        compiler_params=_params("arbitrary"),
        name="moba_select",
    )(qsum, k)


def _moba_prompt_kernel(q_ref, k_ref, v_ref, sel_ref, o_ref):
    qi = pl.program_id(2)
    tq = MOBA_BLOCK
    cols = GROUP * tq
    q_t = jnp.concatenate([q_ref[0, g] for g in range(GROUP)], axis=1)

    def scores(n):
        kb = k_ref[0, 0, pl.ds(pl.multiple_of(n * tq, tq), tq), :]
        return _dot(kb, q_t)

    key = lax.broadcasted_iota(I32, (tq, cols), 0)
    qry = lax.broadcasted_iota(I32, (tq, cols), 1) & (tq - 1)
    s = jnp.where(key <= qry, scores(qi), NEG_INF)
    m0 = jnp.max(s, axis=0, keepdims=True)
    p = jnp.exp(s - m0)
    l0 = jnp.sum(p, axis=0, keepdims=True)
    acc0 = _dot(v_ref[0, 0, qi], p.astype(BF16))

    def body(n, carry):
        m, l, acc = carry
        chosen = sel_ref[0, 0, pl.ds(n, 1), :]
        keep = jnp.concatenate([chosen] * GROUP, axis=1) > 0.0
        s = jnp.where(keep, scores(n), NEG_INF)
        m_new = jnp.maximum(m, jnp.max(s, axis=0, keepdims=True))
        alpha = jnp.exp(m - m_new)
        p = jnp.exp(s - m_new)
        l = alpha * l + jnp.sum(p, axis=0, keepdims=True)
        acc = alpha * acc + _dot(v_ref[0, 0, n], p.astype(BF16))
        return m_new, l, acc

    _, l, acc = lax.fori_loop(0, qi, body, (m0, l0, acc0))
    out = (acc / l).astype(BF16)
    for g in range(GROUP):
        o_ref[0, g] = out[:, g * tq:(g + 1) * tq]


def _moba_prompt(q_t, k_hm, v_t, sel):
    b, _, _, t = q_t.shape
    nb = t // MOBA_BLOCK
    q_spec = pl.BlockSpec((1, GROUP, HEAD_DIM, MOBA_BLOCK), lambda i, j, n: (i, j, 0, n))
    return pl.pallas_call(
        _moba_prompt_kernel,
        grid=(b, KV_HEADS, nb),
        in_specs=[q_spec,
                  pl.BlockSpec((1, 1, t, HEAD_DIM), lambda i, j, n: (i, j, 0, 0)),
                  pl.BlockSpec((1, 1, nb, HEAD_DIM, MOBA_BLOCK), lambda i, j, n: (i, j, 0, 0, 0)),
                  pl.BlockSpec((1, 1, nb, MOBA_BLOCK), lambda i, j, n: (i, j, 0, n))],
        out_specs=q_spec,
        out_shape=jax.ShapeDtypeStruct(q_t.shape, BF16),
        compiler_params=_params("arbitrary", "arbitrary", "arbitrary"),
        name="moba_prompt",
    )(q_t, k_hm, v_t, sel)


def _swa_prompt_kernel(sink_ref, q_ref, kp_ref, k_ref, vp_ref, v_ref, o_ref):
    j = pl.program_id(1)
    n = pl.program_id(2)
    tq = WINDOW
    cols = GROUP * tq
    q_t = jnp.concatenate([q_ref[0, g] for g in range(GROUP)], axis=1)
    keys = jnp.concatenate([kp_ref[0, 0], k_ref[0, 0]], axis=0)
    vals_t = jnp.concatenate([vp_ref[0, 0, 0], v_ref[0, 0, 0]], axis=1)
    key = lax.broadcasted_iota(I32, (2 * tq, cols), 0)
    qry = lax.broadcasted_iota(I32, (2 * tq, cols), 1) & (tq - 1)
    visible = ((key < tq) & (key >= qry)) | ((key >= tq) & (key - tq <= qry))
    first_block = jnp.where(n > 0, 0.0, NEG_INF)
    s = _dot(keys, q_t)
    s = jnp.where(visible, s + jnp.where(key < tq, first_block, 0.0), NEG_INF)
    sink = jnp.concatenate([jnp.full((1, tq), sink_ref[j * GROUP + g], F32) for g in range(GROUP)], axis=1)
    m = jnp.maximum(jnp.max(s, axis=0, keepdims=True), sink)
    p = jnp.exp(s - m)
    den = jnp.sum(p, axis=0, keepdims=True) + jnp.exp(sink - m)
    out = (_dot(vals_t, p.astype(BF16)) / den).astype(BF16)
    for g in range(GROUP):
        o_ref[0, g] = out[:, g * tq:(g + 1) * tq]


def _swa_prompt(q_t, k_hm, v_t, sinks):
    b, _, _, t = q_t.shape
    nb = t // WINDOW
    before = lambda n: jnp.maximum(n - 1, 0)
    q_spec = pl.BlockSpec((1, GROUP, HEAD_DIM, WINDOW), lambda i, j, n, s: (i, j, 0, n))
    k_own = pl.BlockSpec((1, 1, WINDOW, HEAD_DIM), lambda i, j, n, s: (i, j, n, 0))
    k_prev = pl.BlockSpec((1, 1, WINDOW, HEAD_DIM), lambda i, j, n, s: (i, j, before(n), 0))
    v_own = pl.BlockSpec((1, 1, 1, HEAD_DIM, WINDOW), lambda i, j, n, s: (i, j, n, 0, 0))
    v_prev = pl.BlockSpec((1, 1, 1, HEAD_DIM, WINDOW), lambda i, j, n, s: (i, j, before(n), 0, 0))
    return pl.pallas_call(
        _swa_prompt_kernel,
        grid_spec=pltpu.PrefetchScalarGridSpec(
            num_scalar_prefetch=1,
            grid=(b, KV_HEADS, nb),
            in_specs=[q_spec, k_prev, k_own, v_prev, v_own],
            out_specs=q_spec,
        ),
        out_shape=jax.ShapeDtypeStruct(q_t.shape, BF16),
        compiler_params=_params("arbitrary", "arbitrary", "arbitrary"),
        name="swa_prompt",
    )(sinks, q_t, k_hm, k_hm, v_t, v_t)


def _fold_heads(acc):
    rows = acc.shape[0]
    lane_head = lax.broadcasted_iota(I32, (rows, KV_DIM), 1) // HEAD_DIM
    row_head = (lax.broadcasted_iota(I32, (rows, KV_DIM), 0) // GROUP) % KV_HEADS
    a = jnp.where(lane_head == row_head, acc, 0.0)
    a = a[:, :LANES] + a[:, LANES:]
    a = a + pltpu.roll(a, HEAD_DIM, 1)
    return a[:, :HEAD_DIM]


def _moba_sample_kernel(pt_ref, lhs_ref, kn_ref, vn_ref, kpool, vpool, o_ref,
                        buf, sem, s_scr, g_scr, stat_scr, l_scr, acc_scr, *, n_pages, chunk_pages):
    b = pl.program_id(0)
    c = pl.program_id(1)
    n_seq = pl.num_programs(0)
    n_chunks = n_pages // chunk_pages
    steps = 2 * n_chunks
    g = b * steps + c
    slot = g % 2
    rows = o_ref.shape[1]
    gate_rows = (lhs_ref.shape[1] - rows) // 2
    blk_pages = MOBA_BLOCK // PAGE_SIZE
    nb = n_pages // blk_pages

    def page_copy(pool, page, slot_, p):
        return pltpu.make_async_copy(pool.at[page], buf.at[slot_, p], sem.at[slot_])

    def start(step, seq, slot_):
        base = jnp.where(step >= n_chunks, step - n_chunks, step) * chunk_pages
        pages = [pt_ref[seq, base + p] for p in range(chunk_pages)]

        @pl.when(step < n_chunks)
        def _():
            for p in range(chunk_pages):
                page_copy(kpool, pages[p], slot_, p).start()

        @pl.when(step >= n_chunks)
        def _():
            for p in range(chunk_pages):
                page_copy(vpool, pages[p], slot_, p).start()

    @pl.when(g == 0)
    def _():
        start(c, b, slot)

    nxt = g + 1

    @pl.when(nxt < n_seq * steps)
    def _():
        start(nxt % steps, nxt // steps, nxt % 2)

    for p in range(chunk_pages):
        page_copy(kpool, 0, slot, p).wait()

    lhs = lhs_ref[0]
    qbd = lhs[:rows]

    @pl.when(c < n_chunks)
    def _():
        for p in range(chunk_pages):
            res = _dot(lhs, buf[slot, p].astype(BF16))
            s_scr[c * chunk_pages + p] = res[:rows]
            g_scr[c * chunk_pages + p] = res[rows:]

    @pl.when(c == n_chunks - 1)
    def _():
        lane = lax.broadcasted_iota(I32, (gate_rows, LANES), 1).astype(F32)
        gate = jnp.full((gate_rows, LANES), NEG_INF, F32)
        for n in range(nb):
            part = g_scr[n * blk_pages]
            for j in range(1, blk_pages):
                part = part + g_scr[n * blk_pages + j]
            col = jnp.sum(part[:gate_rows] + part[gate_rows:], axis=-1, keepdims=True) / MOBA_BLOCK
            gate = jnp.where(lane == n, col, gate)
        sel = jnp.zeros(gate.shape, F32)
        for _ in range(min(MOBA_TOPK, nb)):
            top = jnp.max(gate, axis=-1, keepdims=True)
            first = jnp.min(jnp.where(gate == top, lane, BIG_INDEX), axis=-1, keepdims=True)
            hit = lane == first
            sel = jnp.where(hit, 1.0, sel)
            gate = jnp.where(hit, NEG_INF, gate)
        sel_rows = jnp.concatenate(
            [jnp.broadcast_to(sel[r:r + 1], (GROUP, LANES)) for r in range(sel.shape[0])], axis=0)
        s_new = _dot_nt(qbd, kn_ref[0].astype(BF16))
        qrow = lax.broadcasted_iota(I32, s_new.shape, 0) // N_HEADS
        s_new = jnp.where(lax.broadcasted_iota(I32, s_new.shape, 1) <= qrow, s_new, NEG_INF)
        top = s_new
        for n in range(nb):
            keep = sel_rows[:, n:n + 1] > 0.0
            for j in range(blk_pages):
                s = jnp.where(keep, s_scr[n * blk_pages + j], NEG_INF)
                s_scr[n * blk_pages + j] = s
                top = jnp.maximum(top, s)
        m_run = jnp.max(top, axis=-1, keepdims=True)
        p_new = jnp.exp(s_new - m_run)
        stat_scr[:, 0:1] = m_run
        stat_scr[:, 1:2] = jnp.sum(p_new, axis=-1, keepdims=True)
        l_scr[...] = jnp.zeros(l_scr.shape, F32)
        acc_scr[...] = _dot(p_new.astype(BF16), vn_ref[0].astype(BF16))

    @pl.when(c >= n_chunks)
    def _():
        m_run = stat_scr[:, 0:1]
        l = l_scr[...]
        acc = acc_scr[...]
        for p in range(chunk_pages):
            prob = jnp.exp(s_scr[(c - n_chunks) * chunk_pages + p] - m_run)
            l = l + prob
            acc = acc + _dot_nt(prob.astype(BF16), buf[slot, p].astype(BF16))
        l_scr[...] = l
        acc_scr[...] = acc

    @pl.when(c == steps - 1)
    def _():
        den = jnp.sum(l_scr[...], axis=-1, keepdims=True) + stat_scr[:, 1:2]
        o_ref[0] = _fold_heads(acc_scr[...] / den)


def _moba_sample(page_table, lhs, k_new, v_new, kpool_t, vpool_t):
    db = lhs.shape[0]
    rows = lhs.shape[1] * N_HEADS // (N_HEADS + 2 * KV_HEADS)
    n_pages = page_table.shape[1]
    chunk_pages = n_pages // 2
    nb = n_pages * PAGE_SIZE // MOBA_BLOCK
    assert nb <= LANES and chunk_pages % (MOBA_BLOCK // PAGE_SIZE) == 0
    steps = 2 * (n_pages // chunk_pages)
    seq3 = lambda a: pl.BlockSpec((1,) + a.shape[1:], lambda i, c, pt: (i, 0, 0))
    kern = functools.partial(_moba_sample_kernel, n_pages=n_pages, chunk_pages=chunk_pages)
    return pl.pallas_call(
        kern,
        grid_spec=pltpu.PrefetchScalarGridSpec(
            num_scalar_prefetch=1,
            grid=(db, steps),
            in_specs=[seq3(lhs), seq3(k_new), seq3(v_new),
                      pl.BlockSpec(memory_space=pl.ANY), pl.BlockSpec(memory_space=pl.ANY)],
            out_specs=pl.BlockSpec((1, rows, HEAD_DIM), lambda i, c, pt: (i, 0, 0)),
            scratch_shapes=[
                pltpu.VMEM((2, chunk_pages, KV_DIM, PAGE_SIZE), F32),
                pltpu.SemaphoreType.DMA((2,)),
                pltpu.VMEM((n_pages, rows, PAGE_SIZE), F32),
                pltpu.VMEM((n_pages, lhs.shape[1] - rows, PAGE_SIZE), F32),
                pltpu.VMEM((rows, LANES), F32),
                pltpu.VMEM((rows, LANES), F32),
                pltpu.VMEM((rows, KV_DIM), F32),
            ],
        ),
        out_shape=jax.ShapeDtypeStruct((db, rows, HEAD_DIM), F32),
        compiler_params=_params("arbitrary", "arbitrary"),
        name="moba_sample",
    )(page_table, lhs, k_new, v_new, kpool_t, vpool_t)


def _swa_sample_kernel(qbd_ref, sink_ref, kb_ref, vb_ref, kn_ref, vn_ref, o_ref):
    rows = qbd_ref.shape[1]
    wb = kb_ref.shape[1]
    ds = kn_ref.shape[1]
    sink = sink_ref[...]
    qrow_b = lax.broadcasted_iota(I32, (rows, wb), 0) // N_HEADS
    buf_ok = lax.broadcasted_iota(I32, (rows, wb), 1) >= qrow_b
    qrow_n = lax.broadcasted_iota(I32, (rows, ds), 0) // N_HEADS
    new_ok = lax.broadcasted_iota(I32, (rows, ds), 1) <= qrow_n
    for s in range(qbd_ref.shape[0]):
        qbd = qbd_ref[s]
        sb = jnp.where(buf_ok, _dot_nt(qbd, kb_ref[s].astype(BF16)), NEG_INF)
        sn = jnp.where(new_ok, _dot_nt(qbd, kn_ref[s].astype(BF16)), NEG_INF)
        m = jnp.maximum(jnp.maximum(jnp.max(sb, axis=-1, keepdims=True),
                                    jnp.max(sn, axis=-1, keepdims=True)), sink)
        pb = jnp.exp(sb - m)
        pn = jnp.exp(sn - m)
        den = (jnp.sum(pb, axis=-1, keepdims=True) + jnp.sum(pn, axis=-1, keepdims=True)
               + jnp.exp(sink - m))
        acc = _dot(pb.astype(BF16), vb_ref[s].astype(BF16)) + _dot(pn.astype(BF16), vn_ref[s].astype(BF16))
        o_ref[s] = _fold_heads(acc / den)


def _swa_sample(qbd, sink_rows, buf_k, buf_v, k_new, v_new, seqs_per_step=8):
    db, rows, _ = qbd.shape
    wb = buf_k.shape[1]
    ds = k_new.shape[1]
    spec = lambda r: pl.BlockSpec((seqs_per_step, r, KV_DIM), lambda i: (i, 0, 0))
    return pl.pallas_call(
        _swa_sample_kernel,
        grid=(db // seqs_per_step,),
        in_specs=[spec(rows), pl.BlockSpec((rows, 1), lambda i: (0, 0)),
                  spec(wb), spec(wb), spec(ds), spec(ds)],
        out_specs=pl.BlockSpec((seqs_per_step, rows, HEAD_DIM), lambda i: (i, 0, 0)),
        out_shape=jax.ShapeDtypeStruct((db, rows, HEAD_DIM), F32),
        compiler_params=_params("arbitrary"),
        name="swa_sample",
    )(qbd, sink_rows, buf_k, buf_v, k_new, v_new)


def _attn_out_kernel(a_ref, x_ref, g_ref, sh_ref, sc_ref, wo_ref, bo_ref, lg_ref, lb_ref,
                     wrh_ref, wrl_ref, xn_ref, h_ref, lt_ref):
    o = _dot(a_ref[...], wo_ref[...]) + bo_ref[...]
    xn = _layer_norm(ALPHA * x_ref[...] + g_ref[0] * o, lg_ref[...], lb_ref[...])
    xn_ref[...] = xn
    h = xn * (1.0 + sc_ref[0]) + sh_ref[0]
    h_ref[...] = _pack_rows(h)
    h_hi, h_lo = _split_bf16(h)
    wrh = wrh_ref[...]
    lt_ref[...] = _dot_nt(wrh, h_hi) + _dot_nt(wrh, h_lo) + _dot_nt(wrl_ref[...], h_hi)


def _attn_out(a, x, gate, shift, scale, wo_bf, bo, ln_g, ln_b, wr_hi, wr_lo, tm):
    n = x.shape[0]
    per_mod = n // tm // gate.shape[0]
    mod_rows = gate.shape[1]
    row = pl.BlockSpec((tm, D_MODEL), lambda i: (i, 0))
    mod = pl.BlockSpec((1, mod_rows, D_MODEL), lambda i: (i // per_mod, 0, 0))
    vec = pl.BlockSpec((1, D_MODEL), lambda i: (0, 0))
    full = lambda r, c: pl.BlockSpec((r, c), lambda i: (0, 0))
    return pl.pallas_call(
        _attn_out_kernel,
        grid=(n // tm,),
        in_specs=[row, row, mod, mod, mod, full(Q_DIM, D_MODEL), vec, vec, vec,
                  full(N_EXPERTS, D_MODEL), full(N_EXPERTS, D_MODEL)],
        out_specs=[row, pl.BlockSpec((tm, PACK_WIDTH), lambda i: (i, 0)),
                   pl.BlockSpec((N_EXPERTS, tm), lambda i: (0, i))],
        out_shape=[jax.ShapeDtypeStruct((n, D_MODEL), F32),
                   jax.ShapeDtypeStruct((n, PACK_WIDTH), PACK_DTYPE),
                   jax.ShapeDtypeStruct((N_EXPERTS, n), F32)],
        compiler_params=_params("arbitrary"),
        name="attn_out_norm",
    )(a, x, gate, shift, scale, wo_bf, bo.reshape(1, -1), ln_g.reshape(1, -1), ln_b.reshape(1, -1),
      wr_hi, wr_lo)


def _first_max(x, idx):
    top = jnp.max(x, axis=0, keepdims=True)
    first = jnp.min(jnp.where(x == top, idx, BIG_INDEX), axis=0, keepdims=True)
    return top, first


def _route_kernel(lt_ref, b_ref, idx_ref, w_ref, pos_ref, cnt_ref):
    i = pl.program_id(0)
    tn = lt_ref.shape[1]
    scores = _sigmoid(lt_ref[...])
    biased = scores + b_ref[...]
    e_idx = lax.broadcasted_iota(I32, (N_EXPERTS, tn), 0).astype(F32)
    l_idx = lax.broadcasted_iota(I32, (PER_GROUP, tn), 0).astype(F32)
    g_score = []
    for g in range(N_EXPERT_GROUPS):
        x = biased[g * PER_GROUP:(g + 1) * PER_GROUP]
        top1, first = _first_max(x, l_idx)
        top2 = jnp.max(jnp.where(l_idx == first, NEG_INF, x), axis=0, keepdims=True)
        g_score.append(top1 + top2)
    pieces = []
    for g in range(N_EXPERT_GROUPS):
        ahead = jnp.zeros((1, tn), F32)
        for o in range(N_EXPERT_GROUPS):
            if o != g:
                wins = (g_score[o] >= g_score[g]) if o < g else (g_score[o] > g_score[g])
                ahead = ahead + jnp.where(wins, 1.0, 0.0)
        pieces.append(jnp.where(ahead < TOPK_GROUPS, biased[g * PER_GROUP:(g + 1) * PER_GROUP], NEG_INF))
    masked = jnp.concatenate(pieces, axis=0)
    chosen = jnp.zeros((N_EXPERTS, tn), F32)
    firsts, picked = [], []
    for _ in range(TOP_K):
        _, first = _first_max(masked, e_idx)
        hit = e_idx == first
        firsts.append(first)
        picked.append(jnp.sum(jnp.where(hit, scores, 0.0), axis=0, keepdims=True))
        chosen = jnp.where(hit, 1.0, chosen)
        masked = jnp.where(hit, NEG_INF, masked)
    total = picked[0]
    for s in picked[1:]:
        total = total + s
    for r in range(TOP_K):
        idx_ref[r:r + 1, :] = firsts[r].astype(I32)
        w_ref[r:r + 1, :] = picked[r] / total * ROUTED_SCALE

    @pl.when(i == 0)
    def _():
        cnt_ref[...] = jnp.zeros(cnt_ref.shape, F32)
    r = lax.broadcasted_iota(I32, (tn, tn), 0)
    c = lax.broadcasted_iota(I32, (tn, tn), 1)
    before = jnp.where(r < c, 1.0, 0.0).astype(BF16)
    prior = _dot(chosen.astype(BF16), before) + cnt_ref[...]
    for r in range(TOP_K):
        pos_ref[r:r + 1, :] = jnp.sum(jnp.where(e_idx == firsts[r], prior, 0.0), axis=0,
                                      keepdims=True).astype(I32)
    cnt_ref[...] = cnt_ref[...] + jnp.sum(chosen, axis=1, keepdims=True)


def _route(logits_t, b_router, tn=512):
    n = logits_t.shape[1]
    slot = pl.BlockSpec((TOP_K, tn), lambda i: (0, i))
    return pl.pallas_call(
        _route_kernel,
        grid=(n // tn,),
        in_specs=[pl.BlockSpec((N_EXPERTS, tn), lambda i: (0, i)),
                  pl.BlockSpec((N_EXPERTS, 1), lambda i: (0, 0))],
        out_specs=[slot, slot, slot, pl.BlockSpec((N_EXPERTS, 1), lambda i: (0, 0))],
        out_shape=[jax.ShapeDtypeStruct((TOP_K, n), I32),
                   jax.ShapeDtypeStruct((TOP_K, n), F32),
                   jax.ShapeDtypeStruct((TOP_K, n), I32),
                   jax.ShapeDtypeStruct((N_EXPERTS, 1), F32)],
        compiler_params=_params("arbitrary"),
        name="route",
    )(logits_t, b_router.reshape(N_EXPERTS, 1))


def _dest_kernel(idx_ref, pos_ref, start_ref, dest_ref):
    tn = idx_ref.shape[1]
    e_idx = lax.broadcasted_iota(I32, (N_EXPERTS, tn), 0)
    start = start_ref[...]
    for r in range(TOP_K):
        first = jnp.sum(jnp.where(e_idx == idx_ref[r:r + 1, :], start, 0.0), axis=0, keepdims=True)
        dest_ref[r:r + 1, :] = first.astype(I32) + pos_ref[r:r + 1, :]


def _dest_rows(idx_t, pos_t, pad_start, tn=512):
    n = idx_t.shape[1]
    slot = pl.BlockSpec((TOP_K, tn), lambda i: (0, i))
    return pl.pallas_call(
        _dest_kernel,
        grid=(n // tn,),
        in_specs=[slot, slot, pl.BlockSpec((N_EXPERTS, 1), lambda i: (0, 0))],
        out_specs=slot,
        out_shape=jax.ShapeDtypeStruct((TOP_K, n), I32),
        compiler_params=_params("arbitrary"),
        name="dest_rows",
    )(idx_t, pos_t, pad_start.astype(F32).reshape(N_EXPERTS, 1))


def _moe_kernel(be_ref, nu_ref, x_ref, wg_ref, wu_ref, wd_ref, o_ref, wg_bf, wu_bf, wd_bf):
    i = pl.program_id(0)
    used = i < nu_ref[0]
    changed = (i == 0) | (be_ref[i] != be_ref[jnp.maximum(i - 1, 0)])

    @pl.when(used & changed)
    def _():
        wg_bf[...] = wg_ref[0, 0].astype(BF16)
        wu_bf[...] = wu_ref[0, 0].astype(BF16)
        wd_bf[...] = wd_ref[0, 0].astype(BF16)

    @pl.when(used)
    def _():
        x = jnp.concatenate(_unpack_rows(x_ref[...]), axis=1).astype(BF16)
        gate = _dot(x, wg_bf[...])
        up = _dot(x, wu_bf[...])
        act = gate * _sigmoid(gate) * up
        o_ref[...] = _pack_rows(_dot(act.astype(BF16), wd_bf[...]))

    @pl.when(jnp.logical_not(used))
    def _():
        o_ref[...] = jnp.zeros(o_ref.shape, o_ref.dtype)


def _moe_experts(blk_e, n_used, x_sorted, w_gate, w_up, w_down, layer):
    rows = x_sorted.shape[0]
    n_blocks = rows // MOE_ROWS
    x_map = lambda i, be, nu: (jnp.minimum(i, nu[0] - 1), 0)
    w_map = lambda i, be, nu: (layer, be[i], 0, 0)
    return pl.pallas_call(
        _moe_kernel,
        grid_spec=pltpu.PrefetchScalarGridSpec(
            num_scalar_prefetch=2,
            grid=(n_blocks,),
            in_specs=[pl.BlockSpec((MOE_ROWS, PACK_WIDTH), x_map),
                      pl.BlockSpec((1, 1, D_MODEL, D_EXPERT), w_map),
                      pl.BlockSpec((1, 1, D_MODEL, D_EXPERT), w_map),
                      pl.BlockSpec((1, 1, D_EXPERT, D_MODEL), w_map)],
            out_specs=pl.BlockSpec((MOE_ROWS, PACK_WIDTH), lambda i, be, nu: (i, 0)),
            scratch_shapes=[pltpu.VMEM((D_MODEL, D_EXPERT), BF16),
                            pltpu.VMEM((D_MODEL, D_EXPERT), BF16),
                            pltpu.VMEM((D_EXPERT, D_MODEL), BF16)],
        ),
        out_shape=jax.ShapeDtypeStruct((rows, PACK_WIDTH), PACK_DTYPE),
        compiler_params=_params("arbitrary"),
        name="moe_experts",
    )(blk_e, n_used, x_sorted, w_gate, w_up, w_down)


def _ffn_out_kernel(h_ref, y_ref, w_ref, x_ref, g_ref, wsg_ref, wsu_ref, wsd_ref, lg_ref, lb_ref, o_ref):
    h = jnp.concatenate(_unpack_rows(h_ref[...]), axis=1).astype(BF16)
    gate = _dot(h, wsg_ref[...])
    up = _dot(h, wsu_ref[...])
    shared = _dot((gate * _sigmoid(gate) * up).astype(BF16), wsd_ref[...])
    w = w_ref[...]
    first, second = None, None
    for r in range(TOP_K):
        ya, yb = _unpack_rows(y_ref[r])
        wr = w[:, r:r + 1]
        first = ya * wr if first is None else first + ya * wr
        second = yb * wr if second is None else second + yb * wr
    ffn = shared + jnp.concatenate([first, second], axis=1)
    y = ALPHA * x_ref[...] + g_ref[0] * ffn
    o_ref[...] = _layer_norm(y, lg_ref[...], lb_ref[...])


def _ffn_out(h, y_tok, w_tok, first_row, x, gate, wsg, wsu, wsd, ln_g, ln_b, tm):
    n = x.shape[0]
    per_mod = n // tm // gate.shape[0]
    off = first_row // tm
    row = pl.BlockSpec((tm, D_MODEL), lambda i: (i, 0))
    mod = pl.BlockSpec((1, gate.shape[1], D_MODEL), lambda i: (i // per_mod, 0, 0))
    vec = pl.BlockSpec((1, D_MODEL), lambda i: (0, 0))
    full = lambda a: pl.BlockSpec(a.shape, lambda i: (0, 0))
    return pl.pallas_call(
        _ffn_out_kernel,
        grid=(n // tm,),
        in_specs=[pl.BlockSpec((tm, PACK_WIDTH), lambda i: (i, 0)),
                  pl.BlockSpec((TOP_K, tm, PACK_WIDTH), lambda i: (0, i + off, 0)),
                  pl.BlockSpec((tm, TOP_K), lambda i: (i + off, 0)),
                  row, mod, full(wsg), full(wsu), full(wsd), vec, vec],
        out_specs=row,
        out_shape=jax.ShapeDtypeStruct((n, D_MODEL), F32),
        compiler_params=_params("arbitrary"),
        name="ffn_out_norm",
    )(h, y_tok, w_tok, x, gate, wsg, wsu, wsd, ln_g.reshape(1, -1), ln_b.reshape(1, -1))


def _dispatch(idx_t, pos_t, counts):
    n = idx_t.shape[1]
    counts = counts.reshape(N_EXPERTS).astype(I32)
    padded = (counts + MOE_ROWS - 1) // MOE_ROWS * MOE_ROWS
    pad_end = jnp.cumsum(padded)
    pad_start = pad_end - padded
    dest = _dest_rows(idx_t, pos_t, pad_start)
    n_blocks = (n * TOP_K + N_EXPERTS * (MOE_ROWS - 1)) // MOE_ROWS
    rows = n_blocks * MOE_ROWS
    tok = jnp.broadcast_to(jnp.arange(n, dtype=I32)[None, :], dest.shape)
    row_tok = jnp.zeros((rows,), I32).at[dest.reshape(-1)].set(
        tok.reshape(-1), unique_indices=True, mode="promise_in_bounds")
    n_used = pad_end[-1] // MOE_ROWS
    blk = jnp.minimum(jnp.arange(n_blocks, dtype=I32), n_used - 1) * MOE_ROWS
    blk_e = jnp.minimum(jnp.sum((pad_end[None, :] <= blk[:, None]).astype(I32), axis=1), N_EXPERTS - 1)
    return dest, row_tok, blk_e, n_used.reshape(1).astype(I32)


def _block_diag_rows(x, heads_per_kv):
    db, ds, heads, _ = x.shape
    kv_of_head = jnp.arange(heads) // heads_per_kv
    onehot = (kv_of_head[:, None] == jnp.arange(KV_HEADS)[None, :]).astype(x.dtype)
    out = x[:, :, :, None, :] * onehot[None, None, :, :, None]
    return out.reshape(db, ds * heads, KV_DIM)


def kernel(x_prompt, x_sample, c_prompt, c_sample, cache_moba_k, cache_moba_v, state_swa_k, state_swa_v,
           page_table, w_ada, b_ada, w_qkv, b_qkv, attn_sinks, w_o, b_o, ln_attn_g, ln_attn_b,
           w_router, b_router, w_exp_gate, w_exp_up, w_exp_down, w_sh_gate, w_sh_up, w_sh_down,
           ln_ffn_g, ln_ffn_b):
    bsz, seq, d = x_prompt.shape
    db, ds, _ = x_sample.shape
    n_p, n_s = bsz * seq, db * ds
    past_len = page_table.shape[1] * PAGE_SIZE
    tm_p, tm_s, tm_norm = 512, 256, 256
    depth = w_ada.shape[0]

    mods = _ada_mod(jnp.concatenate([c_prompt, c_sample], 0), w_ada, b_ada)
    tabs_p = _rope_tables(jnp.arange(seq, dtype=I32))
    tabs_s = _rope_tables(jnp.tile(past_len + jnp.arange(ds, dtype=I32), db))

    xp = x_prompt.reshape(n_p, d)
    xs = x_sample.reshape(n_s, d)
    outs = {k: [] for k in ("mkp", "mvp", "mks", "mvs", "skp", "svp", "sks", "svs")}
    for i in range(depth):
        j = i // 2
        m = mods[i].reshape(bsz + db, N_MOD, d)
        mp = [m[:bsz, t][:, None, :] for t in range(N_MOD)]
        ms = [jnp.repeat(m[bsz:, t], ds, axis=0).reshape(n_s // tm_s, tm_s, d)
              for t in range(N_MOD)]
        w_qkv_bf = w_qkv[i].astype(BF16)
        moba = i % 2 == 0
        res_p = _qkv_project(xp, mp[0], mp[1], w_qkv_bf, b_qkv[i], tabs_p, tm_p, moba)
        res_s = _qkv_project(xs, ms[0], ms[1], w_qkv_bf, b_qkv[i], tabs_s, tm_s, moba)
        qp, kp, vp, kp_bf, vp_bf = res_p[:5]
        qs, ks, vs = res_s[:3]
        blk = MOBA_BLOCK if moba else WINDOW
        q_t = qp.reshape(bsz, seq, N_HEADS, HEAD_DIM).transpose(0, 2, 3, 1)
        k_hm = kp_bf.reshape(bsz, seq, KV_HEADS, HEAD_DIM).transpose(0, 2, 1, 3)
        v_t = vp_bf.reshape(bsz, seq // blk, blk, KV_HEADS, HEAD_DIM).transpose(0, 3, 1, 4, 2)
        qbd = _block_diag_rows(qs.reshape(db, ds, N_HEADS, HEAD_DIM), GROUP)
        pad_rows = ((0, 0), (0, LANES - ds), (0, 0))
        ks3 = jnp.pad(ks.reshape(db, ds, KV_DIM), pad_rows)
        vs3 = jnp.pad(vs.reshape(db, ds, KV_DIM), pad_rows)
        kp5 = kp.reshape(bsz, seq, KV_HEADS, HEAD_DIM)
        vp5 = vp.reshape(bsz, seq, KV_HEADS, HEAD_DIM)
        ks5 = ks.reshape(db, ds, KV_HEADS, HEAD_DIM)
        vs5 = vs.reshape(db, ds, KV_HEADS, HEAD_DIM)
        if moba:
            sel = _moba_select(res_p[5].reshape(bsz, seq, KV_DIM), kp.reshape(bsz, seq, KV_DIM))
            a_t = _moba_prompt(q_t, k_hm, v_t, sel)
            qs_hi, qs_lo = _split_bf16(_block_diag_rows(res_s[5].reshape(db, ds, KV_HEADS, HEAD_DIM), 1))
            n_pool = cache_moba_k.shape[1]
            pool_t = lambda pool: pool[j].transpose(0, 2, 3, 1).reshape(n_pool, KV_DIM, PAGE_SIZE)
            a_s = _moba_sample(page_table, jnp.concatenate([qbd, qs_hi, qs_lo], axis=1), ks3, vs3,
                               pool_t(cache_moba_k), pool_t(cache_moba_v))
            outs["mkp"].append(kp5)
            outs["mvp"].append(vp5)
            outs["mks"].append(ks5)
            outs["mvs"].append(vs5)
        else:
            a_t = _swa_prompt(q_t, k_hm, v_t, attn_sinks[j])
            wb = state_swa_k.shape[2]
            buf_k = state_swa_k[j].reshape(db, wb, KV_DIM)
            buf_v = state_swa_v[j].reshape(db, wb, KV_DIM)
            sink_rows = jnp.tile(attn_sinks[j], ds).reshape(ds * N_HEADS, 1)
            a_s = _swa_sample(qbd, sink_rows, buf_k, buf_v, ks3, vs3)
            wbp = min(WINDOW, seq)
            outs["skp"].append(kp5[:, -wbp:])
            outs["svp"].append(vp5[:, -wbp:])
            outs["sks"].append(jnp.concatenate([state_swa_k[j], ks5], axis=1)[:, -wb:])
            outs["svs"].append(jnp.concatenate([state_swa_v[j], vs5], axis=1)[:, -wb:])
        a_p = a_t.transpose(0, 3, 1, 2).reshape(n_p, Q_DIM)
        a_s = a_s.reshape(n_s, Q_DIM).astype(BF16)

        wo_bf = w_o[i].astype(BF16)
        wr_hi, wr_lo = _split_bf16(w_router[i].T)
        args = (wo_bf, b_o[i], ln_attn_g[i], ln_attn_b[i], wr_hi, wr_lo)
        xp, hp, ltp = _attn_out(a_p, xp, mp[2], mp[3], mp[4], *args, tm_norm)
        xs, hs, lts = _attn_out(a_s, xs, ms[2], ms[3], ms[4], *args, tm_s)

        h_all = jnp.concatenate([hp, hs], axis=0)
        idx_t, w_t, pos_t, counts = _route(jnp.concatenate([ltp, lts], axis=1), b_router[i])
        dest, row_tok, blk_e, n_used = _dispatch(idx_t, pos_t, counts)
        y_sorted = _moe_experts(blk_e, n_used, h_all[row_tok], w_exp_gate, w_exp_up, w_exp_down, i)
        y_tok = y_sorted[dest.reshape(-1)].reshape(TOP_K, n_p + n_s, -1)
        w_tok = w_t.T

        ws = (w_sh_gate[i].astype(BF16), w_sh_up[i].astype(BF16), w_sh_down[i].astype(BF16))
        norm = (ln_ffn_g[i], ln_ffn_b[i])
        xp = _ffn_out(hp, y_tok, w_tok, 0, xp, mp[5], *ws, *norm, tm_norm)
        xs = _ffn_out(hs, y_tok, w_tok, n_p, xs, ms[5], *ws, *norm, tm_s)

    st = lambda key: jnp.stack(outs[key])
    return (xp.reshape(bsz, seq, d), xs.reshape(db, ds, d), st("mkp"), st("mvp"), st("mks"), st("mvs"),
            st("skp"), st("svp"), st("sks"), st("svs"))
```

```python
import functools

import jax
import jax.numpy as jnp
from jax import lax
from jax.experimental import pallas as pl
from jax.experimental.pallas import tpu as pltpu

F32 = jnp.float32
BF16 = jnp.bfloat16
I32 = jnp.int32

D_MODEL = 1024
N_HEADS = 16
HEAD_DIM = 64
KV_HEADS = 4
GROUP = N_HEADS // KV_HEADS
Q_DIM = N_HEADS * HEAD_DIM
KV_DIM = KV_HEADS * HEAD_DIM
QKV_DIM = Q_DIM + 2 * KV_DIM
ATTN_SCALE = HEAD_DIM ** -0.5
ROT_DIM = HEAD_DIM // 4
ROPE_THETA = 500000.0
PAGE_SIZE = 128
MOBA_BLOCK = 256
MOBA_TOPK = 3
WINDOW = 128
N_EXPERTS = 256
TOP_K = 8
N_EXPERT_GROUPS = 8
TOPK_GROUPS = 4
PER_GROUP = N_EXPERTS // N_EXPERT_GROUPS
D_EXPERT = 256
ROUTED_SCALE = 2.5
N_MOD = 6
DEPTH = 2
ALPHA = (2 * DEPTH) ** 0.25
LN_EPS = 1e-5

LANES = 128
VMEM_LIMIT = 48 * 1024 * 1024
MOE_ROWS = 512
NEG_INF = float("-inf")
BIG_INDEX = 1e9

_NT = (((1,), (1,)), ((), ()))


def _dot(a, b):
    return jnp.dot(a, b, preferred_element_type=F32)


def _dot_nt(a, b):
    return lax.dot_general(a, b, _NT, preferred_element_type=F32)


def _split_bf16(x):
    hi = x.astype(BF16)
    lo = (x - hi.astype(F32)).astype(BF16)
    return hi, lo


def _sigmoid(x):
    return 1.0 / (1.0 + jnp.exp(-x))


PACK_DTYPE = jnp.uint32
PACK_WIDTH = D_MODEL // 2


def _pack_rows(x):
    half = x.shape[1] // 2
    hi = lax.bitcast_convert_type(x[:, :half].astype(BF16).astype(F32), jnp.uint32)
    lo = lax.bitcast_convert_type(x[:, half:].astype(BF16).astype(F32), jnp.uint32)
    return hi | (lo >> 16)


def _unpack_rows(u):
    hi = lax.bitcast_convert_type(u & jnp.uint32(0xFFFF0000), F32)
    lo = lax.bitcast_convert_type(u << 16, F32)
    return hi, lo


def _params(*sem):
    return pltpu.CompilerParams(dimension_semantics=sem, vmem_limit_bytes=VMEM_LIMIT)


def _layer_norm(y, g, b):
    mu = jnp.mean(y, axis=-1, keepdims=True)
    yc = y - mu
    var = jnp.mean(yc * yc, axis=-1, keepdims=True)
    return yc * lax.rsqrt(var + LN_EPS) * g + b


def _ada_kernel(c_ref, w_ref, b_ref, o_ref):
    c = c_ref[...]
    a_hi, a_lo = _split_bf16(c * _sigmoid(c))
    w_hi, w_lo = _split_bf16(w_ref[0])
    o_ref[0] = _dot(a_hi, w_hi) + _dot(a_lo, w_hi) + _dot(a_hi, w_lo) + b_ref[0]


def _ada_mod(c_all, w_ada, b_ada):
    rows = c_all.shape[0]
    depth = w_ada.shape[0]
    return pl.pallas_call(
        _ada_kernel,
        grid=(depth, N_MOD),
        in_specs=[
            pl.BlockSpec((rows, D_MODEL), lambda l, j: (0, 0)),
            pl.BlockSpec((1, D_MODEL, D_MODEL), lambda l, j: (l, 0, j)),
            pl.BlockSpec((1, 1, D_MODEL), lambda l, j: (l, 0, j)),
        ],
        out_specs=pl.BlockSpec((1, rows, D_MODEL), lambda l, j: (l, 0, j)),
        out_shape=jax.ShapeDtypeStruct((depth, rows, N_MOD * D_MODEL), F32),
        compiler_params=_params("arbitrary", "arbitrary"),
        name="ada_mod",
    )(c_all, w_ada, b_ada.reshape(depth, 1, N_MOD * D_MODEL))


def _qkv_kernel(x_ref, sh_ref, sc_ref, w_ref, b_ref, c_ref, s1_ref, s2_ref,
                q_ref, k_ref, v_ref, kb_ref, vb_ref, *qs_ref):
    u = x_ref[...] * (1.0 + sc_ref[0]) + sh_ref[0]
    qkv = _dot(u.astype(BF16), w_ref[...]) + b_ref[...]
    cos, s1, s2 = c_ref[...], s1_ref[...], s2_ref[...]
    rots = []
    for j in range((Q_DIM + KV_DIM) // LANES):
        blk = qkv[:, j * LANES:(j + 1) * LANES]
        rots.append(blk * cos + pltpu.roll(blk, LANES - ROT_DIM // 2, 1) * s1
                    + pltpu.roll(blk, ROT_DIM // 2, 1) * s2)
    nq = Q_DIM // LANES
    for j in range(nq):
        q_ref[:, j * LANES:(j + 1) * LANES] = (rots[j] * ATTN_SCALE).astype(BF16)
    for j in range(KV_DIM // LANES):
        k_ref[:, j * LANES:(j + 1) * LANES] = rots[nq + j]
        kb_ref[:, j * LANES:(j + 1) * LANES] = rots[nq + j].astype(BF16)
    v = qkv[:, Q_DIM + KV_DIM:]
    v_ref[...] = v
    vb_ref[...] = v.astype(BF16)
    if qs_ref:
        lane = lax.broadcasted_iota(I32, rots[0].shape, 1)
        halves = []
        for kv in range(KV_HEADS):
            t = rots[2 * kv] + rots[2 * kv + 1]
            halves.append(t + pltpu.roll(t, HEAD_DIM, 1))
        for j in range(KV_DIM // LANES):
            qs_ref[0][:, j * LANES:(j + 1) * LANES] = jnp.where(
                lane < HEAD_DIM, halves[2 * j], halves[2 * j + 1])


def _qkv_project(x, shift, scale, w_bf, b, tabs, tm, with_qsum):
    n = x.shape[0]
    n_tab = tabs[0].shape[0] // tm
    per_mod = n // tm // shift.shape[0]
    mod_rows = shift.shape[1]
    row_spec = lambda width: pl.BlockSpec((tm, width), lambda i: (i, 0))
    mod_spec = pl.BlockSpec((1, mod_rows, D_MODEL), lambda i: (i // per_mod, 0, 0))
    tab_spec = pl.BlockSpec((tm, LANES), lambda i: (i % n_tab, 0))
    out_shape = [
        jax.ShapeDtypeStruct((n, Q_DIM), BF16),
        jax.ShapeDtypeStruct((n, KV_DIM), F32),
        jax.ShapeDtypeStruct((n, KV_DIM), F32),
        jax.ShapeDtypeStruct((n, KV_DIM), BF16),
        jax.ShapeDtypeStruct((n, KV_DIM), BF16),
    ]
    out_specs = [row_spec(Q_DIM)] + [row_spec(KV_DIM)] * 4
    if with_qsum:
        out_shape.append(jax.ShapeDtypeStruct((n, KV_DIM), F32))
        out_specs.append(row_spec(KV_DIM))
    return pl.pallas_call(
        _qkv_kernel,
        grid=(n // tm,),
        in_specs=[
            row_spec(D_MODEL), mod_spec, mod_spec,
            pl.BlockSpec((D_MODEL, QKV_DIM), lambda i: (0, 0)),
            pl.BlockSpec((1, QKV_DIM), lambda i: (0, 0)),
            tab_spec, tab_spec, tab_spec,
        ],
        out_specs=out_specs,
        out_shape=out_shape,
        compiler_params=_params("arbitrary"),
        name="qkv_rope",
    )(x, shift, scale, w_bf, b.reshape(1, QKV_DIM), *tabs)


def _rope_tables(pos):
    half = ROT_DIM // 2
    inv_freq = 1.0 / (ROPE_THETA ** (jnp.arange(0, ROT_DIM, 2, dtype=F32) / ROT_DIM))
    ang = pos.astype(F32)[:, None] * inv_freq[None, :]
    cos, sin = jnp.cos(ang), jnp.sin(ang)
    rest = HEAD_DIM - ROT_DIM
    ones = jnp.ones((pos.shape[0], rest), F32)
    zeros = jnp.zeros((pos.shape[0], rest), F32)
    zh = jnp.zeros_like(sin)
    c = jnp.concatenate([cos, cos, ones], -1)
    s1 = jnp.concatenate([-sin, zh, zeros], -1)
    s2 = jnp.concatenate([zh, sin, zeros], -1)
    reps = LANES // HEAD_DIM
    return tuple(jnp.tile(t, (1, reps)) for t in (c, s1, s2))


def _moba_select_kernel(qs_ref, k_ref, sel_ref):
    t_len = k_ref.shape[1]
    nb = t_len // MOBA_BLOCK
    k = k_ref[0]
    nb_pad = 16
    means = jnp.concatenate(
        [jnp.sum(k[n * MOBA_BLOCK:(n + 1) * MOBA_BLOCK], axis=0, keepdims=True) / MOBA_BLOCK
         for n in range(nb)] + [jnp.zeros((nb_pad - nb, KV_DIM), F32)], axis=0)
    lane_head = lax.broadcasted_iota(I32, (nb_pad, KV_DIM), 1) // HEAD_DIM
    q_hi, q_lo = _split_bf16(qs_ref[0])
    pos_blk = lax.broadcasted_iota(I32, (nb, t_len), 1) // MOBA_BLOCK
    blk = lax.broadcasted_iota(I32, (nb, t_len), 0)
    past = blk < pos_blk
    for kv in range(KV_HEADS):
        m_hi, m_lo = _split_bf16(jnp.where(lane_head == kv, means, 0.0))
        gate = (_dot_nt(m_hi, q_hi) + _dot_nt(m_lo, q_hi) + _dot_nt(m_hi, q_lo))[:nb]
        gate = jnp.where(past, gate, NEG_INF)
        rank = jnp.zeros((nb, t_len), F32)
        for m in range(nb):
            other = gate[m:m + 1, :]
            ahead = (other > gate) | ((other == gate) & (blk > m))
            rank = rank + jnp.where(ahead, 1.0, 0.0)
        sel_ref[0, kv] = jnp.where(past & (rank < MOBA_TOPK), 1.0, 0.0)


def _moba_select(qsum, k):
    b, t, _ = k.shape
    nb = t // MOBA_BLOCK
    spec = pl.BlockSpec((1, t, KV_DIM), lambda i: (i, 0, 0))
    return pl.pallas_call(
        _moba_select_kernel,
        grid=(b,),
        in_specs=[spec, spec],
        out_specs=pl.BlockSpec((1, KV_HEADS, nb, t), lambda i: (i, 0, 0, 0)),
        out_shape=jax.ShapeDtypeStruct((b, KV_HEADS, nb, t), F32),
        compiler_params=_params("arbitrary"),
        name="moba_select",
    )(qsum, k)


def _moba_prompt_kernel(q_ref, k_ref, v_ref, sel_ref, o_ref):
    qi = pl.program_id(2)
    tq = MOBA_BLOCK
    cols = GROUP * tq
    q_t = jnp.concatenate([q_ref[0, g] for g in range(GROUP)], axis=1)

    def scores(n):
        kb = k_ref[0, 0, pl.ds(pl.multiple_of(n * tq, tq), tq), :]
        return _dot(kb, q_t)

    key = lax.broadcasted_iota(I32, (tq, cols), 0)
    qry = lax.broadcasted_iota(I32, (tq, cols), 1) & (tq - 1)
    s = jnp.where(key <= qry, scores(qi), NEG_INF)
    m0 = jnp.max(s, axis=0, keepdims=True)
    p = jnp.exp(s - m0)
    l0 = jnp.sum(p, axis=0, keepdims=True)
    acc0 = _dot(v_ref[0, 0, qi], p.astype(BF16))

    def body(n, carry):
        m, l, acc = carry
        chosen = sel_ref[0, 0, pl.ds(n, 1), :]
        keep = jnp.concatenate([chosen] * GROUP, axis=1) > 0.0
        s = jnp.where(keep, scores(n), NEG_INF)
        m_new = jnp.maximum(m, jnp.max(s, axis=0, keepdims=True))
        alpha = jnp.exp(m - m_new)
        p = jnp.exp(s - m_new)
        l = alpha * l + jnp.sum(p, axis=0, keepdims=True)
        acc = alpha * acc + _dot(v_ref[0, 0, n], p.astype(BF16))
        return m_new, l, acc

    _, l, acc = lax.fori_loop(0, qi, body, (m0, l0, acc0))
    _store_token_major(o_ref, acc / l, tq)


def _store_token_major(o_ref, out_t, tq):
    for pair in range(GROUP // 2):
        two = jnp.concatenate([out_t[:, (2 * pair) * tq:(2 * pair + 1) * tq],
                               out_t[:, (2 * pair + 1) * tq:(2 * pair + 2) * tq]], axis=0)
        o_ref[0, :, pair * 2 * HEAD_DIM:(pair + 1) * 2 * HEAD_DIM] = two.T.astype(BF16)


def _moba_prompt(q_t, k_hm, v_t, sel):
    b, _, _, t = q_t.shape
    nb = t // MOBA_BLOCK
    return pl.pallas_call(
        _moba_prompt_kernel,
        grid=(b, KV_HEADS, nb),
        in_specs=[pl.BlockSpec((1, GROUP, HEAD_DIM, MOBA_BLOCK), lambda i, j, n: (i, j, 0, n)),
                  pl.BlockSpec((1, 1, t, HEAD_DIM), lambda i, j, n: (i, j, 0, 0)),
                  pl.BlockSpec((1, 1, nb, HEAD_DIM, MOBA_BLOCK), lambda i, j, n: (i, j, 0, 0, 0)),
                  pl.BlockSpec((1, 1, nb, MOBA_BLOCK), lambda i, j, n: (i, j, 0, n))],
        out_specs=pl.BlockSpec((1, MOBA_BLOCK, GROUP * HEAD_DIM), lambda i, j, n: (i, n, j)),
        out_shape=jax.ShapeDtypeStruct((b, t, Q_DIM), BF16),
        compiler_params=_params("arbitrary", "arbitrary", "arbitrary"),
        name="moba_prompt",
    )(q_t, k_hm, v_t, sel)


def _swa_prompt_kernel(sink_ref, q_ref, kp_ref, k_ref, vp_ref, v_ref, o_ref):
    j = pl.program_id(1)
    n = pl.program_id(2)
    tq = WINDOW
    cols = GROUP * tq
    q_t = jnp.concatenate([q_ref[0, g] for g in range(GROUP)], axis=1)
    keys = jnp.concatenate([kp_ref[0, 0], k_ref[0, 0]], axis=0)
    vals_t = jnp.concatenate([vp_ref[0, 0, 0], v_ref[0, 0, 0]], axis=1)
    key = lax.broadcasted_iota(I32, (2 * tq, cols), 0)
    qry = lax.broadcasted_iota(I32, (2 * tq, cols), 1) & (tq - 1)
    visible = ((key < tq) & (key >= qry)) | ((key >= tq) & (key - tq <= qry))
    first_block = jnp.where(n > 0, 0.0, NEG_INF)
    s = _dot(keys, q_t)
    s = jnp.where(visible, s + jnp.where(key < tq, first_block, 0.0), NEG_INF)
    sink = jnp.concatenate([jnp.full((1, tq), sink_ref[j * GROUP + g], F32) for g in range(GROUP)], axis=1)
    m = jnp.maximum(jnp.max(s, axis=0, keepdims=True), sink)
    p = jnp.exp(s - m)
    den = jnp.sum(p, axis=0, keepdims=True) + jnp.exp(sink - m)
    _store_token_major(o_ref, _dot(vals_t, p.astype(BF16)) / den, tq)


def _swa_prompt(q_t, k_hm, v_t, sinks):
    b, _, _, t = q_t.shape
    nb = t // WINDOW
    before = lambda n: jnp.maximum(n - 1, 0)
    q_spec = pl.BlockSpec((1, GROUP, HEAD_DIM, WINDOW), lambda i, j, n, s: (i, j, 0, n))
    k_own = pl.BlockSpec((1, 1, WINDOW, HEAD_DIM), lambda i, j, n, s: (i, j, n, 0))
    k_prev = pl.BlockSpec((1, 1, WINDOW, HEAD_DIM), lambda i, j, n, s: (i, j, before(n), 0))
    v_own = pl.BlockSpec((1, 1, 1, HEAD_DIM, WINDOW), lambda i, j, n, s: (i, j, n, 0, 0))
    v_prev = pl.BlockSpec((1, 1, 1, HEAD_DIM, WINDOW), lambda i, j, n, s: (i, j, before(n), 0, 0))
    return pl.pallas_call(
        _swa_prompt_kernel,
        grid_spec=pltpu.PrefetchScalarGridSpec(
            num_scalar_prefetch=1,
            grid=(b, KV_HEADS, nb),
            in_specs=[q_spec, k_prev, k_own, v_prev, v_own],
            out_specs=pl.BlockSpec((1, WINDOW, GROUP * HEAD_DIM), lambda i, j, n, s: (i, n, j)),
        ),
        out_shape=jax.ShapeDtypeStruct((b, t, Q_DIM), BF16),
        compiler_params=_params("arbitrary", "arbitrary", "arbitrary"),
        name="swa_prompt",
    )(sinks, q_t, k_hm, k_hm, v_t, v_t)


def _fold_heads(acc):
    rows = acc.shape[0]
    lane_head = lax.broadcasted_iota(I32, (rows, KV_DIM), 1) // HEAD_DIM
    row_head = (lax.broadcasted_iota(I32, (rows, KV_DIM), 0) // GROUP) % KV_HEADS
    a = jnp.where(lane_head == row_head, acc, 0.0)
    a = a[:, :LANES] + a[:, LANES:]
    a = a + pltpu.roll(a, HEAD_DIM, 1)
    return a[:, :HEAD_DIM]


def _moba_sample_kernel(pt_ref, lhs_ref, kn_ref, vn_ref, kpool, vpool, o_ref,
                        buf, sem, s_scr, g_scr, stat_scr, l_scr, acc_scr, *, n_pages, chunk_pages):
    b = pl.program_id(0)
    c = pl.program_id(1)
    n_seq = pl.num_programs(0)
    n_chunks = n_pages // chunk_pages
    steps = 2 * n_chunks
    g = b * steps + c
    slot = g % 2
    rows = o_ref.shape[1]
    gate_rows = (lhs_ref.shape[1] - rows) // 2
    blk_pages = MOBA_BLOCK // PAGE_SIZE
    nb = n_pages // blk_pages

    def page_copy(pool, page, slot_, p):
        return pltpu.make_async_copy(pool.at[page], buf.at[slot_, p], sem.at[slot_])

    def start(step, seq, slot_):
        base = jnp.where(step >= n_chunks, step - n_chunks, step) * chunk_pages
        pages = [pt_ref[seq, base + p] for p in range(chunk_pages)]

        @pl.when(step < n_chunks)
        def _():
            for p in range(chunk_pages):
                page_copy(kpool, pages[p], slot_, p).start(priority=p % 2)

        @pl.when(step >= n_chunks)
        def _():
            for p in range(chunk_pages):
                page_copy(vpool, pages[p], slot_, p).start(priority=p % 2)

    @pl.when(g == 0)
    def _():
        start(c, b, slot)

    nxt = g + 1

    @pl.when(nxt < n_seq * steps)
    def _():
        start(nxt % steps, nxt // steps, nxt % 2)

    for p in range(chunk_pages):
        page_copy(kpool, 0, slot, p).wait()

    lhs = lhs_ref[0]
    qbd = lhs[:rows]

    @pl.when(c < n_chunks)
    def _():
        for p in range(chunk_pages):
            res = _dot(lhs, buf[slot, p].astype(BF16))
            s_scr[c * chunk_pages + p] = res[:rows]
            g_scr[c * chunk_pages + p] = res[rows:]

    @pl.when(c == n_chunks - 1)
    def _():
        lane = lax.broadcasted_iota(I32, (gate_rows, LANES), 1).astype(F32)
        gate = jnp.full((gate_rows, LANES), NEG_INF, F32)
        for n in range(nb):
            part = g_scr[n * blk_pages]
            for j in range(1, blk_pages):
                part = part + g_scr[n * blk_pages + j]
            col = jnp.sum(part[:gate_rows] + part[gate_rows:], axis=-1, keepdims=True) / MOBA_BLOCK
            gate = jnp.where(lane == n, col, gate)
        sel = jnp.zeros(gate.shape, F32)
        for _ in range(min(MOBA_TOPK, nb)):
            top = jnp.max(gate, axis=-1, keepdims=True)
            first = jnp.min(jnp.where(gate == top, lane, BIG_INDEX), axis=-1, keepdims=True)
            hit = lane == first
            sel = jnp.where(hit, 1.0, sel)
            gate = jnp.where(hit, NEG_INF, gate)
        sel_rows = jnp.concatenate(
            [jnp.broadcast_to(sel[r:r + 1], (GROUP, LANES)) for r in range(sel.shape[0])], axis=0)
        s_new = _dot_nt(qbd, kn_ref[0].astype(BF16))
        qrow = lax.broadcasted_iota(I32, s_new.shape, 0) // N_HEADS
        s_new = jnp.where(lax.broadcasted_iota(I32, s_new.shape, 1) <= qrow, s_new, NEG_INF)
        top = s_new
        for n in range(nb):
            keep = sel_rows[:, n:n + 1] > 0.0
            for j in range(blk_pages):
                s = jnp.where(keep, s_scr[n * blk_pages + j], NEG_INF)
                s_scr[n * blk_pages + j] = s
                top = jnp.maximum(top, s)
        m_run = jnp.max(top, axis=-1, keepdims=True)
        p_new = jnp.exp(s_new - m_run)
        stat_scr[:, 0:1] = m_run
        stat_scr[:, 1:2] = jnp.sum(p_new, axis=-1, keepdims=True)
        l_scr[...] = jnp.zeros(l_scr.shape, F32)
        acc_scr[...] = _dot(p_new.astype(BF16), vn_ref[0].astype(BF16))

    @pl.when(c >= n_chunks)
    def _():
        m_run = stat_scr[:, 0:1]
        l = l_scr[...]
        acc = acc_scr[...]
        for p in range(chunk_pages):
            prob = jnp.exp(s_scr[(c - n_chunks) * chunk_pages + p] - m_run)
            l = l + prob
            acc = acc + _dot_nt(prob.astype(BF16), buf[slot, p].astype(BF16))
        l_scr[...] = l
        acc_scr[...] = acc

    @pl.when(c == steps - 1)
    def _():
        den = jnp.sum(l_scr[...], axis=-1, keepdims=True) + stat_scr[:, 1:2]
        o_ref[0] = _fold_heads(acc_scr[...] / den)


def _moba_sample(page_table, lhs, k_new, v_new, kpool_t, vpool_t):
    db = lhs.shape[0]
    rows = lhs.shape[1] * N_HEADS // (N_HEADS + 2 * KV_HEADS)
    n_pages = page_table.shape[1]
    chunk_pages = n_pages // 2
    nb = n_pages * PAGE_SIZE // MOBA_BLOCK
    assert nb <= LANES and chunk_pages % (MOBA_BLOCK // PAGE_SIZE) == 0
    steps = 2 * (n_pages // chunk_pages)
    seq3 = lambda a: pl.BlockSpec((1,) + a.shape[1:], lambda i, c, pt: (i, 0, 0))
    kern = functools.partial(_moba_sample_kernel, n_pages=n_pages, chunk_pages=chunk_pages)
    return pl.pallas_call(
        kern,
        grid_spec=pltpu.PrefetchScalarGridSpec(
            num_scalar_prefetch=1,
            grid=(db, steps),
            in_specs=[seq3(lhs), seq3(k_new), seq3(v_new),
                      pl.BlockSpec(memory_space=pl.ANY), pl.BlockSpec(memory_space=pl.ANY)],
            out_specs=pl.BlockSpec((1, rows, HEAD_DIM), lambda i, c, pt: (i, 0, 0)),
            scratch_shapes=[
                pltpu.VMEM((2, chunk_pages, KV_DIM, PAGE_SIZE), F32),
                pltpu.SemaphoreType.DMA((2,)),
                pltpu.VMEM((n_pages, rows, PAGE_SIZE), F32),
                pltpu.VMEM((n_pages, lhs.shape[1] - rows, PAGE_SIZE), F32),
                pltpu.VMEM((rows, LANES), F32),
                pltpu.VMEM((rows, LANES), F32),
                pltpu.VMEM((rows, KV_DIM), F32),
            ],
        ),
        out_shape=jax.ShapeDtypeStruct((db, rows, HEAD_DIM), F32),
        compiler_params=_params("arbitrary", "arbitrary"),
        name="moba_sample",
    )(page_table, lhs, k_new, v_new, kpool_t, vpool_t)


def _swa_sample_kernel(qbd_ref, sink_ref, kb_ref, vb_ref, kn_ref, vn_ref, o_ref):
    rows = qbd_ref.shape[1]
    wb = kb_ref.shape[1]
    ds = kn_ref.shape[1]
    sink = sink_ref[...]
    qrow_b = lax.broadcasted_iota(I32, (rows, wb), 0) // N_HEADS
    buf_ok = lax.broadcasted_iota(I32, (rows, wb), 1) >= qrow_b
    qrow_n = lax.broadcasted_iota(I32, (rows, ds), 0) // N_HEADS
    new_ok = lax.broadcasted_iota(I32, (rows, ds), 1) <= qrow_n
    for s in range(qbd_ref.shape[0]):
        qbd = qbd_ref[s]
        sb = jnp.where(buf_ok, _dot_nt(qbd, kb_ref[s].astype(BF16)), NEG_INF)
        sn = jnp.where(new_ok, _dot_nt(qbd, kn_ref[s].astype(BF16)), NEG_INF)
        m = jnp.maximum(jnp.maximum(jnp.max(sb, axis=-1, keepdims=True),
                                    jnp.max(sn, axis=-1, keepdims=True)), sink)
        pb = jnp.exp(sb - m)
        pn = jnp.exp(sn - m)
        den = (jnp.sum(pb, axis=-1, keepdims=True) + jnp.sum(pn, axis=-1, keepdims=True)
               + jnp.exp(sink - m))
        acc = _dot(pb.astype(BF16), vb_ref[s].astype(BF16)) + _dot(pn.astype(BF16), vn_ref[s].astype(BF16))
        o_ref[s] = _fold_heads(acc / den)


def _swa_sample(qbd, sink_rows, buf_k, buf_v, k_new, v_new, seqs_per_step=8):
    db, rows, _ = qbd.shape
    wb = buf_k.shape[1]
    ds = k_new.shape[1]
    spec = lambda r: pl.BlockSpec((seqs_per_step, r, KV_DIM), lambda i: (i, 0, 0))
    return pl.pallas_call(
        _swa_sample_kernel,
        grid=(db // seqs_per_step,),
        in_specs=[spec(rows), pl.BlockSpec((rows, 1), lambda i: (0, 0)),
                  spec(wb), spec(wb), spec(ds), spec(ds)],
        out_specs=pl.BlockSpec((seqs_per_step, rows, HEAD_DIM), lambda i: (i, 0, 0)),
        out_shape=jax.ShapeDtypeStruct((db, rows, HEAD_DIM), F32),
        compiler_params=_params("arbitrary"),
        name="swa_sample",
    )(qbd, sink_rows, buf_k, buf_v, k_new, v_new)


def _attn_out_kernel(a_ref, x_ref, g_ref, sh_ref, sc_ref, wo_ref, bo_ref, lg_ref, lb_ref,
                     wrh_ref, wrl_ref, xn_ref, h_ref, lt_ref):
    o = _dot(a_ref[...], wo_ref[...]) + bo_ref[...]
    xn = _layer_norm(ALPHA * x_ref[...] + g_ref[0] * o, lg_ref[...], lb_ref[...])
    xn_ref[...] = xn
    h = xn * (1.0 + sc_ref[0]) + sh_ref[0]
    h_ref[...] = _pack_rows(h)
    h_hi, h_lo = _split_bf16(h)
    wrh = wrh_ref[...]
    lt_ref[...] = _dot_nt(wrh, h_hi) + _dot_nt(wrh, h_lo) + _dot_nt(wrl_ref[...], h_hi)


def _attn_out(a, x, gate, shift, scale, wo_bf, bo, ln_g, ln_b, wr_hi, wr_lo, tm):
    n = x.shape[0]
    per_mod = n // tm // gate.shape[0]
    mod_rows = gate.shape[1]
    row = pl.BlockSpec((tm, D_MODEL), lambda i: (i, 0))
    mod = pl.BlockSpec((1, mod_rows, D_MODEL), lambda i: (i // per_mod, 0, 0))
    vec = pl.BlockSpec((1, D_MODEL), lambda i: (0, 0))
    full = lambda r, c: pl.BlockSpec((r, c), lambda i: (0, 0))
    return pl.pallas_call(
        _attn_out_kernel,
        grid=(n // tm,),
        in_specs=[row, row, mod, mod, mod, full(Q_DIM, D_MODEL), vec, vec, vec,
                  full(N_EXPERTS, D_MODEL), full(N_EXPERTS, D_MODEL)],
        out_specs=[row, pl.BlockSpec((tm, PACK_WIDTH), lambda i: (i, 0)),
                   pl.BlockSpec((N_EXPERTS, tm), lambda i: (0, i))],
        out_shape=[jax.ShapeDtypeStruct((n, D_MODEL), F32),
                   jax.ShapeDtypeStruct((n, PACK_WIDTH), PACK_DTYPE),
                   jax.ShapeDtypeStruct((N_EXPERTS, n), F32)],
        compiler_params=_params("arbitrary"),
        name="attn_out_norm",
    )(a, x, gate, shift, scale, wo_bf, bo.reshape(1, -1), ln_g.reshape(1, -1), ln_b.reshape(1, -1),
      wr_hi, wr_lo)


def _first_max(x, idx):
    top = jnp.max(x, axis=0, keepdims=True)
    first = jnp.min(jnp.where(x == top, idx, BIG_INDEX), axis=0, keepdims=True)
    return top, first


def _route_kernel(lt_ref, b_ref, idx_ref, w_ref, pos_ref, cnt_ref):
    i = pl.program_id(0)
    tn = lt_ref.shape[1]
    scores = _sigmoid(lt_ref[...])
    biased = scores + b_ref[...]
    e_idx = lax.broadcasted_iota(I32, (N_EXPERTS, tn), 0).astype(F32)
    l_idx = lax.broadcasted_iota(I32, (PER_GROUP, tn), 0).astype(F32)
    g_score = []
    for g in range(N_EXPERT_GROUPS):
        x = biased[g * PER_GROUP:(g + 1) * PER_GROUP]
        top1, first = _first_max(x, l_idx)
        top2 = jnp.max(jnp.where(l_idx == first, NEG_INF, x), axis=0, keepdims=True)
        g_score.append(top1 + top2)
    pieces = []
    for g in range(N_EXPERT_GROUPS):
        ahead = jnp.zeros((1, tn), F32)
        for o in range(N_EXPERT_GROUPS):
            if o != g:
                wins = (g_score[o] >= g_score[g]) if o < g else (g_score[o] > g_score[g])
                ahead = ahead + jnp.where(wins, 1.0, 0.0)
        pieces.append(jnp.where(ahead < TOPK_GROUPS, biased[g * PER_GROUP:(g + 1) * PER_GROUP], NEG_INF))
    masked = jnp.concatenate(pieces, axis=0)
    chosen = jnp.zeros((N_EXPERTS, tn), F32)
    firsts, picked = [], []
    for _ in range(TOP_K):
        _, first = _first_max(masked, e_idx)
        hit = e_idx == first
        firsts.append(first)
        picked.append(jnp.sum(jnp.where(hit, scores, 0.0), axis=0, keepdims=True))
        chosen = jnp.where(hit, 1.0, chosen)
        masked = jnp.where(hit, NEG_INF, masked)
    total = picked[0]
    for s in picked[1:]:
        total = total + s
    for r in range(TOP_K):
        idx_ref[r:r + 1, :] = firsts[r].astype(I32)
        w_ref[r:r + 1, :] = picked[r] / total * ROUTED_SCALE

    @pl.when(i == 0)
    def _():
        cnt_ref[...] = jnp.zeros(cnt_ref.shape, F32)
    r = lax.broadcasted_iota(I32, (tn, tn), 0)
    c = lax.broadcasted_iota(I32, (tn, tn), 1)
    before = jnp.where(r < c, 1.0, 0.0).astype(BF16)
    prior = _dot(chosen.astype(BF16), before) + cnt_ref[...]
    for r in range(TOP_K):
        pos_ref[r:r + 1, :] = jnp.sum(jnp.where(e_idx == firsts[r], prior, 0.0), axis=0,
                                      keepdims=True).astype(I32)
    cnt_ref[...] = cnt_ref[...] + jnp.sum(chosen, axis=1, keepdims=True)


def _route(logits_t, b_router, tn=512):
    n = logits_t.shape[1]
    slot = pl.BlockSpec((TOP_K, tn), lambda i: (0, i))
    return pl.pallas_call(
        _route_kernel,
        grid=(n // tn,),
        in_specs=[pl.BlockSpec((N_EXPERTS, tn), lambda i: (0, i)),
                  pl.BlockSpec((N_EXPERTS, 1), lambda i: (0, 0))],
        out_specs=[slot, slot, slot, pl.BlockSpec((N_EXPERTS, 1), lambda i: (0, 0))],
        out_shape=[jax.ShapeDtypeStruct((TOP_K, n), I32),
                   jax.ShapeDtypeStruct((TOP_K, n), F32),
                   jax.ShapeDtypeStruct((TOP_K, n), I32),
                   jax.ShapeDtypeStruct((N_EXPERTS, 1), F32)],
        compiler_params=_params("arbitrary"),
        name="route",
    )(logits_t, b_router.reshape(N_EXPERTS, 1))


def _dest_kernel(idx_ref, pos_ref, start_ref, dest_ref):
    tn = idx_ref.shape[1]
    e_idx = lax.broadcasted_iota(I32, (N_EXPERTS, tn), 0)
    start = start_ref[...]
    for r in range(TOP_K):
        first = jnp.sum(jnp.where(e_idx == idx_ref[r:r + 1, :], start, 0.0), axis=0, keepdims=True)
        dest_ref[r:r + 1, :] = first.astype(I32) + pos_ref[r:r + 1, :]


def _dest_rows(idx_t, pos_t, pad_start, tn=512):
    n = idx_t.shape[1]
    slot = pl.BlockSpec((TOP_K, tn), lambda i: (0, i))
    return pl.pallas_call(
        _dest_kernel,
        grid=(n // tn,),
        in_specs=[slot, slot, pl.BlockSpec((N_EXPERTS, 1), lambda i: (0, 0))],
        out_specs=slot,
        out_shape=jax.ShapeDtypeStruct((TOP_K, n), I32),
        compiler_params=_params("arbitrary"),
        name="dest_rows",
    )(idx_t, pos_t, pad_start.astype(F32).reshape(N_EXPERTS, 1))


def _moe_kernel(be_ref, nu_ref, x_ref, wg_ref, wu_ref, wd_ref, o_ref, wg_bf, wu_bf, wd_bf):
    i = pl.program_id(0)
    used = i < nu_ref[0]
    changed = (i == 0) | (be_ref[i] != be_ref[jnp.maximum(i - 1, 0)])

    @pl.when(used & changed)
    def _():
        wg_bf[...] = wg_ref[0, 0].astype(BF16)
        wu_bf[...] = wu_ref[0, 0].astype(BF16)
        wd_bf[...] = wd_ref[0, 0].astype(BF16)

    @pl.when(used)
    def _():
        x = jnp.concatenate(_unpack_rows(x_ref[...]), axis=1).astype(BF16)
        gate = _dot(x, wg_bf[...])
        up = _dot(x, wu_bf[...])
        act = gate * _sigmoid(gate) * up
        o_ref[...] = _pack_rows(_dot(act.astype(BF16), wd_bf[...]))

    @pl.when(jnp.logical_not(used))
    def _():
        o_ref[...] = jnp.zeros(o_ref.shape, o_ref.dtype)


def _moe_experts(blk_e, n_used, x_sorted, w_gate, w_up, w_down, layer):
    rows = x_sorted.shape[0]
    n_blocks = rows // MOE_ROWS
    x_map = lambda i, be, nu: (jnp.minimum(i, nu[0] - 1), 0)
    w_map = lambda i, be, nu: (layer, be[i], 0, 0)
    return pl.pallas_call(
        _moe_kernel,
        grid_spec=pltpu.PrefetchScalarGridSpec(
            num_scalar_prefetch=2,
            grid=(n_blocks,),
            in_specs=[pl.BlockSpec((MOE_ROWS, PACK_WIDTH), x_map),
                      pl.BlockSpec((1, 1, D_MODEL, D_EXPERT), w_map),
                      pl.BlockSpec((1, 1, D_MODEL, D_EXPERT), w_map),
                      pl.BlockSpec((1, 1, D_EXPERT, D_MODEL), w_map)],
            out_specs=pl.BlockSpec((MOE_ROWS, PACK_WIDTH), lambda i, be, nu: (i, 0)),
            scratch_shapes=[pltpu.VMEM((D_MODEL, D_EXPERT), BF16),
                            pltpu.VMEM((D_MODEL, D_EXPERT), BF16),
                            pltpu.VMEM((D_EXPERT, D_MODEL), BF16)],
        ),
        out_shape=jax.ShapeDtypeStruct((rows, PACK_WIDTH), PACK_DTYPE),
        compiler_params=_params("arbitrary"),
        name="moe_experts",
    )(blk_e, n_used, x_sorted, w_gate, w_up, w_down)


def _ffn_out_kernel(h_ref, y_ref, w_ref, x_ref, g_ref, wsg_ref, wsu_ref, wsd_ref, lg_ref, lb_ref, o_ref):
    h = jnp.concatenate(_unpack_rows(h_ref[...]), axis=1).astype(BF16)
    gate = _dot(h, wsg_ref[...])
    up = _dot(h, wsu_ref[...])
    shared = _dot((gate * _sigmoid(gate) * up).astype(BF16), wsd_ref[...])
    w = w_ref[...]
    first, second = None, None
    for r in range(TOP_K):
        ya, yb = _unpack_rows(y_ref[r])
        wr = w[:, r:r + 1]
        first = ya * wr if first is None else first + ya * wr
        second = yb * wr if second is None else second + yb * wr
    ffn = shared + jnp.concatenate([first, second], axis=1)
    y = ALPHA * x_ref[...] + g_ref[0] * ffn
    o_ref[...] = _layer_norm(y, lg_ref[...], lb_ref[...])


def _ffn_out(h, y_tok, w_tok, first_row, x, gate, wsg, wsu, wsd, ln_g, ln_b, tm):
    n = x.shape[0]
    per_mod = n // tm // gate.shape[0]
    off = first_row // tm
    row = pl.BlockSpec((tm, D_MODEL), lambda i: (i, 0))
    mod = pl.BlockSpec((1, gate.shape[1], D_MODEL), lambda i: (i // per_mod, 0, 0))
    vec = pl.BlockSpec((1, D_MODEL), lambda i: (0, 0))
    full = lambda a: pl.BlockSpec(a.shape, lambda i: (0, 0))
    return pl.pallas_call(
        _ffn_out_kernel,
        grid=(n // tm,),
        in_specs=[pl.BlockSpec((tm, PACK_WIDTH), lambda i: (i, 0)),
                  pl.BlockSpec((TOP_K, tm, PACK_WIDTH), lambda i: (0, i + off, 0)),
                  pl.BlockSpec((tm, TOP_K), lambda i: (i + off, 0)),
                  row, mod, full(wsg), full(wsu), full(wsd), vec, vec],
        out_specs=row,
        out_shape=jax.ShapeDtypeStruct((n, D_MODEL), F32),
        compiler_params=_params("arbitrary"),
        name="ffn_out_norm",
    )(h, y_tok, w_tok, x, gate, wsg, wsu, wsd, ln_g.reshape(1, -1), ln_b.reshape(1, -1))


def _dispatch(idx_t, pos_t, counts):
    n = idx_t.shape[1]
    counts = counts.reshape(N_EXPERTS).astype(I32)
    padded = (counts + MOE_ROWS - 1) // MOE_ROWS * MOE_ROWS
    pad_end = jnp.cumsum(padded)
    pad_start = pad_end - padded
    dest = _dest_rows(idx_t, pos_t, pad_start)
    n_blocks = (n * TOP_K + N_EXPERTS * (MOE_ROWS - 1)) // MOE_ROWS
    rows = n_blocks * MOE_ROWS
    tok = jnp.broadcast_to(jnp.arange(n, dtype=I32)[None, :], dest.shape)
    row_tok = jnp.zeros((rows,), I32).at[dest.reshape(-1)].set(
        tok.reshape(-1), unique_indices=True, mode="promise_in_bounds")
    n_used = pad_end[-1] // MOE_ROWS
    blk = jnp.minimum(jnp.arange(n_blocks, dtype=I32), n_used - 1) * MOE_ROWS
    blk_e = jnp.minimum(jnp.sum((pad_end[None, :] <= blk[:, None]).astype(I32), axis=1), N_EXPERTS - 1)
    return dest, row_tok, blk_e, n_used.reshape(1).astype(I32)


def _block_diag_rows(x, heads_per_kv):
    db, ds, heads, _ = x.shape
    kv_of_head = jnp.arange(heads) // heads_per_kv
    onehot = (kv_of_head[:, None] == jnp.arange(KV_HEADS)[None, :]).astype(x.dtype)
    out = x[:, :, :, None, :] * onehot[None, None, :, :, None]
    return out.reshape(db, ds * heads, KV_DIM)


def kernel(x_prompt, x_sample, c_prompt, c_sample, cache_moba_k, cache_moba_v, state_swa_k, state_swa_v,
           page_table, w_ada, b_ada, w_qkv, b_qkv, attn_sinks, w_o, b_o, ln_attn_g, ln_attn_b,
           w_router, b_router, w_exp_gate, w_exp_up, w_exp_down, w_sh_gate, w_sh_up, w_sh_down,
           ln_ffn_g, ln_ffn_b):
    bsz, seq, d = x_prompt.shape
    db, ds, _ = x_sample.shape
    n_p, n_s = bsz * seq, db * ds
    past_len = page_table.shape[1] * PAGE_SIZE
    tm_p, tm_s, tm_norm = 512, 256, 256
    depth = w_ada.shape[0]

    mods = _ada_mod(jnp.concatenate([c_prompt, c_sample], 0), w_ada, b_ada)
    tabs_p = _rope_tables(jnp.arange(seq, dtype=I32))
    tabs_s = _rope_tables(jnp.tile(past_len + jnp.arange(ds, dtype=I32), db))

    xp = x_prompt.reshape(n_p, d)
    xs = x_sample.reshape(n_s, d)
    outs = {k: [] for k in ("mkp", "mvp", "mks", "mvs", "skp", "svp", "sks", "svs")}
    for i in range(depth):
        j = i // 2
        m = mods[i].reshape(bsz + db, N_MOD, d)
        mp = [m[:bsz, t][:, None, :] for t in range(N_MOD)]
        ms = [jnp.repeat(m[bsz:, t], ds, axis=0).reshape(n_s // tm_s, tm_s, d)
              for t in range(N_MOD)]
        w_qkv_bf = w_qkv[i].astype(BF16)
        moba = i % 2 == 0
        res_p = _qkv_project(xp, mp[0], mp[1], w_qkv_bf, b_qkv[i], tabs_p, tm_p, moba)
        res_s = _qkv_project(xs, ms[0], ms[1], w_qkv_bf, b_qkv[i], tabs_s, tm_s, moba)
        qp, kp, vp, kp_bf, vp_bf = res_p[:5]
        qs, ks, vs = res_s[:3]
        blk = MOBA_BLOCK if moba else WINDOW
        q_t = qp.reshape(bsz, seq, N_HEADS, HEAD_DIM).transpose(0, 2, 3, 1)
        k_hm = kp_bf.reshape(bsz, seq, KV_HEADS, HEAD_DIM).transpose(0, 2, 1, 3)
        v_t = vp_bf.reshape(bsz, seq // blk, blk, KV_HEADS, HEAD_DIM).transpose(0, 3, 1, 4, 2)
        qbd = _block_diag_rows(qs.reshape(db, ds, N_HEADS, HEAD_DIM), GROUP)
        pad_rows = ((0, 0), (0, LANES - ds), (0, 0))
        ks3 = jnp.pad(ks.reshape(db, ds, KV_DIM), pad_rows)
        vs3 = jnp.pad(vs.reshape(db, ds, KV_DIM), pad_rows)
        kp5 = kp.reshape(bsz, seq, KV_HEADS, HEAD_DIM)
        vp5 = vp.reshape(bsz, seq, KV_HEADS, HEAD_DIM)
        ks5 = ks.reshape(db, ds, KV_HEADS, HEAD_DIM)
        vs5 = vs.reshape(db, ds, KV_HEADS, HEAD_DIM)
        if moba:
            sel = _moba_select(res_p[5].reshape(bsz, seq, KV_DIM), kp.reshape(bsz, seq, KV_DIM))
            a_t = _moba_prompt(q_t, k_hm, v_t, sel)
            qs_hi, qs_lo = _split_bf16(_block_diag_rows(res_s[5].reshape(db, ds, KV_HEADS, HEAD_DIM), 1))
            n_pool = cache_moba_k.shape[1]
            pool_t = lambda pool: pool[j].transpose(0, 2, 3, 1).reshape(n_pool, KV_DIM, PAGE_SIZE)
            a_s = _moba_sample(page_table, jnp.concatenate([qbd, qs_hi, qs_lo], axis=1), ks3, vs3,
                               pool_t(cache_moba_k), pool_t(cache_moba_v))
            outs["mkp"].append(kp5)
            outs["mvp"].append(vp5)
            outs["mks"].append(ks5)
            outs["mvs"].append(vs5)
        else:
            a_t = _swa_prompt(q_t, k_hm, v_t, attn_sinks[j])
            wb = state_swa_k.shape[2]
            buf_k = state_swa_k[j].reshape(db, wb, KV_DIM)
            buf_v = state_swa_v[j].reshape(db, wb, KV_DIM)
            sink_rows = jnp.tile(attn_sinks[j], ds).reshape(ds * N_HEADS, 1)
            a_s = _swa_sample(qbd, sink_rows, buf_k, buf_v, ks3, vs3)
            wbp = min(WINDOW, seq)
            outs["skp"].append(kp5[:, -wbp:])
            outs["svp"].append(vp5[:, -wbp:])
            outs["sks"].append(jnp.concatenate([state_swa_k[j], ks5], axis=1)[:, -wb:])
            outs["svs"].append(jnp.concatenate([state_swa_v[j], vs5], axis=1)[:, -wb:])
        a_p = a_t.reshape(n_p, Q_DIM)
        a_s = a_s.reshape(n_s, Q_DIM).astype(BF16)

        wo_bf = w_o[i].astype(BF16)
        wr_hi, wr_lo = _split_bf16(w_router[i].T)
        args = (wo_bf, b_o[i], ln_attn_g[i], ln_attn_b[i], wr_hi, wr_lo)
        xp, hp, ltp = _attn_out(a_p, xp, mp[2], mp[3], mp[4], *args, tm_norm)
        xs, hs, lts = _attn_out(a_s, xs, ms[2], ms[3], ms[4], *args, tm_s)

        h_all = jnp.concatenate([hp, hs], axis=0)
        idx_t, w_t, pos_t, counts = _route(jnp.concatenate([ltp, lts], axis=1), b_router[i])
        dest, row_tok, blk_e, n_used = _dispatch(idx_t, pos_t, counts)
        y_sorted = _moe_experts(blk_e, n_used, h_all[row_tok], w_exp_gate, w_exp_up, w_exp_down, i)
        y_tok = y_sorted[dest.reshape(-1)].reshape(TOP_K, n_p + n_s, -1)
        w_tok = w_t.T

        ws = (w_sh_gate[i].astype(BF16), w_sh_up[i].astype(BF16), w_sh_down[i].astype(BF16))
        norm = (ln_ffn_g[i], ln_ffn_b[i])
        xp = _ffn_out(hp, y_tok, w_tok, 0, xp, mp[5], *ws, *norm, tm_norm)
        xs = _ffn_out(hs, y_tok, w_tok, n_p, xs, ms[5], *ws, *norm, tm_s)

    st = lambda key: jnp.stack(outs[key])
    return (xp.reshape(bsz, seq, d), xs.reshape(db, ds, d), st("mkp"), st("mvp"), st("mks"), st("mvs"),
            st("skp"), st("svp"), st("sks"), st("svs"))
```

```python
import functools

import jax
import jax.numpy as jnp
from jax import lax
from jax.experimental import pallas as pl
from jax.experimental.pallas import tpu as pltpu
from jax.experimental.pallas import tpu_sc as plsc

F32 = jnp.float32
BF16 = jnp.bfloat16
I32 = jnp.int32

D_MODEL = 1024
N_HEADS = 16
HEAD_DIM = 64
KV_HEADS = 4
GROUP = N_HEADS // KV_HEADS
Q_DIM = N_HEADS * HEAD_DIM
KV_DIM = KV_HEADS * HEAD_DIM
QKV_DIM = Q_DIM + 2 * KV_DIM
ATTN_SCALE = HEAD_DIM ** -0.5
ROT_DIM = HEAD_DIM // 4
ROPE_THETA = 500000.0
PAGE_SIZE = 128
MOBA_BLOCK = 256
MOBA_TOPK = 3
WINDOW = 128
N_EXPERTS = 256
TOP_K = 8
N_EXPERT_GROUPS = 8
TOPK_GROUPS = 4
PER_GROUP = N_EXPERTS // N_EXPERT_GROUPS
D_EXPERT = 256
ROUTED_SCALE = 2.5
N_MOD = 6
DEPTH = 2
ALPHA = (2 * DEPTH) ** 0.25
LN_EPS = 1e-5

LANES = 128
VMEM_LIMIT = 48 * 1024 * 1024
MOE_ROWS = 512
NEG_INF = float("-inf")
BIG_INDEX = 1e9

_NT = (((1,), (1,)), ((), ()))


def _dot(a, b):
    return jnp.dot(a, b, preferred_element_type=F32)


def _dot_nt(a, b):
    return lax.dot_general(a, b, _NT, preferred_element_type=F32)


def _split_bf16(x):
    hi = x.astype(BF16)
    lo = (x - hi.astype(F32)).astype(BF16)
    return hi, lo


def _sigmoid(x):
    return 1.0 / (1.0 + jnp.exp(-x))


PACK_DTYPE = jnp.uint32
PACK_WIDTH = D_MODEL // 2


def _pack_rows(x):
    half = x.shape[1] // 2
    hi = lax.bitcast_convert_type(x[:, :half].astype(BF16).astype(F32), jnp.uint32)
    lo = lax.bitcast_convert_type(x[:, half:].astype(BF16).astype(F32), jnp.uint32)
    return hi | (lo >> 16)


def _unpack_rows(u):
    hi = lax.bitcast_convert_type(u & jnp.uint32(0xFFFF0000), F32)
    lo = lax.bitcast_convert_type(u << 16, F32)
    return hi, lo


def _params(*sem):
    return pltpu.CompilerParams(dimension_semantics=sem, vmem_limit_bytes=VMEM_LIMIT)


def _layer_norm(y, g, b):
    mu = jnp.mean(y, axis=-1, keepdims=True)
    yc = y - mu
    var = jnp.mean(yc * yc, axis=-1, keepdims=True)
    return yc * lax.rsqrt(var + LN_EPS) * g + b


def _ada_kernel(c_ref, w_ref, b_ref, o_ref):
    c = c_ref[...]
    a_hi, a_lo = _split_bf16(c * _sigmoid(c))
    w_hi, w_lo = _split_bf16(w_ref[0])
    o_ref[0] = _dot(a_hi, w_hi) + _dot(a_lo, w_hi) + _dot(a_hi, w_lo) + b_ref[0]


def _ada_mod(c_all, w_ada, b_ada):
    rows = c_all.shape[0]
    depth = w_ada.shape[0]
    return pl.pallas_call(
        _ada_kernel,
        grid=(depth, N_MOD),
        in_specs=[
            pl.BlockSpec((rows, D_MODEL), lambda l, j: (0, 0)),
            pl.BlockSpec((1, D_MODEL, D_MODEL), lambda l, j: (l, 0, j)),
            pl.BlockSpec((1, 1, D_MODEL), lambda l, j: (l, 0, j)),
        ],
        out_specs=pl.BlockSpec((1, rows, D_MODEL), lambda l, j: (l, 0, j)),
        out_shape=jax.ShapeDtypeStruct((depth, rows, N_MOD * D_MODEL), F32),
        compiler_params=_params("arbitrary", "arbitrary"),
        name="ada_mod",
    )(c_all, w_ada, b_ada.reshape(depth, 1, N_MOD * D_MODEL))


def _qkv_kernel(x_ref, sh_ref, sc_ref, w_ref, b_ref, c_ref, s1_ref, s2_ref,
                q_ref, k_ref, v_ref, kb_ref, vb_ref, *qs_ref):
    u = x_ref[...] * (1.0 + sc_ref[0]) + sh_ref[0]
    qkv = _dot(u.astype(BF16), w_ref[...]) + b_ref[...]
    cos, s1, s2 = c_ref[...], s1_ref[...], s2_ref[...]
    rots = []
    for j in range((Q_DIM + KV_DIM) // LANES):
        blk = qkv[:, j * LANES:(j + 1) * LANES]
        rots.append(blk * cos + pltpu.roll(blk, LANES - ROT_DIM // 2, 1) * s1
                    + pltpu.roll(blk, ROT_DIM // 2, 1) * s2)
    nq = Q_DIM // LANES
    for j in range(nq):
        q_ref[:, j * LANES:(j + 1) * LANES] = (rots[j] * ATTN_SCALE).astype(BF16)
    for j in range(KV_DIM // LANES):
        k_ref[:, j * LANES:(j + 1) * LANES] = rots[nq + j]
        kb_ref[:, j * LANES:(j + 1) * LANES] = rots[nq + j].astype(BF16)
    v = qkv[:, Q_DIM + KV_DIM:]
    v_ref[...] = v
    vb_ref[...] = v.astype(BF16)
    if qs_ref:
        lane = lax.broadcasted_iota(I32, rots[0].shape, 1)
        halves = []
        for kv in range(KV_HEADS):
            t = rots[2 * kv] + rots[2 * kv + 1]
            halves.append(t + pltpu.roll(t, HEAD_DIM, 1))
        for j in range(KV_DIM // LANES):
            qs_ref[0][:, j * LANES:(j + 1) * LANES] = jnp.where(
                lane < HEAD_DIM, halves[2 * j], halves[2 * j + 1])


def _qkv_project(x, shift, scale, w_bf, b, tabs, tm, with_qsum):
    n = x.shape[0]
    n_tab = tabs[0].shape[0] // tm
    per_mod = n // tm // shift.shape[0]
    mod_rows = shift.shape[1]
    row_spec = lambda width: pl.BlockSpec((tm, width), lambda i: (i, 0))
    mod_spec = pl.BlockSpec((1, mod_rows, D_MODEL), lambda i: (i // per_mod, 0, 0))
    tab_spec = pl.BlockSpec((tm, LANES), lambda i: (i % n_tab, 0))
    out_shape = [
        jax.ShapeDtypeStruct((n, Q_DIM), BF16),
        jax.ShapeDtypeStruct((n, KV_DIM), F32),
        jax.ShapeDtypeStruct((n, KV_DIM), F32),
        jax.ShapeDtypeStruct((n, KV_DIM), BF16),
        jax.ShapeDtypeStruct((n, KV_DIM), BF16),
    ]
    out_specs = [row_spec(Q_DIM)] + [row_spec(KV_DIM)] * 4
    if with_qsum:
        out_shape.append(jax.ShapeDtypeStruct((n, KV_DIM), F32))
        out_specs.append(row_spec(KV_DIM))
    return pl.pallas_call(
        _qkv_kernel,
        grid=(n // tm,),
        in_specs=[
            row_spec(D_MODEL), mod_spec, mod_spec,
            pl.BlockSpec((D_MODEL, QKV_DIM), lambda i: (0, 0)),
            pl.BlockSpec((1, QKV_DIM), lambda i: (0, 0)),
            tab_spec, tab_spec, tab_spec,
        ],
        out_specs=out_specs,
        out_shape=out_shape,
        compiler_params=_params("arbitrary"),
        name="qkv_rope",
    )(x, shift, scale, w_bf, b.reshape(1, QKV_DIM), *tabs)


def _rope_tables(pos):
    half = ROT_DIM // 2
    inv_freq = 1.0 / (ROPE_THETA ** (jnp.arange(0, ROT_DIM, 2, dtype=F32) / ROT_DIM))
    ang = pos.astype(F32)[:, None] * inv_freq[None, :]
    cos, sin = jnp.cos(ang), jnp.sin(ang)
    rest = HEAD_DIM - ROT_DIM
    ones = jnp.ones((pos.shape[0], rest), F32)
    zeros = jnp.zeros((pos.shape[0], rest), F32)
    zh = jnp.zeros_like(sin)
    c = jnp.concatenate([cos, cos, ones], -1)
    s1 = jnp.concatenate([-sin, zh, zeros], -1)
    s2 = jnp.concatenate([zh, sin, zeros], -1)
    reps = LANES // HEAD_DIM
    return tuple(jnp.tile(t, (1, reps)) for t in (c, s1, s2))


def _moba_select_kernel(qs_ref, k_ref, sel_ref):
    t_len = k_ref.shape[1]
    nb = t_len // MOBA_BLOCK
    k = k_ref[0]
    nb_pad = 16
    means = jnp.concatenate(
        [jnp.sum(k[n * MOBA_BLOCK:(n + 1) * MOBA_BLOCK], axis=0, keepdims=True) / MOBA_BLOCK
         for n in range(nb)] + [jnp.zeros((nb_pad - nb, KV_DIM), F32)], axis=0)
    lane_head = lax.broadcasted_iota(I32, (nb_pad, KV_DIM), 1) // HEAD_DIM
    q_hi, q_lo = _split_bf16(qs_ref[0])
    pos_blk = lax.broadcasted_iota(I32, (nb, t_len), 1) // MOBA_BLOCK
    blk = lax.broadcasted_iota(I32, (nb, t_len), 0)
    past = blk < pos_blk
    for kv in range(KV_HEADS):
        m_hi, m_lo = _split_bf16(jnp.where(lane_head == kv, means, 0.0))
        gate = (_dot_nt(m_hi, q_hi) + _dot_nt(m_lo, q_hi) + _dot_nt(m_hi, q_lo))[:nb]
        gate = jnp.where(past, gate, NEG_INF)
        rank = jnp.zeros((nb, t_len), F32)
        for m in range(nb):
            other = gate[m:m + 1, :]
            ahead = (other > gate) | ((other == gate) & (blk > m))
            rank = rank + jnp.where(ahead, 1.0, 0.0)
        sel_ref[0, kv] = jnp.where(past & (rank < MOBA_TOPK), 1.0, 0.0)


def _moba_select(qsum, k):
    b, t, _ = k.shape
    nb = t // MOBA_BLOCK
    spec = pl.BlockSpec((1, t, KV_DIM), lambda i: (i, 0, 0))
    return pl.pallas_call(
        _moba_select_kernel,
        grid=(b,),
        in_specs=[spec, spec],
        out_specs=pl.BlockSpec((1, KV_HEADS, nb, t), lambda i: (i, 0, 0, 0)),
        out_shape=jax.ShapeDtypeStruct((b, KV_HEADS, nb, t), F32),
        compiler_params=_params("arbitrary"),
        name="moba_select",
    )(qsum, k)


def _moba_prompt_kernel(q_ref, k_ref, v_ref, sel_ref, o_ref):
    qi = pl.program_id(2)
    tq = MOBA_BLOCK
    cols = GROUP * tq
    q_t = jnp.concatenate([q_ref[0, g] for g in range(GROUP)], axis=1)

    def scores(n):
        kb = k_ref[0, 0, pl.ds(pl.multiple_of(n * tq, tq), tq), :]
        return _dot(kb, q_t)

    key = lax.broadcasted_iota(I32, (tq, cols), 0)
    qry = lax.broadcasted_iota(I32, (tq, cols), 1) & (tq - 1)
    s = jnp.where(key <= qry, scores(qi), NEG_INF)
    m0 = jnp.max(s, axis=0, keepdims=True)
    p = jnp.exp(s - m0)
    l0 = jnp.sum(p, axis=0, keepdims=True)
    acc0 = _dot(v_ref[0, 0, qi], p.astype(BF16))

    def body(n, carry):
        m, l, acc = carry
        chosen = sel_ref[0, 0, pl.ds(n, 1), :]
        keep = jnp.concatenate([chosen] * GROUP, axis=1) > 0.0
        s = jnp.where(keep, scores(n), NEG_INF)
        m_new = jnp.maximum(m, jnp.max(s, axis=0, keepdims=True))
        alpha = jnp.exp(m - m_new)
        p = jnp.exp(s - m_new)
        l = alpha * l + jnp.sum(p, axis=0, keepdims=True)
        acc = alpha * acc + _dot(v_ref[0, 0, n], p.astype(BF16))
        return m_new, l, acc

    _, l, acc = lax.fori_loop(0, qi, body, (m0, l0, acc0))
    _store_token_major(o_ref, acc / l, tq)


def _store_token_major(o_ref, out_t, tq):
    for pair in range(GROUP // 2):
        two = jnp.concatenate([out_t[:, (2 * pair) * tq:(2 * pair + 1) * tq],
                               out_t[:, (2 * pair + 1) * tq:(2 * pair + 2) * tq]], axis=0)
        o_ref[0, :, pair * 2 * HEAD_DIM:(pair + 1) * 2 * HEAD_DIM] = two.T.astype(BF16)


def _moba_prompt(q_t, k_hm, v_t, sel):
    b, _, _, t = q_t.shape
    nb = t // MOBA_BLOCK
    return pl.pallas_call(
        _moba_prompt_kernel,
        grid=(b, KV_HEADS, nb),
        in_specs=[pl.BlockSpec((1, GROUP, HEAD_DIM, MOBA_BLOCK), lambda i, j, n: (i, j, 0, n)),
                  pl.BlockSpec((1, 1, t, HEAD_DIM), lambda i, j, n: (i, j, 0, 0)),
                  pl.BlockSpec((1, 1, nb, HEAD_DIM, MOBA_BLOCK), lambda i, j, n: (i, j, 0, 0, 0)),
                  pl.BlockSpec((1, 1, nb, MOBA_BLOCK), lambda i, j, n: (i, j, 0, n))],
        out_specs=pl.BlockSpec((1, MOBA_BLOCK, GROUP * HEAD_DIM), lambda i, j, n: (i, n, j)),
        out_shape=jax.ShapeDtypeStruct((b, t, Q_DIM), BF16),
        compiler_params=_params("arbitrary", "arbitrary", "arbitrary"),
        name="moba_prompt",
    )(q_t, k_hm, v_t, sel)


def _swa_prompt_kernel(sink_ref, q_ref, kp_ref, k_ref, vp_ref, v_ref, o_ref):
    j = pl.program_id(1)
    n = pl.program_id(2)
    tq = WINDOW
    cols = GROUP * tq
    q_t = jnp.concatenate([q_ref[0, g] for g in range(GROUP)], axis=1)
    keys = jnp.concatenate([kp_ref[0, 0], k_ref[0, 0]], axis=0)
    vals_t = jnp.concatenate([vp_ref[0, 0, 0], v_ref[0, 0, 0]], axis=1)
    key = lax.broadcasted_iota(I32, (2 * tq, cols), 0)
    qry = lax.broadcasted_iota(I32, (2 * tq, cols), 1) & (tq - 1)
    visible = ((key < tq) & (key >= qry)) | ((key >= tq) & (key - tq <= qry))
    first_block = jnp.where(n > 0, 0.0, NEG_INF)
    s = _dot(keys, q_t)
    s = jnp.where(visible, s + jnp.where(key < tq, first_block, 0.0), NEG_INF)
    sink = jnp.concatenate([jnp.full((1, tq), sink_ref[j * GROUP + g], F32) for g in range(GROUP)], axis=1)
    m = jnp.maximum(jnp.max(s, axis=0, keepdims=True), sink)
    p = jnp.exp(s - m)
    den = jnp.sum(p, axis=0, keepdims=True) + jnp.exp(sink - m)
    _store_token_major(o_ref, _dot(vals_t, p.astype(BF16)) / den, tq)


def _swa_prompt(q_t, k_hm, v_t, sinks):
    b, _, _, t = q_t.shape
    nb = t // WINDOW
    before = lambda n: jnp.maximum(n - 1, 0)
    q_spec = pl.BlockSpec((1, GROUP, HEAD_DIM, WINDOW), lambda i, j, n, s: (i, j, 0, n))
    k_own = pl.BlockSpec((1, 1, WINDOW, HEAD_DIM), lambda i, j, n, s: (i, j, n, 0))
    k_prev = pl.BlockSpec((1, 1, WINDOW, HEAD_DIM), lambda i, j, n, s: (i, j, before(n), 0))
    v_own = pl.BlockSpec((1, 1, 1, HEAD_DIM, WINDOW), lambda i, j, n, s: (i, j, n, 0, 0))
    v_prev = pl.BlockSpec((1, 1, 1, HEAD_DIM, WINDOW), lambda i, j, n, s: (i, j, before(n), 0, 0))
    return pl.pallas_call(
        _swa_prompt_kernel,
        grid_spec=pltpu.PrefetchScalarGridSpec(
            num_scalar_prefetch=1,
            grid=(b, KV_HEADS, nb),
            in_specs=[q_spec, k_prev, k_own, v_prev, v_own],
            out_specs=pl.BlockSpec((1, WINDOW, GROUP * HEAD_DIM), lambda i, j, n, s: (i, n, j)),
        ),
        out_shape=jax.ShapeDtypeStruct((b, t, Q_DIM), BF16),
        compiler_params=_params("arbitrary", "arbitrary", "arbitrary"),
        name="swa_prompt",
    )(sinks, q_t, k_hm, k_hm, v_t, v_t)


def _fold_heads(acc):
    rows = acc.shape[0]
    lane_head = lax.broadcasted_iota(I32, (rows, KV_DIM), 1) // HEAD_DIM
    row_head = (lax.broadcasted_iota(I32, (rows, KV_DIM), 0) // GROUP) % KV_HEADS
    a = jnp.where(lane_head == row_head, acc, 0.0)
    a = a[:, :LANES] + a[:, LANES:]
    a = a + pltpu.roll(a, HEAD_DIM, 1)
    return a[:, :HEAD_DIM]


def _moba_sample_kernel(pt_ref, lhs_ref, kn_ref, vn_ref, kpool, vpool, o_ref,
                        buf, sem, s_scr, g_scr, stat_scr, l_scr, acc_scr, *, n_pages, chunk_pages):
    b = pl.program_id(0)
    c = pl.program_id(1)
    n_seq = pl.num_programs(0)
    n_chunks = n_pages // chunk_pages
    steps = 2 * n_chunks
    g = b * steps + c
    slot = g % 2
    rows = o_ref.shape[1]
    gate_rows = (lhs_ref.shape[1] - rows) // 2
    blk_pages = MOBA_BLOCK // PAGE_SIZE
    nb = n_pages // blk_pages

    def page_copy(pool, page, slot_, p):
        return pltpu.make_async_copy(pool.at[page], buf.at[slot_, p], sem.at[slot_])

    def start(step, seq, slot_):
        base = jnp.where(step >= n_chunks, step - n_chunks, step) * chunk_pages
        pages = [pt_ref[seq, base + p] for p in range(chunk_pages)]

        @pl.when(step < n_chunks)
        def _():
            for p in range(chunk_pages):
                page_copy(kpool, pages[p], slot_, p).start(priority=p % 2)

        @pl.when(step >= n_chunks)
        def _():
            for p in range(chunk_pages):
                page_copy(vpool, pages[p], slot_, p).start(priority=p % 2)

    @pl.when(g == 0)
    def _():
        start(c, b, slot)

    nxt = g + 1

    @pl.when(nxt < n_seq * steps)
    def _():
        start(nxt % steps, nxt // steps, nxt % 2)

    for p in range(chunk_pages):
        page_copy(kpool, 0, slot, p).wait()

    lhs = lhs_ref[0]
    qbd = lhs[:rows]

    @pl.when(c < n_chunks)
    def _():
        for p in range(chunk_pages):
            res = _dot(lhs, buf[slot, p].astype(BF16))
            s_scr[c * chunk_pages + p] = res[:rows]
            g_scr[c * chunk_pages + p] = res[rows:]

    @pl.when(c == n_chunks - 1)
    def _():
        lane = lax.broadcasted_iota(I32, (gate_rows, LANES), 1).astype(F32)
        gate = jnp.full((gate_rows, LANES), NEG_INF, F32)
        for n in range(nb):
            part = g_scr[n * blk_pages]
            for j in range(1, blk_pages):
                part = part + g_scr[n * blk_pages + j]
            col = jnp.sum(part[:gate_rows] + part[gate_rows:], axis=-1, keepdims=True) / MOBA_BLOCK
            gate = jnp.where(lane == n, col, gate)
        sel = jnp.zeros(gate.shape, F32)
        for _ in range(min(MOBA_TOPK, nb)):
            top = jnp.max(gate, axis=-1, keepdims=True)
            first = jnp.min(jnp.where(gate == top, lane, BIG_INDEX), axis=-1, keepdims=True)
            hit = lane == first
            sel = jnp.where(hit, 1.0, sel)
            gate = jnp.where(hit, NEG_INF, gate)
        sel_rows = jnp.concatenate(
            [jnp.broadcast_to(sel[r:r + 1], (GROUP, LANES)) for r in range(sel.shape[0])], axis=0)
        s_new = _dot_nt(qbd, kn_ref[0].astype(BF16))
        qrow = lax.broadcasted_iota(I32, s_new.shape, 0) // N_HEADS
        s_new = jnp.where(lax.broadcasted_iota(I32, s_new.shape, 1) <= qrow, s_new, NEG_INF)
        top = s_new
        for n in range(nb):
            keep = sel_rows[:, n:n + 1] > 0.0
            for j in range(blk_pages):
                s = jnp.where(keep, s_scr[n * blk_pages + j], NEG_INF)
                s_scr[n * blk_pages + j] = s
                top = jnp.maximum(top, s)
        m_run = jnp.max(top, axis=-1, keepdims=True)
        p_new = jnp.exp(s_new - m_run)
        stat_scr[:, 0:1] = m_run
        stat_scr[:, 1:2] = jnp.sum(p_new, axis=-1, keepdims=True)
        l_scr[...] = jnp.zeros(l_scr.shape, F32)
        acc_scr[...] = _dot(p_new.astype(BF16), vn_ref[0].astype(BF16))

    @pl.when(c >= n_chunks)
    def _():
        m_run = stat_scr[:, 0:1]
        l = l_scr[...]
        acc = acc_scr[...]
        for p in range(chunk_pages):
            prob = jnp.exp(s_scr[(c - n_chunks) * chunk_pages + p] - m_run)
            l = l + prob
            acc = acc + _dot_nt(prob.astype(BF16), buf[slot, p].astype(BF16))
        l_scr[...] = l
        acc_scr[...] = acc

    @pl.when(c == steps - 1)
    def _():
        den = jnp.sum(l_scr[...], axis=-1, keepdims=True) + stat_scr[:, 1:2]
        o_ref[0] = _fold_heads(acc_scr[...] / den)


def _moba_sample(page_table, lhs, k_new, v_new, kpool_t, vpool_t):
    db = lhs.shape[0]
    rows = lhs.shape[1] * N_HEADS // (N_HEADS + 2 * KV_HEADS)
    n_pages = page_table.shape[1]
    chunk_pages = n_pages // 2
    nb = n_pages * PAGE_SIZE // MOBA_BLOCK
    assert nb <= LANES and chunk_pages % (MOBA_BLOCK // PAGE_SIZE) == 0
    steps = 2 * (n_pages // chunk_pages)
    seq3 = lambda a: pl.BlockSpec((1,) + a.shape[1:], lambda i, c, pt: (i, 0, 0))
    kern = functools.partial(_moba_sample_kernel, n_pages=n_pages, chunk_pages=chunk_pages)
    return pl.pallas_call(
        kern,
        grid_spec=pltpu.PrefetchScalarGridSpec(
            num_scalar_prefetch=1,
            grid=(db, steps),
            in_specs=[seq3(lhs), seq3(k_new), seq3(v_new),
                      pl.BlockSpec(memory_space=pl.ANY), pl.BlockSpec(memory_space=pl.ANY)],
            out_specs=pl.BlockSpec((1, rows, HEAD_DIM), lambda i, c, pt: (i, 0, 0)),
            scratch_shapes=[
                pltpu.VMEM((2, chunk_pages, KV_DIM, PAGE_SIZE), F32),
                pltpu.SemaphoreType.DMA((2,)),
                pltpu.VMEM((n_pages, rows, PAGE_SIZE), F32),
                pltpu.VMEM((n_pages, lhs.shape[1] - rows, PAGE_SIZE), F32),
                pltpu.VMEM((rows, LANES), F32),
                pltpu.VMEM((rows, LANES), F32),
                pltpu.VMEM((rows, KV_DIM), F32),
            ],
        ),
        out_shape=jax.ShapeDtypeStruct((db, rows, HEAD_DIM), F32),
        compiler_params=_params("arbitrary", "arbitrary"),
        name="moba_sample",
    )(page_table, lhs, k_new, v_new, kpool_t, vpool_t)


def _swa_sample_kernel(qbd_ref, sink_ref, kb_ref, vb_ref, kn_ref, vn_ref, o_ref):
    rows = qbd_ref.shape[1]
    wb = kb_ref.shape[1]
    ds = kn_ref.shape[1]
    sink = sink_ref[...]
    qrow_b = lax.broadcasted_iota(I32, (rows, wb), 0) // N_HEADS
    buf_ok = lax.broadcasted_iota(I32, (rows, wb), 1) >= qrow_b
    qrow_n = lax.broadcasted_iota(I32, (rows, ds), 0) // N_HEADS
    new_ok = lax.broadcasted_iota(I32, (rows, ds), 1) <= qrow_n
    for s in range(qbd_ref.shape[0]):
        qbd = qbd_ref[s]
        sb = jnp.where(buf_ok, _dot_nt(qbd, kb_ref[s].astype(BF16)), NEG_INF)
        sn = jnp.where(new_ok, _dot_nt(qbd, kn_ref[s].astype(BF16)), NEG_INF)
        m = jnp.maximum(jnp.maximum(jnp.max(sb, axis=-1, keepdims=True),
                                    jnp.max(sn, axis=-1, keepdims=True)), sink)
        pb = jnp.exp(sb - m)
        pn = jnp.exp(sn - m)
        den = (jnp.sum(pb, axis=-1, keepdims=True) + jnp.sum(pn, axis=-1, keepdims=True)
               + jnp.exp(sink - m))
        acc = _dot(pb.astype(BF16), vb_ref[s].astype(BF16)) + _dot(pn.astype(BF16), vn_ref[s].astype(BF16))
        o_ref[s] = _fold_heads(acc / den)


def _swa_sample(qbd, sink_rows, buf_k, buf_v, k_new, v_new, seqs_per_step=8):
    db, rows, _ = qbd.shape
    wb = buf_k.shape[1]
    ds = k_new.shape[1]
    spec = lambda r: pl.BlockSpec((seqs_per_step, r, KV_DIM), lambda i: (i, 0, 0))
    return pl.pallas_call(
        _swa_sample_kernel,
        grid=(db // seqs_per_step,),
        in_specs=[spec(rows), pl.BlockSpec((rows, 1), lambda i: (0, 0)),
                  spec(wb), spec(wb), spec(ds), spec(ds)],
        out_specs=pl.BlockSpec((seqs_per_step, rows, HEAD_DIM), lambda i: (i, 0, 0)),
        out_shape=jax.ShapeDtypeStruct((db, rows, HEAD_DIM), F32),
        compiler_params=_params("arbitrary"),
        name="swa_sample",
    )(qbd, sink_rows, buf_k, buf_v, k_new, v_new)


def _attn_out_kernel(a_ref, x_ref, g_ref, sh_ref, sc_ref, wo_ref, bo_ref, lg_ref, lb_ref,
                     wrh_ref, wrl_ref, xn_ref, h_ref, lt_ref):
    o = _dot(a_ref[...], wo_ref[...]) + bo_ref[...]
    xn = _layer_norm(ALPHA * x_ref[...] + g_ref[0] * o, lg_ref[...], lb_ref[...])
    xn_ref[...] = xn
    h = xn * (1.0 + sc_ref[0]) + sh_ref[0]
    h_ref[...] = _pack_rows(h)
    h_hi, h_lo = _split_bf16(h)
    wrh = wrh_ref[...]
    lt_ref[...] = _dot_nt(wrh, h_hi) + _dot_nt(wrh, h_lo) + _dot_nt(wrl_ref[...], h_hi)


def _attn_out(a, x, gate, shift, scale, wo_bf, bo, ln_g, ln_b, wr_hi, wr_lo, tm):
    n = x.shape[0]
    per_mod = n // tm // gate.shape[0]
    mod_rows = gate.shape[1]
    row = pl.BlockSpec((tm, D_MODEL), lambda i: (i, 0))
    mod = pl.BlockSpec((1, mod_rows, D_MODEL), lambda i: (i // per_mod, 0, 0))
    vec = pl.BlockSpec((1, D_MODEL), lambda i: (0, 0))
    full = lambda r, c: pl.BlockSpec((r, c), lambda i: (0, 0))
    return pl.pallas_call(
        _attn_out_kernel,
        grid=(n // tm,),
        in_specs=[row, row, mod, mod, mod, full(Q_DIM, D_MODEL), vec, vec, vec,
                  full(N_EXPERTS, D_MODEL), full(N_EXPERTS, D_MODEL)],
        out_specs=[row, pl.BlockSpec((tm, PACK_WIDTH), lambda i: (i, 0)),
                   pl.BlockSpec((N_EXPERTS, tm), lambda i: (0, i))],
        out_shape=[jax.ShapeDtypeStruct((n, D_MODEL), F32),
                   jax.ShapeDtypeStruct((n, PACK_WIDTH), PACK_DTYPE),
                   jax.ShapeDtypeStruct((N_EXPERTS, n), F32)],
        compiler_params=_params("arbitrary"),
        name="attn_out_norm",
    )(a, x, gate, shift, scale, wo_bf, bo.reshape(1, -1), ln_g.reshape(1, -1), ln_b.reshape(1, -1),
      wr_hi, wr_lo)


def _first_max(x, idx):
    top = jnp.max(x, axis=0, keepdims=True)
    first = jnp.min(jnp.where(x == top, idx, BIG_INDEX), axis=0, keepdims=True)
    return top, first


def _route_kernel(lt_ref, b_ref, idx_ref, w_ref, pos_ref, cnt_ref):
    i = pl.program_id(0)
    tn = lt_ref.shape[1]
    scores = _sigmoid(lt_ref[...])
    biased = scores + b_ref[...]
    e_idx = lax.broadcasted_iota(I32, (N_EXPERTS, tn), 0).astype(F32)
    l_idx = lax.broadcasted_iota(I32, (PER_GROUP, tn), 0).astype(F32)
    g_score = []
    for g in range(N_EXPERT_GROUPS):
        x = biased[g * PER_GROUP:(g + 1) * PER_GROUP]
        top1, first = _first_max(x, l_idx)
        top2 = jnp.max(jnp.where(l_idx == first, NEG_INF, x), axis=0, keepdims=True)
        g_score.append(top1 + top2)
    pieces = []
    for g in range(N_EXPERT_GROUPS):
        ahead = jnp.zeros((1, tn), F32)
        for o in range(N_EXPERT_GROUPS):
            if o != g:
                wins = (g_score[o] >= g_score[g]) if o < g else (g_score[o] > g_score[g])
                ahead = ahead + jnp.where(wins, 1.0, 0.0)
        pieces.append(jnp.where(ahead < TOPK_GROUPS, biased[g * PER_GROUP:(g + 1) * PER_GROUP], NEG_INF))
    masked = jnp.concatenate(pieces, axis=0)
    chosen = jnp.zeros((N_EXPERTS, tn), F32)
    firsts, picked = [], []
    for _ in range(TOP_K):
        _, first = _first_max(masked, e_idx)
        hit = e_idx == first
        firsts.append(first)
        picked.append(jnp.sum(jnp.where(hit, scores, 0.0), axis=0, keepdims=True))
        chosen = jnp.where(hit, 1.0, chosen)
        masked = jnp.where(hit, NEG_INF, masked)
    total = picked[0]
    for s in picked[1:]:
        total = total + s
    for r in range(TOP_K):
        idx_ref[r:r + 1, :] = firsts[r].astype(I32)
        w_ref[r:r + 1, :] = picked[r] / total * ROUTED_SCALE

    @pl.when(i == 0)
    def _():
        cnt_ref[...] = jnp.zeros(cnt_ref.shape, F32)
    r = lax.broadcasted_iota(I32, (tn, tn), 0)
    c = lax.broadcasted_iota(I32, (tn, tn), 1)
    before = jnp.where(r < c, 1.0, 0.0).astype(BF16)
    prior = _dot(chosen.astype(BF16), before) + cnt_ref[...]
    for r in range(TOP_K):
        pos_ref[r:r + 1, :] = jnp.sum(jnp.where(e_idx == firsts[r], prior, 0.0), axis=0,
                                      keepdims=True).astype(I32)
    cnt_ref[...] = cnt_ref[...] + jnp.sum(chosen, axis=1, keepdims=True)


def _route(logits_t, b_router, tn=512):
    n = logits_t.shape[1]
    slot = pl.BlockSpec((TOP_K, tn), lambda i: (0, i))
    return pl.pallas_call(
        _route_kernel,
        grid=(n // tn,),
        in_specs=[pl.BlockSpec((N_EXPERTS, tn), lambda i: (0, i)),
                  pl.BlockSpec((N_EXPERTS, 1), lambda i: (0, 0))],
        out_specs=[slot, slot, slot, pl.BlockSpec((N_EXPERTS, 1), lambda i: (0, 0))],
        out_shape=[jax.ShapeDtypeStruct((TOP_K, n), I32),
                   jax.ShapeDtypeStruct((TOP_K, n), F32),
                   jax.ShapeDtypeStruct((TOP_K, n), I32),
                   jax.ShapeDtypeStruct((N_EXPERTS, 1), F32)],
        compiler_params=_params("arbitrary"),
        name="route",
    )(logits_t, b_router.reshape(N_EXPERTS, 1))


def _dest_kernel(idx_ref, pos_ref, start_ref, dest_ref):
    tn = idx_ref.shape[1]
    e_idx = lax.broadcasted_iota(I32, (N_EXPERTS, tn), 0)
    start = start_ref[...]
    for r in range(TOP_K):
        first = jnp.sum(jnp.where(e_idx == idx_ref[r:r + 1, :], start, 0.0), axis=0, keepdims=True)
        dest_ref[r:r + 1, :] = first.astype(I32) + pos_ref[r:r + 1, :]


def _dest_rows(idx_t, pos_t, pad_start, tn=512):
    n = idx_t.shape[1]
    slot = pl.BlockSpec((TOP_K, tn), lambda i: (0, i))
    return pl.pallas_call(
        _dest_kernel,
        grid=(n // tn,),
        in_specs=[slot, slot, pl.BlockSpec((N_EXPERTS, 1), lambda i: (0, 0))],
        out_specs=slot,
        out_shape=jax.ShapeDtypeStruct((TOP_K, n), I32),
        compiler_params=_params("arbitrary"),
        name="dest_rows",
    )(idx_t, pos_t, pad_start.astype(F32).reshape(N_EXPERTS, 1))


SC_CORES = 2
SC_SUBCORES = 16
SC_WINDOW = 48


def _scatter_rows(rows_tok, dest_flat, n_rows):
    n, width = rows_tok.shape
    workers = SC_CORES * SC_SUBCORES
    per_worker = n // workers
    assert n % workers == 0 and per_worker % SC_WINDOW == 0 and SC_WINDOW % 8 == 0
    mesh = plsc.VectorSubcoreMesh(core_axis_name="c", subcore_axis_name="s",
                                  num_cores=SC_CORES, num_subcores=SC_SUBCORES)

    def body(rows_hbm, dest_hbm, out_hbm, idx_v, rows_v):
        worker = lax.axis_index("s") * SC_CORES + lax.axis_index("c")
        base = worker * per_worker

        @pl.loop(0, per_worker // SC_WINDOW)
        def _(j):
            t0 = pl.multiple_of(base + j * SC_WINDOW, 8)
            pltpu.sync_copy(rows_hbm.at[pl.ds(t0, SC_WINDOW)], rows_v)
            for r in range(TOP_K):
                pltpu.sync_copy(dest_hbm.at[pl.ds(pl.multiple_of(r * n + t0, 8), SC_WINDOW)], idx_v)
                pltpu.sync_copy(rows_v, out_hbm.at[idx_v])

    return pl.kernel(
        body,
        out_type=jax.ShapeDtypeStruct((n_rows, width), rows_tok.dtype),
        mesh=mesh,
        scratch_types=[pltpu.VMEM((SC_WINDOW,), I32), pltpu.VMEM((SC_WINDOW, width), rows_tok.dtype)],
        name="scatter_rows",
    )(rows_tok, dest_flat)


def _moe_kernel(be_ref, nu_ref, x_ref, wg_ref, wu_ref, wd_ref, o_ref, wg_bf, wu_bf, wd_bf):
    i = pl.program_id(0)
    used = i < nu_ref[0]
    changed = (i == 0) | (be_ref[i] != be_ref[jnp.maximum(i - 1, 0)])

    @pl.when(used & changed)
    def _():
        wg_bf[...] = wg_ref[0, 0].astype(BF16)
        wu_bf[...] = wu_ref[0, 0].astype(BF16)
        wd_bf[...] = wd_ref[0, 0].astype(BF16)

    @pl.when(used)
    def _():
        x = jnp.concatenate(_unpack_rows(x_ref[...]), axis=1).astype(BF16)
        gate = _dot(x, wg_bf[...])
        up = _dot(x, wu_bf[...])
        act = gate * _sigmoid(gate) * up
        o_ref[...] = _pack_rows(_dot(act.astype(BF16), wd_bf[...]))

    @pl.when(jnp.logical_not(used))
    def _():
        o_ref[...] = jnp.zeros(o_ref.shape, o_ref.dtype)


def _moe_experts(blk_e, n_used, x_sorted, w_gate, w_up, w_down, layer):
    rows = x_sorted.shape[0]
    n_blocks = rows // MOE_ROWS
    x_map = lambda i, be, nu: (jnp.minimum(i, nu[0] - 1), 0)
    w_map = lambda i, be, nu: (layer, be[i], 0, 0)
    return pl.pallas_call(
        _moe_kernel,
        grid_spec=pltpu.PrefetchScalarGridSpec(
            num_scalar_prefetch=2,
            grid=(n_blocks,),
            in_specs=[pl.BlockSpec((MOE_ROWS, PACK_WIDTH), x_map),
                      pl.BlockSpec((1, 1, D_MODEL, D_EXPERT), w_map),
                      pl.BlockSpec((1, 1, D_MODEL, D_EXPERT), w_map),
                      pl.BlockSpec((1, 1, D_EXPERT, D_MODEL), w_map)],
            out_specs=pl.BlockSpec((MOE_ROWS, PACK_WIDTH), lambda i, be, nu: (i, 0)),
            scratch_shapes=[pltpu.VMEM((D_MODEL, D_EXPERT), BF16),
                            pltpu.VMEM((D_MODEL, D_EXPERT), BF16),
                            pltpu.VMEM((D_EXPERT, D_MODEL), BF16)],
        ),
        out_shape=jax.ShapeDtypeStruct((rows, PACK_WIDTH), PACK_DTYPE),
        compiler_params=_params("arbitrary"),
        name="moe_experts",
    )(blk_e, n_used, x_sorted, w_gate, w_up, w_down)


def _ffn_out_kernel(h_ref, y_ref, w_ref, x_ref, g_ref, wsg_ref, wsu_ref, wsd_ref, lg_ref, lb_ref, o_ref):
    h = jnp.concatenate(_unpack_rows(h_ref[...]), axis=1).astype(BF16)
    gate = _dot(h, wsg_ref[...])
    up = _dot(h, wsu_ref[...])
    shared = _dot((gate * _sigmoid(gate) * up).astype(BF16), wsd_ref[...])
    w = w_ref[...]
    first, second = None, None
    for r in range(TOP_K):
        ya, yb = _unpack_rows(y_ref[r])
        wr = w[:, r:r + 1]
        first = ya * wr if first is None else first + ya * wr
        second = yb * wr if second is None else second + yb * wr
    ffn = shared + jnp.concatenate([first, second], axis=1)
    y = ALPHA * x_ref[...] + g_ref[0] * ffn
    o_ref[...] = _layer_norm(y, lg_ref[...], lb_ref[...])


def _ffn_out(h, y_tok, w_tok, first_row, x, gate, wsg, wsu, wsd, ln_g, ln_b, tm):
    n = x.shape[0]
    per_mod = n // tm // gate.shape[0]
    off = first_row // tm
    row = pl.BlockSpec((tm, D_MODEL), lambda i: (i, 0))
    mod = pl.BlockSpec((1, gate.shape[1], D_MODEL), lambda i: (i // per_mod, 0, 0))
    vec = pl.BlockSpec((1, D_MODEL), lambda i: (0, 0))
    full = lambda a: pl.BlockSpec(a.shape, lambda i: (0, 0))
    return pl.pallas_call(
        _ffn_out_kernel,
        grid=(n // tm,),
        in_specs=[pl.BlockSpec((tm, PACK_WIDTH), lambda i: (i, 0)),
                  pl.BlockSpec((TOP_K, tm, PACK_WIDTH), lambda i: (0, i + off, 0)),
                  pl.BlockSpec((tm, TOP_K), lambda i: (i + off, 0)),
                  row, mod, full(wsg), full(wsu), full(wsd), vec, vec],
        out_specs=row,
        out_shape=jax.ShapeDtypeStruct((n, D_MODEL), F32),
        compiler_params=_params("arbitrary"),
        name="ffn_out_norm",
    )(h, y_tok, w_tok, x, gate, wsg, wsu, wsd, ln_g.reshape(1, -1), ln_b.reshape(1, -1))


def _dispatch(idx_t, pos_t, counts):
    n = idx_t.shape[1]
    counts = counts.reshape(N_EXPERTS).astype(I32)
    padded = (counts + MOE_ROWS - 1) // MOE_ROWS * MOE_ROWS
    pad_end = jnp.cumsum(padded)
    pad_start = pad_end - padded
    dest = _dest_rows(idx_t, pos_t, pad_start)
    n_blocks = (n * TOP_K + N_EXPERTS * (MOE_ROWS - 1)) // MOE_ROWS
    n_used = pad_end[-1] // MOE_ROWS
    blk = jnp.minimum(jnp.arange(n_blocks, dtype=I32), n_used - 1) * MOE_ROWS
    blk_e = jnp.minimum(jnp.sum((pad_end[None, :] <= blk[:, None]).astype(I32), axis=1), N_EXPERTS - 1)
    return dest, n_blocks * MOE_ROWS, blk_e, n_used.reshape(1).astype(I32)


def _block_diag_rows(x, heads_per_kv):
    db, ds, heads, _ = x.shape
    kv_of_head = jnp.arange(heads) // heads_per_kv
    onehot = (kv_of_head[:, None] == jnp.arange(KV_HEADS)[None, :]).astype(x.dtype)
    out = x[:, :, :, None, :] * onehot[None, None, :, :, None]
    return out.reshape(db, ds * heads, KV_DIM)


def kernel(x_prompt, x_sample, c_prompt, c_sample, cache_moba_k, cache_moba_v, state_swa_k, state_swa_v,
           page_table, w_ada, b_ada, w_qkv, b_qkv, attn_sinks, w_o, b_o, ln_attn_g, ln_attn_b,
           w_router, b_router, w_exp_gate, w_exp_up, w_exp_down, w_sh_gate, w_sh_up, w_sh_down,
           ln_ffn_g, ln_ffn_b):
    bsz, seq, d = x_prompt.shape
    db, ds, _ = x_sample.shape
    n_p, n_s = bsz * seq, db * ds
    past_len = page_table.shape[1] * PAGE_SIZE
    tm_p, tm_s, tm_norm = 512, 256, 256
    depth = w_ada.shape[0]

    mods = _ada_mod(jnp.concatenate([c_prompt, c_sample], 0), w_ada, b_ada)
    tabs_p = _rope_tables(jnp.arange(seq, dtype=I32))
    tabs_s = _rope_tables(jnp.tile(past_len + jnp.arange(ds, dtype=I32), db))

    xp = x_prompt.reshape(n_p, d)
    xs = x_sample.reshape(n_s, d)
    outs = {k: [] for k in ("mkp", "mvp", "mks", "mvs", "skp", "svp", "sks", "svs")}
    for i in range(depth):
        j = i // 2
        m = mods[i].reshape(bsz + db, N_MOD, d)
        mp = [m[:bsz, t][:, None, :] for t in range(N_MOD)]
        ms = [jnp.repeat(m[bsz:, t], ds, axis=0).reshape(n_s // tm_s, tm_s, d)
              for t in range(N_MOD)]
        w_qkv_bf = w_qkv[i].astype(BF16)
        moba = i % 2 == 0
        res_p = _qkv_project(xp, mp[0], mp[1], w_qkv_bf, b_qkv[i], tabs_p, tm_p, moba)
        res_s = _qkv_project(xs, ms[0], ms[1], w_qkv_bf, b_qkv[i], tabs_s, tm_s, moba)
        qp, kp, vp, kp_bf, vp_bf = res_p[:5]
        qs, ks, vs = res_s[:3]
        blk = MOBA_BLOCK if moba else WINDOW
        q_t = qp.reshape(bsz, seq, N_HEADS, HEAD_DIM).transpose(0, 2, 3, 1)
        k_hm = kp_bf.reshape(bsz, seq, KV_HEADS, HEAD_DIM).transpose(0, 2, 1, 3)
        v_t = vp_bf.reshape(bsz, seq // blk, blk, KV_HEADS, HEAD_DIM).transpose(0, 3, 1, 4, 2)
        qbd = _block_diag_rows(qs.reshape(db, ds, N_HEADS, HEAD_DIM), GROUP)
        pad_rows = ((0, 0), (0, LANES - ds), (0, 0))
        ks3 = jnp.pad(ks.reshape(db, ds, KV_DIM), pad_rows)
        vs3 = jnp.pad(vs.reshape(db, ds, KV_DIM), pad_rows)
        kp5 = kp.reshape(bsz, seq, KV_HEADS, HEAD_DIM)
        vp5 = vp.reshape(bsz, seq, KV_HEADS, HEAD_DIM)
        ks5 = ks.reshape(db, ds, KV_HEADS, HEAD_DIM)
        vs5 = vs.reshape(db, ds, KV_HEADS, HEAD_DIM)
        if moba:
            sel = _moba_select(res_p[5].reshape(bsz, seq, KV_DIM), kp.reshape(bsz, seq, KV_DIM))
            a_t = _moba_prompt(q_t, k_hm, v_t, sel)
            qs_hi, qs_lo = _split_bf16(_block_diag_rows(res_s[5].reshape(db, ds, KV_HEADS, HEAD_DIM), 1))
            n_pool = cache_moba_k.shape[1]
            pool_t = lambda pool: pool[j].transpose(0, 2, 3, 1).reshape(n_pool, KV_DIM, PAGE_SIZE)
            a_s = _moba_sample(page_table, jnp.concatenate([qbd, qs_hi, qs_lo], axis=1), ks3, vs3,
                               pool_t(cache_moba_k), pool_t(cache_moba_v))
            outs["mkp"].append(kp5)
            outs["mvp"].append(vp5)
            outs["mks"].append(ks5)
            outs["mvs"].append(vs5)
        else:
            a_t = _swa_prompt(q_t, k_hm, v_t, attn_sinks[j])
            wb = state_swa_k.shape[2]
            buf_k = state_swa_k[j].reshape(db, wb, KV_DIM)
            buf_v = state_swa_v[j].reshape(db, wb, KV_DIM)
            sink_rows = jnp.tile(attn_sinks[j], ds).reshape(ds * N_HEADS, 1)
            a_s = _swa_sample(qbd, sink_rows, buf_k, buf_v, ks3, vs3)
            wbp = min(WINDOW, seq)
            outs["skp"].append(kp5[:, -wbp:])
            outs["svp"].append(vp5[:, -wbp:])
            outs["sks"].append(jnp.concatenate([state_swa_k[j], ks5], axis=1)[:, -wb:])
            outs["svs"].append(jnp.concatenate([state_swa_v[j], vs5], axis=1)[:, -wb:])
        a_p = a_t.reshape(n_p, Q_DIM)
        a_s = a_s.reshape(n_s, Q_DIM).astype(BF16)

        wo_bf = w_o[i].astype(BF16)
        wr_hi, wr_lo = _split_bf16(w_router[i].T)
        args = (wo_bf, b_o[i], ln_attn_g[i], ln_attn_b[i], wr_hi, wr_lo)
        xp, hp, ltp = _attn_out(a_p, xp, mp[2], mp[3], mp[4], *args, tm_norm)
        xs, hs, lts = _attn_out(a_s, xs, ms[2], ms[3], ms[4], *args, tm_s)

        h_all = jnp.concatenate([hp, hs], axis=0)
        idx_t, w_t, pos_t, counts = _route(jnp.concatenate([ltp, lts], axis=1), b_router[i])
        dest, n_rows, blk_e, n_used = _dispatch(idx_t, pos_t, counts)
        x_sorted = _scatter_rows(lax.bitcast_convert_type(h_all, I32), dest.reshape(-1), n_rows)
        x_sorted = lax.bitcast_convert_type(x_sorted, PACK_DTYPE)
        y_sorted = _moe_experts(blk_e, n_used, x_sorted, w_exp_gate, w_exp_up, w_exp_down, i)
        y_tok = y_sorted[dest.reshape(-1)].reshape(TOP_K, n_p + n_s, -1)
        w_tok = w_t.T

        ws = (w_sh_gate[i].astype(BF16), w_sh_up[i].astype(BF16), w_sh_down[i].astype(BF16))
        norm = (ln_ffn_g[i], ln_ffn_b[i])
        xp = _ffn_out(hp, y_tok, w_tok, 0, xp, mp[5], *ws, *norm, tm_norm)
        xs = _ffn_out(hs, y_tok, w_tok, n_p, xs, ms[5], *ws, *norm, tm_s)

    st = lambda key: jnp.stack(outs[key])
    return (xp.reshape(bsz, seq, d), xs.reshape(db, ds, d), st("mkp"), st("mvp"), st("mks"), st("mvs"),
            st("skp"), st("svp"), st("sks"), st("svs"))
```

```python
import functools

import jax
import jax.numpy as jnp
from jax import lax
from jax.experimental import pallas as pl
from jax.experimental.pallas import tpu as pltpu
from jax.experimental.pallas import tpu_sc as plsc

F32 = jnp.float32
BF16 = jnp.bfloat16
I32 = jnp.int32

D_MODEL = 1024
N_HEADS = 16
HEAD_DIM = 64
KV_HEADS = 4
GROUP = N_HEADS // KV_HEADS
Q_DIM = N_HEADS * HEAD_DIM
KV_DIM = KV_HEADS * HEAD_DIM
QKV_DIM = Q_DIM + 2 * KV_DIM
ATTN_SCALE = HEAD_DIM ** -0.5
ROT_DIM = HEAD_DIM // 4
ROPE_THETA = 500000.0
PAGE_SIZE = 128
MOBA_BLOCK = 256
MOBA_TOPK = 3
WINDOW = 128
N_EXPERTS = 256
TOP_K = 8
N_EXPERT_GROUPS = 8
TOPK_GROUPS = 4
PER_GROUP = N_EXPERTS // N_EXPERT_GROUPS
D_EXPERT = 256
ROUTED_SCALE = 2.5
N_MOD = 6
DEPTH = 2
ALPHA = (2 * DEPTH) ** 0.25
LN_EPS = 1e-5

LANES = 128
VMEM_LIMIT = 48 * 1024 * 1024
MOE_ROWS = 512
NEG_INF = float("-inf")
BIG_INDEX = 1e9

_NT = (((1,), (1,)), ((), ()))


def _dot(a, b):
    return jnp.dot(a, b, preferred_element_type=F32)


def _dot_nt(a, b):
    return lax.dot_general(a, b, _NT, preferred_element_type=F32)


def _split_bf16(x):
    hi = x.astype(BF16)
    lo = (x - hi.astype(F32)).astype(BF16)
    return hi, lo


def _sigmoid(x):
    return 1.0 / (1.0 + jnp.exp(-x))


PACK_DTYPE = jnp.int32
PACK_WIDTH = D_MODEL // 2


def _pack_rows(x):
    half = x.shape[1] // 2
    hi = lax.bitcast_convert_type(x[:, :half].astype(BF16).astype(F32), jnp.uint32)
    lo = lax.bitcast_convert_type(x[:, half:].astype(BF16).astype(F32), jnp.uint32)
    return lax.bitcast_convert_type(hi | (lo >> 16), PACK_DTYPE)


def _unpack_rows(words):
    u = lax.bitcast_convert_type(words, jnp.uint32)
    hi = lax.bitcast_convert_type(u & jnp.uint32(0xFFFF0000), F32)
    lo = lax.bitcast_convert_type(u << 16, F32)
    return hi, lo


def _params(*sem):
    return pltpu.CompilerParams(dimension_semantics=sem, vmem_limit_bytes=VMEM_LIMIT)


def _layer_norm(y, g, b):
    mu = jnp.mean(y, axis=-1, keepdims=True)
    yc = y - mu
    var = jnp.mean(yc * yc, axis=-1, keepdims=True)
    return yc * lax.rsqrt(var + LN_EPS) * g + b


def _ada_kernel(c_ref, w_ref, b_ref, o_ref):
    c = c_ref[...]
    a_hi, a_lo = _split_bf16(c * _sigmoid(c))
    w_hi, w_lo = _split_bf16(w_ref[0])
    o_ref[0] = _dot(a_hi, w_hi) + _dot(a_lo, w_hi) + _dot(a_hi, w_lo) + b_ref[0]


def _ada_mod(c_all, w_ada, b_ada):
    rows = c_all.shape[0]
    depth = w_ada.shape[0]
    return pl.pallas_call(
        _ada_kernel,
        grid=(depth, N_MOD),
        in_specs=[
            pl.BlockSpec((rows, D_MODEL), lambda l, j: (0, 0)),
            pl.BlockSpec((1, D_MODEL, D_MODEL), lambda l, j: (l, 0, j)),
            pl.BlockSpec((1, 1, D_MODEL), lambda l, j: (l, 0, j)),
        ],
        out_specs=pl.BlockSpec((1, rows, D_MODEL), lambda l, j: (l, 0, j)),
        out_shape=jax.ShapeDtypeStruct((depth, rows, N_MOD * D_MODEL), F32),
        compiler_params=_params("arbitrary", "arbitrary"),
        name="ada_mod",
    )(c_all, w_ada, b_ada.reshape(depth, 1, N_MOD * D_MODEL))


def _qkv_kernel(x_ref, sh_ref, sc_ref, w_ref, b_ref, c_ref, s1_ref, s2_ref,
                q_ref, k_ref, v_ref, kb_ref, vb_ref, *qs_ref):
    u = x_ref[...] * (1.0 + sc_ref[0]) + sh_ref[0]
    qkv = _dot(u.astype(BF16), w_ref[...]) + b_ref[...]
    cos, s1, s2 = c_ref[...], s1_ref[...], s2_ref[...]
    rots = []
    for j in range((Q_DIM + KV_DIM) // LANES):
        blk = qkv[:, j * LANES:(j + 1) * LANES]
        rots.append(blk * cos + pltpu.roll(blk, LANES - ROT_DIM // 2, 1) * s1
                    + pltpu.roll(blk, ROT_DIM // 2, 1) * s2)
    nq = Q_DIM // LANES
    for j in range(nq):
        q_ref[:, j * LANES:(j + 1) * LANES] = (rots[j] * ATTN_SCALE).astype(BF16)
    for j in range(KV_DIM // LANES):
        k_ref[:, j * LANES:(j + 1) * LANES] = rots[nq + j]
        kb_ref[:, j * LANES:(j + 1) * LANES] = rots[nq + j].astype(BF16)
    v = qkv[:, Q_DIM + KV_DIM:]
    v_ref[...] = v
    vb_ref[...] = v.astype(BF16)
    if qs_ref:
        lane = lax.broadcasted_iota(I32, rots[0].shape, 1)
        halves = []
        for kv in range(KV_HEADS):
            t = rots[2 * kv] + rots[2 * kv + 1]
            halves.append(t + pltpu.roll(t, HEAD_DIM, 1))
        for j in range(KV_DIM // LANES):
            qs_ref[0][:, j * LANES:(j + 1) * LANES] = jnp.where(
                lane < HEAD_DIM, halves[2 * j], halves[2 * j + 1])


def _qkv_project(x, shift, scale, w_bf, b, tabs, tm, with_qsum):
    n = x.shape[0]
    n_tab = tabs[0].shape[0] // tm
    per_mod = n // tm // shift.shape[0]
    mod_rows = shift.shape[1]
    row_spec = lambda width: pl.BlockSpec((tm, width), lambda i: (i, 0))
    mod_spec = pl.BlockSpec((1, mod_rows, D_MODEL), lambda i: (i // per_mod, 0, 0))
    tab_spec = pl.BlockSpec((tm, LANES), lambda i: (i % n_tab, 0))
    out_shape = [
        jax.ShapeDtypeStruct((n, Q_DIM), BF16),
        jax.ShapeDtypeStruct((n, KV_DIM), F32),
        jax.ShapeDtypeStruct((n, KV_DIM), F32),
        jax.ShapeDtypeStruct((n, KV_DIM), BF16),
        jax.ShapeDtypeStruct((n, KV_DIM), BF16),
    ]
    out_specs = [row_spec(Q_DIM)] + [row_spec(KV_DIM)] * 4
    if with_qsum:
        out_shape.append(jax.ShapeDtypeStruct((n, KV_DIM), F32))
        out_specs.append(row_spec(KV_DIM))
    return pl.pallas_call(
        _qkv_kernel,
        grid=(n // tm,),
        in_specs=[
            row_spec(D_MODEL), mod_spec, mod_spec,
            pl.BlockSpec((D_MODEL, QKV_DIM), lambda i: (0, 0)),
            pl.BlockSpec((1, QKV_DIM), lambda i: (0, 0)),
            tab_spec, tab_spec, tab_spec,
        ],
        out_specs=out_specs,
        out_shape=out_shape,
        compiler_params=_params("arbitrary"),
        name="qkv_rope",
    )(x, shift, scale, w_bf, b.reshape(1, QKV_DIM), *tabs)


def _rope_tables(pos):
    half = ROT_DIM // 2
    inv_freq = 1.0 / (ROPE_THETA ** (jnp.arange(0, ROT_DIM, 2, dtype=F32) / ROT_DIM))
    ang = pos.astype(F32)[:, None] * inv_freq[None, :]
    cos, sin = jnp.cos(ang), jnp.sin(ang)
    rest = HEAD_DIM - ROT_DIM
    ones = jnp.ones((pos.shape[0], rest), F32)
    zeros = jnp.zeros((pos.shape[0], rest), F32)
    zh = jnp.zeros_like(sin)
    c = jnp.concatenate([cos, cos, ones], -1)
    s1 = jnp.concatenate([-sin, zh, zeros], -1)
    s2 = jnp.concatenate([zh, sin, zeros], -1)
    reps = LANES // HEAD_DIM
    return tuple(jnp.tile(t, (1, reps)) for t in (c, s1, s2))


def _moba_select_kernel(qs_ref, k_ref, sel_ref):
    t_len = k_ref.shape[1]
    nb = t_len // MOBA_BLOCK
    k = k_ref[0]
    nb_pad = 16
    means = jnp.concatenate(
        [jnp.sum(k[n * MOBA_BLOCK:(n + 1) * MOBA_BLOCK], axis=0, keepdims=True) / MOBA_BLOCK
         for n in range(nb)] + [jnp.zeros((nb_pad - nb, KV_DIM), F32)], axis=0)
    lane_head = lax.broadcasted_iota(I32, (nb_pad, KV_DIM), 1) // HEAD_DIM
    q_hi, q_lo = _split_bf16(qs_ref[0])
    pos_blk = lax.broadcasted_iota(I32, (nb, t_len), 1) // MOBA_BLOCK
    blk = lax.broadcasted_iota(I32, (nb, t_len), 0)
    past = blk < pos_blk
    for kv in range(KV_HEADS):
        m_hi, m_lo = _split_bf16(jnp.where(lane_head == kv, means, 0.0))
        gate = (_dot_nt(m_hi, q_hi) + _dot_nt(m_lo, q_hi) + _dot_nt(m_hi, q_lo))[:nb]
        gate = jnp.where(past, gate, NEG_INF)
        rank = jnp.zeros((nb, t_len), F32)
        for m in range(nb):
            other = gate[m:m + 1, :]
            ahead = (other > gate) | ((other == gate) & (blk > m))
            rank = rank + jnp.where(ahead, 1.0, 0.0)
        sel_ref[0, kv] = jnp.where(past & (rank < MOBA_TOPK), 1.0, 0.0)


def _moba_select(qsum, k):
    b, t, _ = k.shape
    nb = t // MOBA_BLOCK
    spec = pl.BlockSpec((1, t, KV_DIM), lambda i: (i, 0, 0))
    return pl.pallas_call(
        _moba_select_kernel,
        grid=(b,),
        in_specs=[spec, spec],
        out_specs=pl.BlockSpec((1, KV_HEADS, nb, t), lambda i: (i, 0, 0, 0)),
        out_shape=jax.ShapeDtypeStruct((b, KV_HEADS, nb, t), F32),
        compiler_params=_params("arbitrary"),
        name="moba_select",
    )(qsum, k)


def _moba_prompt_kernel(q_ref, k_ref, v_ref, sel_ref, o_ref):
    qi = pl.program_id(2)
    tq = MOBA_BLOCK
    cols = GROUP * tq
    q_t = jnp.concatenate([q_ref[0, g] for g in range(GROUP)], axis=1)

    def scores(n):
        kb = k_ref[0, 0, pl.ds(pl.multiple_of(n * tq, tq), tq), :]
        return _dot(kb, q_t)

    key = lax.broadcasted_iota(I32, (tq, cols), 0)
    qry = lax.broadcasted_iota(I32, (tq, cols), 1) & (tq - 1)
    s = jnp.where(key <= qry, scores(qi), NEG_INF)
    m0 = jnp.max(s, axis=0, keepdims=True)
    p = jnp.exp(s - m0)
    l0 = jnp.sum(p, axis=0, keepdims=True)
    acc0 = _dot(v_ref[0, 0, qi], p.astype(BF16))

    def body(n, carry):
        m, l, acc = carry
        chosen = sel_ref[0, 0, pl.ds(n, 1), :]
        keep = jnp.concatenate([chosen] * GROUP, axis=1) > 0.0
        s = jnp.where(keep, scores(n), NEG_INF)
        m_new = jnp.maximum(m, jnp.max(s, axis=0, keepdims=True))
        alpha = jnp.exp(m - m_new)
        p = jnp.exp(s - m_new)
        l = alpha * l + jnp.sum(p, axis=0, keepdims=True)
        acc = alpha * acc + _dot(v_ref[0, 0, n], p.astype(BF16))
        return m_new, l, acc

    _, l, acc = lax.fori_loop(0, qi, body, (m0, l0, acc0))
    _store_token_major(o_ref, acc / l, tq)


def _store_token_major(o_ref, out_t, tq):
    for pair in range(GROUP // 2):
        two = jnp.concatenate([out_t[:, (2 * pair) * tq:(2 * pair + 1) * tq],
                               out_t[:, (2 * pair + 1) * tq:(2 * pair + 2) * tq]], axis=0)
        o_ref[0, :, pair * 2 * HEAD_DIM:(pair + 1) * 2 * HEAD_DIM] = two.T.astype(BF16)


def _moba_prompt(q_t, k_hm, v_t, sel):
    b, _, _, t = q_t.shape
    nb = t // MOBA_BLOCK
    return pl.pallas_call(
        _moba_prompt_kernel,
        grid=(b, KV_HEADS, nb),
        in_specs=[pl.BlockSpec((1, GROUP, HEAD_DIM, MOBA_BLOCK), lambda i, j, n: (i, j, 0, n)),
                  pl.BlockSpec((1, 1, t, HEAD_DIM), lambda i, j, n: (i, j, 0, 0)),
                  pl.BlockSpec((1, 1, nb, HEAD_DIM, MOBA_BLOCK), lambda i, j, n: (i, j, 0, 0, 0)),
                  pl.BlockSpec((1, 1, nb, MOBA_BLOCK), lambda i, j, n: (i, j, 0, n))],
        out_specs=pl.BlockSpec((1, MOBA_BLOCK, GROUP * HEAD_DIM), lambda i, j, n: (i, n, j)),
        out_shape=jax.ShapeDtypeStruct((b, t, Q_DIM), BF16),
        compiler_params=_params("arbitrary", "arbitrary", "arbitrary"),
        name="moba_prompt",
    )(q_t, k_hm, v_t, sel)


def _swa_prompt_kernel(sink_ref, q_ref, kp_ref, k_ref, vp_ref, v_ref, o_ref):
    j = pl.program_id(1)
    n = pl.program_id(2)
    tq = WINDOW
    cols = GROUP * tq
    q_t = jnp.concatenate([q_ref[0, g] for g in range(GROUP)], axis=1)
    keys = jnp.concatenate([kp_ref[0, 0], k_ref[0, 0]], axis=0)
    vals_t = jnp.concatenate([vp_ref[0, 0, 0], v_ref[0, 0, 0]], axis=1)
    key = lax.broadcasted_iota(I32, (2 * tq, cols), 0)
    qry = lax.broadcasted_iota(I32, (2 * tq, cols), 1) & (tq - 1)
    visible = ((key < tq) & (key >= qry)) | ((key >= tq) & (key - tq <= qry))
    first_block = jnp.where(n > 0, 0.0, NEG_INF)
    s = _dot(keys, q_t)
    s = jnp.where(visible, s + jnp.where(key < tq, first_block, 0.0), NEG_INF)
    sink = jnp.concatenate([jnp.full((1, tq), sink_ref[j * GROUP + g], F32) for g in range(GROUP)], axis=1)
    m = jnp.maximum(jnp.max(s, axis=0, keepdims=True), sink)
    p = jnp.exp(s - m)
    den = jnp.sum(p, axis=0, keepdims=True) + jnp.exp(sink - m)
    _store_token_major(o_ref, _dot(vals_t, p.astype(BF16)) / den, tq)


def _swa_prompt(q_t, k_hm, v_t, sinks):
    b, _, _, t = q_t.shape
    nb = t // WINDOW
    before = lambda n: jnp.maximum(n - 1, 0)
    q_spec = pl.BlockSpec((1, GROUP, HEAD_DIM, WINDOW), lambda i, j, n, s: (i, j, 0, n))
    k_own = pl.BlockSpec((1, 1, WINDOW, HEAD_DIM), lambda i, j, n, s: (i, j, n, 0))
    k_prev = pl.BlockSpec((1, 1, WINDOW, HEAD_DIM), lambda i, j, n, s: (i, j, before(n), 0))
    v_own = pl.BlockSpec((1, 1, 1, HEAD_DIM, WINDOW), lambda i, j, n, s: (i, j, n, 0, 0))
    v_prev = pl.BlockSpec((1, 1, 1, HEAD_DIM, WINDOW), lambda i, j, n, s: (i, j, before(n), 0, 0))
    return pl.pallas_call(
        _swa_prompt_kernel,
        grid_spec=pltpu.PrefetchScalarGridSpec(
            num_scalar_prefetch=1,
            grid=(b, KV_HEADS, nb),
            in_specs=[q_spec, k_prev, k_own, v_prev, v_own],
            out_specs=pl.BlockSpec((1, WINDOW, GROUP * HEAD_DIM), lambda i, j, n, s: (i, n, j)),
        ),
        out_shape=jax.ShapeDtypeStruct((b, t, Q_DIM), BF16),
        compiler_params=_params("arbitrary", "arbitrary", "arbitrary"),
        name="swa_prompt",
    )(sinks, q_t, k_hm, k_hm, v_t, v_t)


def _fold_heads(acc):
    rows = acc.shape[0]
    lane_head = lax.broadcasted_iota(I32, (rows, KV_DIM), 1) // HEAD_DIM
    row_head = (lax.broadcasted_iota(I32, (rows, KV_DIM), 0) // GROUP) % KV_HEADS
    a = jnp.where(lane_head == row_head, acc, 0.0)
    a = a[:, :LANES] + a[:, LANES:]
    a = a + pltpu.roll(a, HEAD_DIM, 1)
    return a[:, :HEAD_DIM]


PREFETCH = 2
SLOTS = PREFETCH + 1


def _moba_sample_kernel(pt_ref, lhs_ref, kn_ref, vn_ref, kpool, vpool, o_ref,
                        buf, sem, s_scr, g_scr, stat_scr, l_scr, acc_scr, *, n_pages, chunk_pages):
    b = pl.program_id(0)
    c = pl.program_id(1)
    n_seq = pl.num_programs(0)
    n_chunks = n_pages // chunk_pages
    steps = 2 * n_chunks
    g = b * steps + c
    slot = g % SLOTS
    rows = o_ref.shape[1]
    gate_rows = (lhs_ref.shape[1] - rows) // 2
    blk_pages = MOBA_BLOCK // PAGE_SIZE
    nb = n_pages // blk_pages

    def page_copy(pool, page, slot_, p):
        return pltpu.make_async_copy(pool.at[page], buf.at[slot_, p], sem.at[slot_])

    def start(step, seq, slot_):
        base = jnp.where(step >= n_chunks, step - n_chunks, step) * chunk_pages
        pages = [pt_ref[seq, base + p] for p in range(chunk_pages)]

        @pl.when(step < n_chunks)
        def _():
            for p in range(chunk_pages):
                page_copy(kpool, pages[p], slot_, p).start(priority=p % 2)

        @pl.when(step >= n_chunks)
        def _():
            for p in range(chunk_pages):
                page_copy(vpool, pages[p], slot_, p).start(priority=p % 2)

    @pl.when(g == 0)
    def _():
        for ahead in range(PREFETCH):
            start(jnp.int32(ahead % steps), jnp.int32(ahead // steps), ahead % SLOTS)

    nxt = g + PREFETCH

    @pl.when(nxt < n_seq * steps)
    def _():
        start(nxt % steps, nxt // steps, nxt % SLOTS)

    for p in range(chunk_pages):
        page_copy(kpool, 0, slot, p).wait()

    lhs = lhs_ref[0]
    qbd = lhs[:rows]

    @pl.when(c < n_chunks)
    def _():
        for p in range(chunk_pages):
            res = _dot(lhs, buf[slot, p].astype(BF16))
            s_scr[c * chunk_pages + p] = res[:rows]
            g_scr[c * chunk_pages + p] = res[rows:]

    @pl.when(c == n_chunks - 1)
    def _():
        lane = lax.broadcasted_iota(I32, (gate_rows, LANES), 1).astype(F32)
        gate = jnp.full((gate_rows, LANES), NEG_INF, F32)
        for n in range(nb):
            part = g_scr[n * blk_pages]
            for j in range(1, blk_pages):
                part = part + g_scr[n * blk_pages + j]
            col = jnp.sum(part[:gate_rows] + part[gate_rows:], axis=-1, keepdims=True) / MOBA_BLOCK
            gate = jnp.where(lane == n, col, gate)
        sel = jnp.zeros(gate.shape, F32)
        for _ in range(min(MOBA_TOPK, nb)):
            top = jnp.max(gate, axis=-1, keepdims=True)
            first = jnp.min(jnp.where(gate == top, lane, BIG_INDEX), axis=-1, keepdims=True)
            hit = lane == first
            sel = jnp.where(hit, 1.0, sel)
            gate = jnp.where(hit, NEG_INF, gate)
        sel_rows = jnp.concatenate(
            [jnp.broadcast_to(sel[r:r + 1], (GROUP, LANES)) for r in range(sel.shape[0])], axis=0)
        s_new = _dot_nt(qbd, kn_ref[0].astype(BF16))
        qrow = lax.broadcasted_iota(I32, s_new.shape, 0) // N_HEADS
        s_new = jnp.where(lax.broadcasted_iota(I32, s_new.shape, 1) <= qrow, s_new, NEG_INF)
        top = s_new
        for n in range(nb):
            keep = sel_rows[:, n:n + 1] > 0.0
            for j in range(blk_pages):
                s = jnp.where(keep, s_scr[n * blk_pages + j], NEG_INF)
                s_scr[n * blk_pages + j] = s
                top = jnp.maximum(top, s)
        m_run = jnp.max(top, axis=-1, keepdims=True)
        p_new = jnp.exp(s_new - m_run)
        stat_scr[:, 0:1] = m_run
        stat_scr[:, 1:2] = jnp.sum(p_new, axis=-1, keepdims=True)
        l_scr[...] = jnp.zeros(l_scr.shape, F32)
        acc_scr[...] = _dot(p_new.astype(BF16), vn_ref[0].astype(BF16))

    @pl.when(c >= n_chunks)
    def _():
        m_run = stat_scr[:, 0:1]
        l = l_scr[...]
        acc = acc_scr[...]
        for p in range(chunk_pages):
            prob = jnp.exp(s_scr[(c - n_chunks) * chunk_pages + p] - m_run)
            l = l + prob
            acc = acc + _dot_nt(prob.astype(BF16), buf[slot, p].astype(BF16))
        l_scr[...] = l
        acc_scr[...] = acc

    @pl.when(c == steps - 1)
    def _():
        den = jnp.sum(l_scr[...], axis=-1, keepdims=True) + stat_scr[:, 1:2]
        o_ref[0] = _fold_heads(acc_scr[...] / den)


def _moba_sample(page_table, lhs, k_new, v_new, kpool_t, vpool_t):
    db = lhs.shape[0]
    rows = lhs.shape[1] * N_HEADS // (N_HEADS + 2 * KV_HEADS)
    n_pages = page_table.shape[1]
    chunk_pages = n_pages // 2
    nb = n_pages * PAGE_SIZE // MOBA_BLOCK
    assert nb <= LANES and chunk_pages % (MOBA_BLOCK // PAGE_SIZE) == 0
    steps = 2 * (n_pages // chunk_pages)
    seq3 = lambda a: pl.BlockSpec((1,) + a.shape[1:], lambda i, c, pt: (i, 0, 0))
    kern = functools.partial(_moba_sample_kernel, n_pages=n_pages, chunk_pages=chunk_pages)
    return pl.pallas_call(
        kern,
        grid_spec=pltpu.PrefetchScalarGridSpec(
            num_scalar_prefetch=1,
            grid=(db, steps),
            in_specs=[seq3(lhs), seq3(k_new), seq3(v_new),
                      pl.BlockSpec(memory_space=pl.ANY), pl.BlockSpec(memory_space=pl.ANY)],
            out_specs=pl.BlockSpec((1, rows, HEAD_DIM), lambda i, c, pt: (i, 0, 0)),
            scratch_shapes=[
                pltpu.VMEM((SLOTS, chunk_pages, KV_DIM, PAGE_SIZE), F32),
                pltpu.SemaphoreType.DMA((SLOTS,)),
                pltpu.VMEM((n_pages, rows, PAGE_SIZE), F32),
                pltpu.VMEM((n_pages, lhs.shape[1] - rows, PAGE_SIZE), F32),
                pltpu.VMEM((rows, LANES), F32),
                pltpu.VMEM((rows, LANES), F32),
                pltpu.VMEM((rows, KV_DIM), F32),
            ],
        ),
        out_shape=jax.ShapeDtypeStruct((db, rows, HEAD_DIM), F32),
        compiler_params=_params("arbitrary", "arbitrary"),
        name="moba_sample",
    )(page_table, lhs, k_new, v_new, kpool_t, vpool_t)


def _swa_sample_kernel(qbd_ref, sink_ref, kb_ref, vb_ref, kn_ref, vn_ref, o_ref):
    rows = qbd_ref.shape[1]
    wb = kb_ref.shape[1]
    ds = kn_ref.shape[1]
    sink = sink_ref[...]
    qrow_b = lax.broadcasted_iota(I32, (rows, wb), 0) // N_HEADS
    buf_ok = lax.broadcasted_iota(I32, (rows, wb), 1) >= qrow_b
    qrow_n = lax.broadcasted_iota(I32, (rows, ds), 0) // N_HEADS
    new_ok = lax.broadcasted_iota(I32, (rows, ds), 1) <= qrow_n
    for s in range(qbd_ref.shape[0]):
        qbd = qbd_ref[s]
        sb = jnp.where(buf_ok, _dot_nt(qbd, kb_ref[s].astype(BF16)), NEG_INF)
        sn = jnp.where(new_ok, _dot_nt(qbd, kn_ref[s].astype(BF16)), NEG_INF)
        m = jnp.maximum(jnp.maximum(jnp.max(sb, axis=-1, keepdims=True),
                                    jnp.max(sn, axis=-1, keepdims=True)), sink)
        pb = jnp.exp(sb - m)
        pn = jnp.exp(sn - m)
        den = (jnp.sum(pb, axis=-1, keepdims=True) + jnp.sum(pn, axis=-1, keepdims=True)
               + jnp.exp(sink - m))
        acc = _dot(pb.astype(BF16), vb_ref[s].astype(BF16)) + _dot(pn.astype(BF16), vn_ref[s].astype(BF16))
        o_ref[s] = _fold_heads(acc / den)


def _swa_sample(qbd, sink_rows, buf_k, buf_v, k_new, v_new, seqs_per_step=8):
    db, rows, _ = qbd.shape
    wb = buf_k.shape[1]
    ds = k_new.shape[1]
    spec = lambda r: pl.BlockSpec((seqs_per_step, r, KV_DIM), lambda i: (i, 0, 0))
    return pl.pallas_call(
        _swa_sample_kernel,
        grid=(db // seqs_per_step,),
        in_specs=[spec(rows), pl.BlockSpec((rows, 1), lambda i: (0, 0)),
                  spec(wb), spec(wb), spec(ds), spec(ds)],
        out_specs=pl.BlockSpec((seqs_per_step, rows, HEAD_DIM), lambda i: (i, 0, 0)),
        out_shape=jax.ShapeDtypeStruct((db, rows, HEAD_DIM), F32),
        compiler_params=_params("arbitrary"),
        name="swa_sample",
    )(qbd, sink_rows, buf_k, buf_v, k_new, v_new)


def _attn_out_kernel(a_ref, x_ref, g_ref, sh_ref, sc_ref, wo_ref, bo_ref, lg_ref, lb_ref,
                     wrh_ref, wrl_ref, xn_ref, h_ref, lt_ref):
    o = _dot(a_ref[...], wo_ref[...]) + bo_ref[...]
    xn = _layer_norm(ALPHA * x_ref[...] + g_ref[0] * o, lg_ref[...], lb_ref[...])
    xn_ref[...] = xn
    h = xn * (1.0 + sc_ref[0]) + sh_ref[0]
    h_ref[...] = _pack_rows(h)
    h_hi, h_lo = _split_bf16(h)
    wrh = wrh_ref[...]
    lt_ref[...] = _dot_nt(wrh, h_hi) + _dot_nt(wrh, h_lo) + _dot_nt(wrl_ref[...], h_hi)


def _attn_out(a, x, gate, shift, scale, wo_bf, bo, ln_g, ln_b, wr_hi, wr_lo, tm):
    n = x.shape[0]
    per_mod = n // tm // gate.shape[0]
    mod_rows = gate.shape[1]
    row = pl.BlockSpec((tm, D_MODEL), lambda i: (i, 0))
    mod = pl.BlockSpec((1, mod_rows, D_MODEL), lambda i: (i // per_mod, 0, 0))
    vec = pl.BlockSpec((1, D_MODEL), lambda i: (0, 0))
    full = lambda r, c: pl.BlockSpec((r, c), lambda i: (0, 0))
    return pl.pallas_call(
        _attn_out_kernel,
        grid=(n // tm,),
        in_specs=[row, row, mod, mod, mod, full(Q_DIM, D_MODEL), vec, vec, vec,
                  full(N_EXPERTS, D_MODEL), full(N_EXPERTS, D_MODEL)],
        out_specs=[row, pl.BlockSpec((tm, PACK_WIDTH), lambda i: (i, 0)),
                   pl.BlockSpec((N_EXPERTS, tm), lambda i: (0, i))],
        out_shape=[jax.ShapeDtypeStruct((n, D_MODEL), F32),
                   jax.ShapeDtypeStruct((n, PACK_WIDTH), PACK_DTYPE),
                   jax.ShapeDtypeStruct((N_EXPERTS, n), F32)],
        compiler_params=_params("arbitrary"),
        name="attn_out_norm",
    )(a, x, gate, shift, scale, wo_bf, bo.reshape(1, -1), ln_g.reshape(1, -1), ln_b.reshape(1, -1),
      wr_hi, wr_lo)


def _first_max(x, idx):
    top = jnp.max(x, axis=0, keepdims=True)
    first = jnp.min(jnp.where(x == top, idx, BIG_INDEX), axis=0, keepdims=True)
    return top, first


def _route_kernel(lt_ref, b_ref, idx_ref, w_ref, pos_ref, cnt_ref):
    i = pl.program_id(0)
    tn = lt_ref.shape[1]
    scores = _sigmoid(lt_ref[...])
    biased = scores + b_ref[...]
    e_idx = lax.broadcasted_iota(I32, (N_EXPERTS, tn), 0).astype(F32)
    l_idx = lax.broadcasted_iota(I32, (PER_GROUP, tn), 0).astype(F32)
    g_score = []
    for g in range(N_EXPERT_GROUPS):
        x = biased[g * PER_GROUP:(g + 1) * PER_GROUP]
        top1, first = _first_max(x, l_idx)
        top2 = jnp.max(jnp.where(l_idx == first, NEG_INF, x), axis=0, keepdims=True)
        g_score.append(top1 + top2)
    pieces = []
    for g in range(N_EXPERT_GROUPS):
        ahead = jnp.zeros((1, tn), F32)
        for o in range(N_EXPERT_GROUPS):
            if o != g:
                wins = (g_score[o] >= g_score[g]) if o < g else (g_score[o] > g_score[g])
                ahead = ahead + jnp.where(wins, 1.0, 0.0)
        pieces.append(jnp.where(ahead < TOPK_GROUPS, biased[g * PER_GROUP:(g + 1) * PER_GROUP], NEG_INF))
    masked = jnp.concatenate(pieces, axis=0)
    chosen = jnp.zeros((N_EXPERTS, tn), F32)
    firsts, picked = [], []
    for _ in range(TOP_K):
        _, first = _first_max(masked, e_idx)
        hit = e_idx == first
        firsts.append(first)
        picked.append(jnp.sum(jnp.where(hit, scores, 0.0), axis=0, keepdims=True))
        chosen = jnp.where(hit, 1.0, chosen)
        masked = jnp.where(hit, NEG_INF, masked)
    total = picked[0]
    for s in picked[1:]:
        total = total + s
    for r in range(TOP_K):
        idx_ref[r:r + 1, :] = firsts[r].astype(I32)
        w_ref[r:r + 1, :] = picked[r] / total * ROUTED_SCALE

    @pl.when(i == 0)
    def _():
        cnt_ref[...] = jnp.zeros(cnt_ref.shape, F32)
    r = lax.broadcasted_iota(I32, (tn, tn), 0)
    c = lax.broadcasted_iota(I32, (tn, tn), 1)
    before = jnp.where(r < c, 1.0, 0.0).astype(BF16)
    prior = _dot(chosen.astype(BF16), before) + cnt_ref[...]
    for r in range(TOP_K):
        pos_ref[r:r + 1, :] = jnp.sum(jnp.where(e_idx == firsts[r], prior, 0.0), axis=0,
                                      keepdims=True).astype(I32)
    cnt_ref[...] = cnt_ref[...] + jnp.sum(chosen, axis=1, keepdims=True)


def _route(logits_t, b_router, tn=512):
    n = logits_t.shape[1]
    slot = pl.BlockSpec((TOP_K, tn), lambda i: (0, i))
    return pl.pallas_call(
        _route_kernel,
        grid=(n // tn,),
        in_specs=[pl.BlockSpec((N_EXPERTS, tn), lambda i: (0, i)),
                  pl.BlockSpec((N_EXPERTS, 1), lambda i: (0, 0))],
        out_specs=[slot, slot, slot, pl.BlockSpec((N_EXPERTS, 1), lambda i: (0, 0))],
        out_shape=[jax.ShapeDtypeStruct((TOP_K, n), I32),
                   jax.ShapeDtypeStruct((TOP_K, n), F32),
                   jax.ShapeDtypeStruct((TOP_K, n), I32),
                   jax.ShapeDtypeStruct((N_EXPERTS, 1), F32)],
        compiler_params=_params("arbitrary"),
        name="route",
    )(logits_t, b_router.reshape(N_EXPERTS, 1))


def _dest_kernel(idx_ref, pos_ref, start_ref, dest_ref):
    tn = idx_ref.shape[1]
    e_idx = lax.broadcasted_iota(I32, (N_EXPERTS, tn), 0)
    start = start_ref[...]
    for r in range(TOP_K):
        first = jnp.sum(jnp.where(e_idx == idx_ref[r:r + 1, :], start, 0.0), axis=0, keepdims=True)
        dest_ref[r:r + 1, :] = first.astype(I32) + pos_ref[r:r + 1, :]


def _dest_rows(idx_t, pos_t, pad_start, tn=512):
    n = idx_t.shape[1]
    slot = pl.BlockSpec((TOP_K, tn), lambda i: (0, i))
    return pl.pallas_call(
        _dest_kernel,
        grid=(n // tn,),
        in_specs=[slot, slot, pl.BlockSpec((N_EXPERTS, 1), lambda i: (0, 0))],
        out_specs=slot,
        out_shape=jax.ShapeDtypeStruct((TOP_K, n), I32),
        compiler_params=_params("arbitrary"),
        name="dest_rows",
    )(idx_t, pos_t, pad_start.astype(F32).reshape(N_EXPERTS, 1))


SC_CORES = 2
SC_SUBCORES = 16
SC_WINDOW = 48


def _scatter_rows(rows_tok, dest_flat, n_rows):
    n, width = rows_tok.shape
    workers = SC_CORES * SC_SUBCORES
    per_worker = n // workers
    assert n % workers == 0 and per_worker % SC_WINDOW == 0 and SC_WINDOW % 8 == 0
    mesh = plsc.VectorSubcoreMesh(core_axis_name="c", subcore_axis_name="s",
                                  num_cores=SC_CORES, num_subcores=SC_SUBCORES)

    def body(rows_hbm, dest_hbm, out_hbm, idx_v, rows_v):
        worker = lax.axis_index("s") * SC_CORES + lax.axis_index("c")
        base = worker * per_worker

        @pl.loop(0, per_worker // SC_WINDOW)
        def _(j):
            t0 = pl.multiple_of(base + j * SC_WINDOW, 8)
            pltpu.sync_copy(rows_hbm.at[pl.ds(t0, SC_WINDOW)], rows_v)
            for r in range(TOP_K):
                pltpu.sync_copy(dest_hbm.at[pl.ds(pl.multiple_of(r * n + t0, 8), SC_WINDOW)], idx_v)
                pltpu.sync_copy(rows_v, out_hbm.at[idx_v])

    return pl.kernel(
        body,
        out_type=jax.ShapeDtypeStruct((n_rows, width), rows_tok.dtype),
        mesh=mesh,
        scratch_types=[pltpu.VMEM((SC_WINDOW,), I32), pltpu.VMEM((SC_WINDOW, width), rows_tok.dtype)],
        name="scatter_rows",
    )(rows_tok, dest_flat)


def _moe_kernel(be_ref, nu_ref, x_ref, wg_ref, wu_ref, wd_ref, o_ref, wg_bf, wu_bf, wd_bf):
    i = pl.program_id(0)
    used = i < nu_ref[0]
    changed = (i == 0) | (be_ref[i] != be_ref[jnp.maximum(i - 1, 0)])

    @pl.when(used & changed)
    def _():
        wg_bf[...] = wg_ref[0, 0].astype(BF16)
        wu_bf[...] = wu_ref[0, 0].astype(BF16)
        wd_bf[...] = wd_ref[0, 0].astype(BF16)

    @pl.when(used)
    def _():
        x = jnp.concatenate(_unpack_rows(x_ref[...]), axis=1).astype(BF16)
        gate = _dot(x, wg_bf[...])
        up = _dot(x, wu_bf[...])
        act = gate * _sigmoid(gate) * up
        o_ref[...] = _pack_rows(_dot(act.astype(BF16), wd_bf[...]))


def _moe_experts(blk_e, n_used, x_sorted, w_gate, w_up, w_down, layer):
    rows = x_sorted.shape[0]
    n_blocks = rows // MOE_ROWS
    x_map = lambda i, be, nu: (jnp.minimum(i, nu[0] - 1), 0)
    w_map = lambda i, be, nu: (layer, be[i], 0, 0)
    return pl.pallas_call(
        _moe_kernel,
        grid_spec=pltpu.PrefetchScalarGridSpec(
            num_scalar_prefetch=2,
            grid=(n_blocks,),
            in_specs=[pl.BlockSpec((MOE_ROWS, PACK_WIDTH), x_map),
                      pl.BlockSpec((1, 1, D_MODEL, D_EXPERT), w_map),
                      pl.BlockSpec((1, 1, D_MODEL, D_EXPERT), w_map),
                      pl.BlockSpec((1, 1, D_EXPERT, D_MODEL), w_map)],
            out_specs=pl.BlockSpec((MOE_ROWS, PACK_WIDTH), x_map),
            scratch_shapes=[pltpu.VMEM((D_MODEL, D_EXPERT), BF16),
                            pltpu.VMEM((D_MODEL, D_EXPERT), BF16),
                            pltpu.VMEM((D_EXPERT, D_MODEL), BF16)],
        ),
        out_shape=jax.ShapeDtypeStruct((rows, PACK_WIDTH), PACK_DTYPE),
        compiler_params=_params("arbitrary"),
        name="moe_experts",
    )(blk_e, n_used, x_sorted, w_gate, w_up, w_down)


def _ffn_out_kernel(h_ref, y_ref, w_ref, x_ref, g_ref, wsg_ref, wsu_ref, wsd_ref, lg_ref, lb_ref, o_ref):
    h = jnp.concatenate(_unpack_rows(h_ref[...]), axis=1).astype(BF16)
    gate = _dot(h, wsg_ref[...])
    up = _dot(h, wsu_ref[...])
    shared = _dot((gate * _sigmoid(gate) * up).astype(BF16), wsd_ref[...])
    w = w_ref[...]
    first, second = None, None
    for r in range(TOP_K):
        ya, yb = _unpack_rows(y_ref[r])
        wr = w[:, r:r + 1]
        first = ya * wr if first is None else first + ya * wr
        second = yb * wr if second is None else second + yb * wr
    ffn = shared + jnp.concatenate([first, second], axis=1)
    y = ALPHA * x_ref[...] + g_ref[0] * ffn
    o_ref[...] = _layer_norm(y, lg_ref[...], lb_ref[...])


def _ffn_out(h, y_tok, w_tok, first_row, x, gate, wsg, wsu, wsd, ln_g, ln_b, tm):
    n = x.shape[0]
    per_mod = n // tm // gate.shape[0]
    off = first_row // tm
    row = pl.BlockSpec((tm, D_MODEL), lambda i: (i, 0))
    mod = pl.BlockSpec((1, gate.shape[1], D_MODEL), lambda i: (i // per_mod, 0, 0))
    vec = pl.BlockSpec((1, D_MODEL), lambda i: (0, 0))
    full = lambda a: pl.BlockSpec(a.shape, lambda i: (0, 0))
    return pl.pallas_call(
        _ffn_out_kernel,
        grid=(n // tm,),
        in_specs=[pl.BlockSpec((tm, PACK_WIDTH), lambda i: (i, 0)),
                  pl.BlockSpec((TOP_K, tm, PACK_WIDTH), lambda i: (0, i + off, 0)),
                  pl.BlockSpec((tm, TOP_K), lambda i: (i + off, 0)),
                  row, mod, full(wsg), full(wsu), full(wsd), vec, vec],
        out_specs=row,
        out_shape=jax.ShapeDtypeStruct((n, D_MODEL), F32),
        compiler_params=_params("arbitrary"),
        name="ffn_out_norm",
    )(h, y_tok, w_tok, x, gate, wsg, wsu, wsd, ln_g.reshape(1, -1), ln_b.reshape(1, -1))


def _dispatch(idx_t, pos_t, counts):
    n = idx_t.shape[1]
    counts = counts.reshape(N_EXPERTS).astype(I32)
    padded = (counts + MOE_ROWS - 1) // MOE_ROWS * MOE_ROWS
    pad_end = jnp.cumsum(padded)
    pad_start = pad_end - padded
    dest = _dest_rows(idx_t, pos_t, pad_start)
    n_blocks = (n * TOP_K + N_EXPERTS * (MOE_ROWS - 1)) // MOE_ROWS
    n_used = pad_end[-1] // MOE_ROWS
    blk = jnp.minimum(jnp.arange(n_blocks, dtype=I32), n_used - 1) * MOE_ROWS
    blk_e = jnp.minimum(jnp.sum((pad_end[None, :] <= blk[:, None]).astype(I32), axis=1), N_EXPERTS - 1)
    return dest, n_blocks * MOE_ROWS, blk_e, n_used.reshape(1).astype(I32)


def _block_diag_rows(x, heads_per_kv):
    db, ds, heads, _ = x.shape
    kv_of_head = jnp.arange(heads) // heads_per_kv
    onehot = (kv_of_head[:, None] == jnp.arange(KV_HEADS)[None, :]).astype(x.dtype)
    out = x[:, :, :, None, :] * onehot[None, None, :, :, None]
    return out.reshape(db, ds * heads, KV_DIM)


def kernel(x_prompt, x_sample, c_prompt, c_sample, cache_moba_k, cache_moba_v, state_swa_k, state_swa_v,
           page_table, w_ada, b_ada, w_qkv, b_qkv, attn_sinks, w_o, b_o, ln_attn_g, ln_attn_b,
           w_router, b_router, w_exp_gate, w_exp_up, w_exp_down, w_sh_gate, w_sh_up, w_sh_down,
           ln_ffn_g, ln_ffn_b):
    bsz, seq, d = x_prompt.shape
    db, ds, _ = x_sample.shape
    n_p, n_s = bsz * seq, db * ds
    past_len = page_table.shape[1] * PAGE_SIZE
    tm_p, tm_s, tm_norm = 512, 256, 256
    depth = w_ada.shape[0]

    mods = _ada_mod(jnp.concatenate([c_prompt, c_sample], 0), w_ada, b_ada)
    tabs_p = _rope_tables(jnp.arange(seq, dtype=I32))
    tabs_s = _rope_tables(jnp.tile(past_len + jnp.arange(ds, dtype=I32), db))

    xp = x_prompt.reshape(n_p, d)
    xs = x_sample.reshape(n_s, d)
    outs = {k: [] for k in ("mkp", "mvp", "mks", "mvs", "skp", "svp", "sks", "svs")}
    for i in range(depth):
        j = i // 2
        m = mods[i].reshape(bsz + db, N_MOD, d)
        mp = [m[:bsz, t][:, None, :] for t in range(N_MOD)]
        ms = [jnp.repeat(m[bsz:, t], ds, axis=0).reshape(n_s // tm_s, tm_s, d)
              for t in range(N_MOD)]
        w_qkv_bf = w_qkv[i].astype(BF16)
        moba = i % 2 == 0
        res_p = _qkv_project(xp, mp[0], mp[1], w_qkv_bf, b_qkv[i], tabs_p, tm_p, moba)
        res_s = _qkv_project(xs, ms[0], ms[1], w_qkv_bf, b_qkv[i], tabs_s, tm_s, moba)
        qp, kp, vp, kp_bf, vp_bf = res_p[:5]
        qs, ks, vs = res_s[:3]
        blk = MOBA_BLOCK if moba else WINDOW
        q_t = qp.reshape(bsz, seq, N_HEADS, HEAD_DIM).transpose(0, 2, 3, 1)
        k_hm = kp_bf.reshape(bsz, seq, KV_HEADS, HEAD_DIM).transpose(0, 2, 1, 3)
        v_t = vp_bf.reshape(bsz, seq // blk, blk, KV_HEADS, HEAD_DIM).transpose(0, 3, 1, 4, 2)
        qbd = _block_diag_rows(qs.reshape(db, ds, N_HEADS, HEAD_DIM), GROUP)
        pad_rows = ((0, 0), (0, LANES - ds), (0, 0))
        ks3 = jnp.pad(ks.reshape(db, ds, KV_DIM), pad_rows)
        vs3 = jnp.pad(vs.reshape(db, ds, KV_DIM), pad_rows)
        kp5 = kp.reshape(bsz, seq, KV_HEADS, HEAD_DIM)
        vp5 = vp.reshape(bsz, seq, KV_HEADS, HEAD_DIM)
        ks5 = ks.reshape(db, ds, KV_HEADS, HEAD_DIM)
        vs5 = vs.reshape(db, ds, KV_HEADS, HEAD_DIM)
        if moba:
            sel = _moba_select(res_p[5].reshape(bsz, seq, KV_DIM), kp.reshape(bsz, seq, KV_DIM))
            a_t = _moba_prompt(q_t, k_hm, v_t, sel)
            qs_hi, qs_lo = _split_bf16(_block_diag_rows(res_s[5].reshape(db, ds, KV_HEADS, HEAD_DIM), 1))
            n_pool = cache_moba_k.shape[1]
            pool_t = lambda pool: pool[j].transpose(0, 2, 3, 1).reshape(n_pool, KV_DIM, PAGE_SIZE)
            a_s = _moba_sample(page_table, jnp.concatenate([qbd, qs_hi, qs_lo], axis=1), ks3, vs3,
                               pool_t(cache_moba_k), pool_t(cache_moba_v))
            outs["mkp"].append(kp5)
            outs["mvp"].append(vp5)
            outs["mks"].append(ks5)
            outs["mvs"].append(vs5)
        else:
            a_t = _swa_prompt(q_t, k_hm, v_t, attn_sinks[j])
            wb = state_swa_k.shape[2]
            buf_k = state_swa_k[j].reshape(db, wb, KV_DIM)
            buf_v = state_swa_v[j].reshape(db, wb, KV_DIM)
            sink_rows = jnp.tile(attn_sinks[j], ds).reshape(ds * N_HEADS, 1)
            a_s = _swa_sample(qbd, sink_rows, buf_k, buf_v, ks3, vs3)
            wbp = min(WINDOW, seq)
            outs["skp"].append(kp5[:, -wbp:])
            outs["svp"].append(vp5[:, -wbp:])
            outs["sks"].append(jnp.concatenate([state_swa_k[j], ks5], axis=1)[:, -wb:])
            outs["svs"].append(jnp.concatenate([state_swa_v[j], vs5], axis=1)[:, -wb:])
        a_p = a_t.reshape(n_p, Q_DIM)
        a_s = a_s.reshape(n_s, Q_DIM).astype(BF16)

        wo_bf = w_o[i].astype(BF16)
        wr_hi, wr_lo = _split_bf16(w_router[i].T)
        args = (wo_bf, b_o[i], ln_attn_g[i], ln_attn_b[i], wr_hi, wr_lo)
        xp, hp, ltp = _attn_out(a_p, xp, mp[2], mp[3], mp[4], *args, tm_norm)
        xs, hs, lts = _attn_out(a_s, xs, ms[2], ms[3], ms[4], *args, tm_s)

        h_all = jnp.concatenate([hp, hs], axis=0)
        idx_t, w_t, pos_t, counts = _route(jnp.concatenate([ltp, lts], axis=1), b_router[i])
        dest, n_rows, blk_e, n_used = _dispatch(idx_t, pos_t, counts)
        x_sorted = _scatter_rows(h_all, dest.reshape(-1), n_rows)
        y_sorted = _moe_experts(blk_e, n_used, x_sorted, w_exp_gate, w_exp_up, w_exp_down, i)
        y_tok = y_sorted[dest.reshape(-1)].reshape(TOP_K, n_p + n_s, -1)
        w_tok = w_t.T

        ws = (w_sh_gate[i].astype(BF16), w_sh_up[i].astype(BF16), w_sh_down[i].astype(BF16))
        norm = (ln_ffn_g[i], ln_ffn_b[i])
        xp = _ffn_out(hp, y_tok, w_tok, 0, xp, mp[5], *ws, *norm, tm_norm)
        xs = _ffn_out(hs, y_tok, w_tok, n_p, xs, ms[5], *ws, *norm, tm_s)

    st = lambda key: jnp.stack(outs[key])
    return (xp.reshape(bsz, seq, d), xs.reshape(db, ds, d), st("mkp"), st("mvp"), st("mks"), st("mvs"),
            st("skp"), st("svp"), st("sks"), st("svs"))
```

```python
import functools

import jax
import jax.numpy as jnp
from jax import lax
from jax.experimental import pallas as pl
from jax.experimental.pallas import tpu as pltpu
from jax.experimental.pallas import tpu_sc as plsc

F32 = jnp.float32
BF16 = jnp.bfloat16
I32 = jnp.int32

D_MODEL = 1024
N_HEADS = 16
HEAD_DIM = 64
KV_HEADS = 4
GROUP = N_HEADS // KV_HEADS
Q_DIM = N_HEADS * HEAD_DIM
KV_DIM = KV_HEADS * HEAD_DIM
QKV_DIM = Q_DIM + 2 * KV_DIM
ATTN_SCALE = HEAD_DIM ** -0.5
ROT_DIM = HEAD_DIM // 4
ROPE_THETA = 500000.0
PAGE_SIZE = 128
MOBA_BLOCK = 256
MOBA_TOPK = 3
WINDOW = 128
N_EXPERTS = 256
TOP_K = 8
N_EXPERT_GROUPS = 8
TOPK_GROUPS = 4
PER_GROUP = N_EXPERTS // N_EXPERT_GROUPS
D_EXPERT = 256
ROUTED_SCALE = 2.5
N_MOD = 6
DEPTH = 2
ALPHA = (2 * DEPTH) ** 0.25
LN_EPS = 1e-5

LANES = 128
VMEM_LIMIT = 48 * 1024 * 1024
MOE_ROWS = 512
NEG_INF = float("-inf")
BIG_INDEX = 1e9

_NT = (((1,), (1,)), ((), ()))


def _dot(a, b):
    return jnp.dot(a, b, preferred_element_type=F32)


def _dot_nt(a, b):
    return lax.dot_general(a, b, _NT, preferred_element_type=F32)


def _split_bf16(x):
    hi = x.astype(BF16)
    lo = (x - hi.astype(F32)).astype(BF16)
    return hi, lo


def _sigmoid(x):
    return 1.0 / (1.0 + jnp.exp(-x))


PACK_DTYPE = jnp.int32
PACK_WIDTH = D_MODEL // 2


def _pack_rows(x):
    half = x.shape[1] // 2
    hi = lax.bitcast_convert_type(x[:, :half].astype(BF16).astype(F32), jnp.uint32)
    lo = lax.bitcast_convert_type(x[:, half:].astype(BF16).astype(F32), jnp.uint32)
    return lax.bitcast_convert_type(hi | (lo >> 16), PACK_DTYPE)


def _unpack_rows(words):
    u = lax.bitcast_convert_type(words, jnp.uint32)
    hi = lax.bitcast_convert_type(u & jnp.uint32(0xFFFF0000), F32)
    lo = lax.bitcast_convert_type(u << 16, F32)
    return hi, lo


def _params(*sem):
    return pltpu.CompilerParams(dimension_semantics=sem, vmem_limit_bytes=VMEM_LIMIT)


def _layer_norm(y, g, b):
    mu = jnp.mean(y, axis=-1, keepdims=True)
    yc = y - mu
    var = jnp.mean(yc * yc, axis=-1, keepdims=True)
    return yc * lax.rsqrt(var + LN_EPS) * g + b


def _ada_kernel(c_ref, w_ref, b_ref, o_ref):
    c = c_ref[...]
    a_hi, a_lo = _split_bf16(c * _sigmoid(c))
    w_hi, w_lo = _split_bf16(w_ref[0])
    o_ref[0] = _dot(a_hi, w_hi) + _dot(a_lo, w_hi) + _dot(a_hi, w_lo) + b_ref[0]


def _ada_mod(c_all, w_ada, b_ada):
    rows = c_all.shape[0]
    depth = w_ada.shape[0]
    return pl.pallas_call(
        _ada_kernel,
        grid=(depth, N_MOD),
        in_specs=[
            pl.BlockSpec((rows, D_MODEL), lambda l, j: (0, 0)),
            pl.BlockSpec((1, D_MODEL, D_MODEL), lambda l, j: (l, 0, j)),
            pl.BlockSpec((1, 1, D_MODEL), lambda l, j: (l, 0, j)),
        ],
        out_specs=pl.BlockSpec((1, rows, D_MODEL), lambda l, j: (l, 0, j)),
        out_shape=jax.ShapeDtypeStruct((depth, rows, N_MOD * D_MODEL), F32),
        compiler_params=_params("arbitrary", "arbitrary"),
        name="ada_mod",
    )(c_all, w_ada, b_ada.reshape(depth, 1, N_MOD * D_MODEL))


def _qkv_kernel(x_ref, sh_ref, sc_ref, w_ref, b_ref, c_ref, s1_ref, s2_ref,
                q_ref, k_ref, v_ref, kb_ref, vb_ref, *qs_ref, q_transposed):
    u = x_ref[...] * (1.0 + sc_ref[0]) + sh_ref[0]
    qkv = _dot(u.astype(BF16), w_ref[...]) + b_ref[...]
    cos, s1, s2 = c_ref[...], s1_ref[...], s2_ref[...]
    rots = []
    for j in range((Q_DIM + KV_DIM) // LANES):
        blk = qkv[:, j * LANES:(j + 1) * LANES]
        rots.append(blk * cos + pltpu.roll(blk, LANES - ROT_DIM // 2, 1) * s1
                    + pltpu.roll(blk, ROT_DIM // 2, 1) * s2)
    nq = Q_DIM // LANES
    for j in range(nq):
        if q_transposed:
            q_ref[0, j * LANES:(j + 1) * LANES, :] = (rots[j] * ATTN_SCALE).T.astype(BF16)
        else:
            q_ref[:, j * LANES:(j + 1) * LANES] = (rots[j] * ATTN_SCALE).astype(BF16)
    for j in range(KV_DIM // LANES):
        k_ref[:, j * LANES:(j + 1) * LANES] = rots[nq + j]
        kb_ref[:, j * LANES:(j + 1) * LANES] = rots[nq + j].astype(BF16)
    v = qkv[:, Q_DIM + KV_DIM:]
    v_ref[...] = v
    vb_ref[...] = v.astype(BF16)
    if qs_ref:
        lane = lax.broadcasted_iota(I32, rots[0].shape, 1)
        halves = []
        for kv in range(KV_HEADS):
            t = rots[2 * kv] + rots[2 * kv + 1]
            halves.append(t + pltpu.roll(t, HEAD_DIM, 1))
        for j in range(KV_DIM // LANES):
            qs_ref[0][:, j * LANES:(j + 1) * LANES] = jnp.where(
                lane < HEAD_DIM, halves[2 * j], halves[2 * j + 1])


def _qkv_project(x, shift, scale, w_bf, b, tabs, tm, with_qsum, q_seq_len=None):
    n = x.shape[0]
    n_tab = tabs[0].shape[0] // tm
    per_mod = n // tm // shift.shape[0]
    mod_rows = shift.shape[1]
    row_spec = lambda width: pl.BlockSpec((tm, width), lambda i: (i, 0))
    mod_spec = pl.BlockSpec((1, mod_rows, D_MODEL), lambda i: (i // per_mod, 0, 0))
    tab_spec = pl.BlockSpec((tm, LANES), lambda i: (i % n_tab, 0))
    if q_seq_len is None:
        q_shape, q_spec = jax.ShapeDtypeStruct((n, Q_DIM), BF16), row_spec(Q_DIM)
    else:
        per_seq = q_seq_len // tm
        q_shape = jax.ShapeDtypeStruct((n // q_seq_len, Q_DIM, q_seq_len), BF16)
        q_spec = pl.BlockSpec((1, Q_DIM, tm), lambda i: (i // per_seq, 0, i % per_seq))
    out_shape = [
        q_shape,
        jax.ShapeDtypeStruct((n, KV_DIM), F32),
        jax.ShapeDtypeStruct((n, KV_DIM), F32),
        jax.ShapeDtypeStruct((n, KV_DIM), BF16),
        jax.ShapeDtypeStruct((n, KV_DIM), BF16),
    ]
    out_specs = [q_spec] + [row_spec(KV_DIM)] * 4
    if with_qsum:
        out_shape.append(jax.ShapeDtypeStruct((n, KV_DIM), F32))
        out_specs.append(row_spec(KV_DIM))
    return pl.pallas_call(
        functools.partial(_qkv_kernel, q_transposed=q_seq_len is not None),
        grid=(n // tm,),
        in_specs=[
            row_spec(D_MODEL), mod_spec, mod_spec,
            pl.BlockSpec((D_MODEL, QKV_DIM), lambda i: (0, 0)),
            pl.BlockSpec((1, QKV_DIM), lambda i: (0, 0)),
            tab_spec, tab_spec, tab_spec,
        ],
        out_specs=out_specs,
        out_shape=out_shape,
        compiler_params=_params("arbitrary"),
        name="qkv_rope",
    )(x, shift, scale, w_bf, b.reshape(1, QKV_DIM), *tabs)


def _rope_tables(pos):
    half = ROT_DIM // 2
    inv_freq = 1.0 / (ROPE_THETA ** (jnp.arange(0, ROT_DIM, 2, dtype=F32) / ROT_DIM))
    ang = pos.astype(F32)[:, None] * inv_freq[None, :]
    cos, sin = jnp.cos(ang), jnp.sin(ang)
    rest = HEAD_DIM - ROT_DIM
    ones = jnp.ones((pos.shape[0], rest), F32)
    zeros = jnp.zeros((pos.shape[0], rest), F32)
    zh = jnp.zeros_like(sin)
    c = jnp.concatenate([cos, cos, ones], -1)
    s1 = jnp.concatenate([-sin, zh, zeros], -1)
    s2 = jnp.concatenate([zh, sin, zeros], -1)
    reps = LANES // HEAD_DIM
    return tuple(jnp.tile(t, (1, reps)) for t in (c, s1, s2))


def _moba_select_kernel(qs_ref, k_ref, sel_ref):
    t_len = k_ref.shape[1]
    nb = t_len // MOBA_BLOCK
    k = k_ref[0]
    nb_pad = 16
    means = jnp.concatenate(
        [jnp.sum(k[n * MOBA_BLOCK:(n + 1) * MOBA_BLOCK], axis=0, keepdims=True) / MOBA_BLOCK
         for n in range(nb)] + [jnp.zeros((nb_pad - nb, KV_DIM), F32)], axis=0)
    lane_head = lax.broadcasted_iota(I32, (nb_pad, KV_DIM), 1) // HEAD_DIM
    q_hi, q_lo = _split_bf16(qs_ref[0])
    pos_blk = lax.broadcasted_iota(I32, (nb, t_len), 1) // MOBA_BLOCK
    blk = lax.broadcasted_iota(I32, (nb, t_len), 0)
    past = blk < pos_blk
    for kv in range(KV_HEADS):
        m_hi, m_lo = _split_bf16(jnp.where(lane_head == kv, means, 0.0))
        gate = (_dot_nt(m_hi, q_hi) + _dot_nt(m_lo, q_hi) + _dot_nt(m_hi, q_lo))[:nb]
        gate = jnp.where(past, gate, NEG_INF)
        rank = jnp.zeros((nb, t_len), F32)
        for m in range(nb):
            other = gate[m:m + 1, :]
            ahead = (other > gate) | ((other == gate) & (blk > m))
            rank = rank + jnp.where(ahead, 1.0, 0.0)
        sel_ref[0, kv] = jnp.where(past & (rank < MOBA_TOPK), 1.0, 0.0)


def _moba_select(qsum, k):
    b, t, _ = k.shape
    nb = t // MOBA_BLOCK
    spec = pl.BlockSpec((1, t, KV_DIM), lambda i: (i, 0, 0))
    return pl.pallas_call(
        _moba_select_kernel,
        grid=(b,),
        in_specs=[spec, spec],
        out_specs=pl.BlockSpec((1, KV_HEADS, nb, t), lambda i: (i, 0, 0, 0)),
        out_shape=jax.ShapeDtypeStruct((b, KV_HEADS, nb, t), F32),
        compiler_params=_params("arbitrary"),
        name="moba_select",
    )(qsum, k)


def _moba_prompt_kernel(q_ref, k_ref, v_ref, sel_ref, o_ref):
    qi = pl.program_id(2)
    tq = MOBA_BLOCK
    cols = GROUP * tq
    q_t = jnp.concatenate([q_ref[0, g] for g in range(GROUP)], axis=1)

    def scores(n):
        kb = k_ref[0, 0, pl.ds(pl.multiple_of(n * tq, tq), tq), :]
        return _dot(kb, q_t)

    key = lax.broadcasted_iota(I32, (tq, cols), 0)
    qry = lax.broadcasted_iota(I32, (tq, cols), 1) & (tq - 1)
    s = jnp.where(key <= qry, scores(qi), NEG_INF)
    m0 = jnp.max(s, axis=0, keepdims=True)
    p = jnp.exp(s - m0)
    l0 = jnp.sum(p, axis=0, keepdims=True)
    acc0 = _dot(v_ref[0, 0, qi], p.astype(BF16))

    def body(n, carry):
        m, l, acc = carry
        chosen = sel_ref[0, 0, pl.ds(n, 1), :]
        keep = jnp.concatenate([chosen] * GROUP, axis=1) > 0.0
        s = jnp.where(keep, scores(n), NEG_INF)
        m_new = jnp.maximum(m, jnp.max(s, axis=0, keepdims=True))
        alpha = jnp.exp(m - m_new)
        p = jnp.exp(s - m_new)
        l = alpha * l + jnp.sum(p, axis=0, keepdims=True)
        acc = alpha * acc + _dot(v_ref[0, 0, n], p.astype(BF16))
        return m_new, l, acc

    _, l, acc = lax.fori_loop(0, qi, body, (m0, l0, acc0))
    _store_token_major(o_ref, acc / l, tq)


def _store_token_major(o_ref, out_t, tq):
    for pair in range(GROUP // 2):
        two = jnp.concatenate([out_t[:, (2 * pair) * tq:(2 * pair + 1) * tq],
                               out_t[:, (2 * pair + 1) * tq:(2 * pair + 2) * tq]], axis=0)
        o_ref[0, :, pair * 2 * HEAD_DIM:(pair + 1) * 2 * HEAD_DIM] = two.T.astype(BF16)


def _moba_prompt(q_t, k_hm, v_t, sel):
    b, _, _, t = q_t.shape
    nb = t // MOBA_BLOCK
    return pl.pallas_call(
        _moba_prompt_kernel,
        grid=(b, KV_HEADS, nb),
        in_specs=[pl.BlockSpec((1, GROUP, HEAD_DIM, MOBA_BLOCK), lambda i, j, n: (i, j, 0, n)),
                  pl.BlockSpec((1, 1, t, HEAD_DIM), lambda i, j, n: (i, j, 0, 0)),
                  pl.BlockSpec((1, 1, nb, HEAD_DIM, MOBA_BLOCK), lambda i, j, n: (i, j, 0, 0, 0)),
                  pl.BlockSpec((1, 1, nb, MOBA_BLOCK), lambda i, j, n: (i, j, 0, n))],
        out_specs=pl.BlockSpec((1, MOBA_BLOCK, GROUP * HEAD_DIM), lambda i, j, n: (i, n, j)),
        out_shape=jax.ShapeDtypeStruct((b, t, Q_DIM), BF16),
        compiler_params=_params("arbitrary", "arbitrary", "arbitrary"),
        name="moba_prompt",
    )(q_t, k_hm, v_t, sel)


def _swa_prompt_kernel(sink_ref, q_ref, kp_ref, k_ref, vp_ref, v_ref, o_ref):
    j = pl.program_id(1)
    n = pl.program_id(2)
    tq = WINDOW
    cols = GROUP * tq
    q_t = jnp.concatenate([q_ref[0, g] for g in range(GROUP)], axis=1)
    keys = jnp.concatenate([kp_ref[0, 0], k_ref[0, 0]], axis=0)
    vals_t = jnp.concatenate([vp_ref[0, 0, 0], v_ref[0, 0, 0]], axis=1)
    key = lax.broadcasted_iota(I32, (2 * tq, cols), 0)
    qry = lax.broadcasted_iota(I32, (2 * tq, cols), 1) & (tq - 1)
    visible = ((key < tq) & (key >= qry)) | ((key >= tq) & (key - tq <= qry))
    first_block = jnp.where(n > 0, 0.0, NEG_INF)
    s = _dot(keys, q_t)
    s = jnp.where(visible, s + jnp.where(key < tq, first_block, 0.0), NEG_INF)
    sink = jnp.concatenate([jnp.full((1, tq), sink_ref[j * GROUP + g], F32) for g in range(GROUP)], axis=1)
    m = jnp.maximum(jnp.max(s, axis=0, keepdims=True), sink)
    p = jnp.exp(s - m)
    den = jnp.sum(p, axis=0, keepdims=True) + jnp.exp(sink - m)
    _store_token_major(o_ref, _dot(vals_t, p.astype(BF16)) / den, tq)


def _swa_prompt(q_t, k_hm, v_t, sinks):
    b, _, _, t = q_t.shape
    nb = t // WINDOW
    before = lambda n: jnp.maximum(n - 1, 0)
    q_spec = pl.BlockSpec((1, GROUP, HEAD_DIM, WINDOW), lambda i, j, n, s: (i, j, 0, n))
    k_own = pl.BlockSpec((1, 1, WINDOW, HEAD_DIM), lambda i, j, n, s: (i, j, n, 0))
    k_prev = pl.BlockSpec((1, 1, WINDOW, HEAD_DIM), lambda i, j, n, s: (i, j, before(n), 0))
    v_own = pl.BlockSpec((1, 1, 1, HEAD_DIM, WINDOW), lambda i, j, n, s: (i, j, n, 0, 0))
    v_prev = pl.BlockSpec((1, 1, 1, HEAD_DIM, WINDOW), lambda i, j, n, s: (i, j, before(n), 0, 0))
    return pl.pallas_call(
        _swa_prompt_kernel,
        grid_spec=pltpu.PrefetchScalarGridSpec(
            num_scalar_prefetch=1,
            grid=(b, KV_HEADS, nb),
            in_specs=[q_spec, k_prev, k_own, v_prev, v_own],
            out_specs=pl.BlockSpec((1, WINDOW, GROUP * HEAD_DIM), lambda i, j, n, s: (i, n, j)),
        ),
        out_shape=jax.ShapeDtypeStruct((b, t, Q_DIM), BF16),
        compiler_params=_params("arbitrary", "arbitrary", "arbitrary"),
        name="swa_prompt",
    )(sinks, q_t, k_hm, k_hm, v_t, v_t)


def _fold_heads(acc):
    rows = acc.shape[0]
    lane_head = lax.broadcasted_iota(I32, (rows, KV_DIM), 1) // HEAD_DIM
    row_head = (lax.broadcasted_iota(I32, (rows, KV_DIM), 0) // GROUP) % KV_HEADS
    a = jnp.where(lane_head == row_head, acc, 0.0)
    a = a[:, :LANES] + a[:, LANES:]
    a = a + pltpu.roll(a, HEAD_DIM, 1)
    return a[:, :HEAD_DIM]


PREFETCH = 3
SLOTS = PREFETCH + 1


def _moba_sample_kernel(pt_ref, lhs_ref, kn_ref, vn_ref, kpool, vpool, o_ref,
                        buf, sem, s_scr, g_scr, stat_scr, l_scr, acc_scr, *, n_pages, chunk_pages):
    b = pl.program_id(0)
    c = pl.program_id(1)
    n_seq = pl.num_programs(0)
    n_chunks = n_pages // chunk_pages
    steps = 2 * n_chunks
    g = b * steps + c
    slot = g % SLOTS
    rows = o_ref.shape[1]
    gate_rows = (lhs_ref.shape[1] - rows) // 2
    blk_pages = MOBA_BLOCK // PAGE_SIZE
    nb = n_pages // blk_pages

    def page_copy(pool, page, slot_, p):
        return pltpu.make_async_copy(pool.at[page], buf.at[slot_, p], sem.at[slot_])

    def start(step, seq, slot_):
        base = jnp.where(step >= n_chunks, step - n_chunks, step) * chunk_pages
        pages = [pt_ref[seq, base + p] for p in range(chunk_pages)]

        @pl.when(step < n_chunks)
        def _():
            for p in range(chunk_pages):
                page_copy(kpool, pages[p], slot_, p).start(priority=p % 2)

        @pl.when(step >= n_chunks)
        def _():
            for p in range(chunk_pages):
                page_copy(vpool, pages[p], slot_, p).start(priority=p % 2)

    @pl.when(g == 0)
    def _():
        for ahead in range(PREFETCH):
            start(jnp.int32(ahead % steps), jnp.int32(ahead // steps), ahead % SLOTS)

    nxt = g + PREFETCH

    @pl.when(nxt < n_seq * steps)
    def _():
        start(nxt % steps, nxt // steps, nxt % SLOTS)

    for p in range(chunk_pages):
        page_copy(kpool, 0, slot, p).wait()

    lhs = lhs_ref[0]
    qbd = lhs[:rows]

    @pl.when(c < n_chunks)
    def _():
        for p in range(chunk_pages):
            res = _dot(lhs, buf[slot, p].astype(BF16))
            s_scr[c * chunk_pages + p] = res[:rows]
            g_scr[c * chunk_pages + p] = res[rows:]

    @pl.when(c == n_chunks - 1)
    def _():
        lane = lax.broadcasted_iota(I32, (gate_rows, LANES), 1).astype(F32)
        gate = jnp.full((gate_rows, LANES), NEG_INF, F32)
        for n in range(nb):
            part = g_scr[n * blk_pages]
            for j in range(1, blk_pages):
                part = part + g_scr[n * blk_pages + j]
            col = jnp.sum(part[:gate_rows] + part[gate_rows:], axis=-1, keepdims=True) / MOBA_BLOCK
            gate = jnp.where(lane == n, col, gate)
        sel = jnp.zeros(gate.shape, F32)
        for _ in range(min(MOBA_TOPK, nb)):
            top = jnp.max(gate, axis=-1, keepdims=True)
            first = jnp.min(jnp.where(gate == top, lane, BIG_INDEX), axis=-1, keepdims=True)
            hit = lane == first
            sel = jnp.where(hit, 1.0, sel)
            gate = jnp.where(hit, NEG_INF, gate)
        sel_rows = jnp.concatenate(
            [jnp.broadcast_to(sel[r:r + 1], (GROUP, LANES)) for r in range(sel.shape[0])], axis=0)
        s_new = _dot_nt(qbd, kn_ref[0].astype(BF16))
        qrow = lax.broadcasted_iota(I32, s_new.shape, 0) // N_HEADS
        s_new = jnp.where(lax.broadcasted_iota(I32, s_new.shape, 1) <= qrow, s_new, NEG_INF)
        top = s_new
        for n in range(nb):
            keep = sel_rows[:, n:n + 1] > 0.0
            for j in range(blk_pages):
                s = jnp.where(keep, s_scr[n * blk_pages + j], NEG_INF)
                s_scr[n * blk_pages + j] = s
                top = jnp.maximum(top, s)
        m_run = jnp.max(top, axis=-1, keepdims=True)
        p_new = jnp.exp(s_new - m_run)
        stat_scr[:, 0:1] = m_run
        stat_scr[:, 1:2] = jnp.sum(p_new, axis=-1, keepdims=True)
        l_scr[...] = jnp.zeros(l_scr.shape, F32)
        acc_scr[...] = _dot(p_new.astype(BF16), vn_ref[0].astype(BF16))

    @pl.when(c >= n_chunks)
    def _():
        m_run = stat_scr[:, 0:1]
        l = l_scr[...]
        acc = acc_scr[...]
        for p in range(chunk_pages):
            prob = jnp.exp(s_scr[(c - n_chunks) * chunk_pages + p] - m_run)
            l = l + prob
            acc = acc + _dot_nt(prob.astype(BF16), buf[slot, p].astype(BF16))
        l_scr[...] = l
        acc_scr[...] = acc

    @pl.when(c == steps - 1)
    def _():
        den = jnp.sum(l_scr[...], axis=-1, keepdims=True) + stat_scr[:, 1:2]
        o_ref[0] = _fold_heads(acc_scr[...] / den)


def _moba_sample(page_table, lhs, k_new, v_new, kpool_t, vpool_t):
    db = lhs.shape[0]
    rows = lhs.shape[1] * N_HEADS // (N_HEADS + 2 * KV_HEADS)
    n_pages = page_table.shape[1]
    chunk_pages = n_pages // 2
    nb = n_pages * PAGE_SIZE // MOBA_BLOCK
    assert nb <= LANES and chunk_pages % (MOBA_BLOCK // PAGE_SIZE) == 0
    steps = 2 * (n_pages // chunk_pages)
    seq3 = lambda a: pl.BlockSpec((1,) + a.shape[1:], lambda i, c, pt: (i, 0, 0))
    kern = functools.partial(_moba_sample_kernel, n_pages=n_pages, chunk_pages=chunk_pages)
    return pl.pallas_call(
        kern,
        grid_spec=pltpu.PrefetchScalarGridSpec(
            num_scalar_prefetch=1,
            grid=(db, steps),
            in_specs=[seq3(lhs), seq3(k_new), seq3(v_new),
                      pl.BlockSpec(memory_space=pl.ANY), pl.BlockSpec(memory_space=pl.ANY)],
            out_specs=pl.BlockSpec((1, rows, HEAD_DIM), lambda i, c, pt: (i, 0, 0)),
            scratch_shapes=[
                pltpu.VMEM((SLOTS, chunk_pages, KV_DIM, PAGE_SIZE), F32),
                pltpu.SemaphoreType.DMA((SLOTS,)),
                pltpu.VMEM((n_pages, rows, PAGE_SIZE), F32),
                pltpu.VMEM((n_pages, lhs.shape[1] - rows, PAGE_SIZE), F32),
                pltpu.VMEM((rows, LANES), F32),
                pltpu.VMEM((rows, LANES), F32),
                pltpu.VMEM((rows, KV_DIM), F32),
            ],
        ),
        out_shape=jax.ShapeDtypeStruct((db, rows, HEAD_DIM), F32),
        compiler_params=_params("arbitrary", "arbitrary"),
        name="moba_sample",
    )(page_table, lhs, k_new, v_new, kpool_t, vpool_t)


def _swa_sample_kernel(qbd_ref, sink_ref, kb_ref, vb_ref, kn_ref, vn_ref, o_ref):
    rows = qbd_ref.shape[1]
    wb = kb_ref.shape[1]
    ds = kn_ref.shape[1]
    sink = sink_ref[...]
    qrow_b = lax.broadcasted_iota(I32, (rows, wb), 0) // N_HEADS
    buf_ok = lax.broadcasted_iota(I32, (rows, wb), 1) >= qrow_b
    qrow_n = lax.broadcasted_iota(I32, (rows, ds), 0) // N_HEADS
    new_ok = lax.broadcasted_iota(I32, (rows, ds), 1) <= qrow_n
    for s in range(qbd_ref.shape[0]):
        qbd = qbd_ref[s]
        sb = jnp.where(buf_ok, _dot_nt(qbd, kb_ref[s].astype(BF16)), NEG_INF)
        sn = jnp.where(new_ok, _dot_nt(qbd, kn_ref[s].astype(BF16)), NEG_INF)
        m = jnp.maximum(jnp.maximum(jnp.max(sb, axis=-1, keepdims=True),
                                    jnp.max(sn, axis=-1, keepdims=True)), sink)
        pb = jnp.exp(sb - m)
        pn = jnp.exp(sn - m)
        den = (jnp.sum(pb, axis=-1, keepdims=True) + jnp.sum(pn, axis=-1, keepdims=True)
               + jnp.exp(sink - m))
        acc = _dot(pb.astype(BF16), vb_ref[s].astype(BF16)) + _dot(pn.astype(BF16), vn_ref[s].astype(BF16))
        o_ref[s] = _fold_heads(acc / den)


def _swa_sample(qbd, sink_rows, buf_k, buf_v, k_new, v_new, seqs_per_step=8):
    db, rows, _ = qbd.shape
    wb = buf_k.shape[1]
    ds = k_new.shape[1]
    spec = lambda r: pl.BlockSpec((seqs_per_step, r, KV_DIM), lambda i: (i, 0, 0))
    return pl.pallas_call(
        _swa_sample_kernel,
        grid=(db // seqs_per_step,),
        in_specs=[spec(rows), pl.BlockSpec((rows, 1), lambda i: (0, 0)),
                  spec(wb), spec(wb), spec(ds), spec(ds)],
        out_specs=pl.BlockSpec((seqs_per_step, rows, HEAD_DIM), lambda i: (i, 0, 0)),
        out_shape=jax.ShapeDtypeStruct((db, rows, HEAD_DIM), F32),
        compiler_params=_params("arbitrary"),
        name="swa_sample",
    )(qbd, sink_rows, buf_k, buf_v, k_new, v_new)


def _attn_out_kernel(a_ref, x_ref, g_ref, sh_ref, sc_ref, wo_ref, bo_ref, lg_ref, lb_ref,
                     wrh_ref, wrl_ref, xn_ref, h_ref, lt_ref):
    o = _dot(a_ref[...], wo_ref[...]) + bo_ref[...]
    xn = _layer_norm(ALPHA * x_ref[...] + g_ref[0] * o, lg_ref[...], lb_ref[...])
    xn_ref[...] = xn
    h = xn * (1.0 + sc_ref[0]) + sh_ref[0]
    h_ref[...] = _pack_rows(h)
    h_hi, h_lo = _split_bf16(h)
    wrh = wrh_ref[...]
    lt_ref[...] = _dot_nt(wrh, h_hi) + _dot_nt(wrh, h_lo) + _dot_nt(wrl_ref[...], h_hi)


def _attn_out(a, x, gate, shift, scale, wo_bf, bo, ln_g, ln_b, wr_hi, wr_lo, tm):
    n = x.shape[0]
    per_mod = n // tm // gate.shape[0]
    mod_rows = gate.shape[1]
    row = pl.BlockSpec((tm, D_MODEL), lambda i: (i, 0))
    mod = pl.BlockSpec((1, mod_rows, D_MODEL), lambda i: (i // per_mod, 0, 0))
    vec = pl.BlockSpec((1, D_MODEL), lambda i: (0, 0))
    full = lambda r, c: pl.BlockSpec((r, c), lambda i: (0, 0))
    return pl.pallas_call(
        _attn_out_kernel,
        grid=(n // tm,),
        in_specs=[row, row, mod, mod, mod, full(Q_DIM, D_MODEL), vec, vec, vec,
                  full(N_EXPERTS, D_MODEL), full(N_EXPERTS, D_MODEL)],
        out_specs=[row, pl.BlockSpec((tm, PACK_WIDTH), lambda i: (i, 0)),
                   pl.BlockSpec((N_EXPERTS, tm), lambda i: (0, i))],
        out_shape=[jax.ShapeDtypeStruct((n, D_MODEL), F32),
                   jax.ShapeDtypeStruct((n, PACK_WIDTH), PACK_DTYPE),
                   jax.ShapeDtypeStruct((N_EXPERTS, n), F32)],
        compiler_params=_params("arbitrary"),
        name="attn_out_norm",
    )(a, x, gate, shift, scale, wo_bf, bo.reshape(1, -1), ln_g.reshape(1, -1), ln_b.reshape(1, -1),
      wr_hi, wr_lo)


def _first_max(x, idx):
    top = jnp.max(x, axis=0, keepdims=True)
    first = jnp.min(jnp.where(x == top, idx, BIG_INDEX), axis=0, keepdims=True)
    return top, first


def _route_kernel(lt_ref, b_ref, idx_ref, w_ref, pos_ref, cnt_ref):
    i = pl.program_id(0)
    tn = lt_ref.shape[1]
    scores = _sigmoid(lt_ref[...])
    biased = scores + b_ref[...]
    e_idx = lax.broadcasted_iota(I32, (N_EXPERTS, tn), 0).astype(F32)
    l_idx = lax.broadcasted_iota(I32, (PER_GROUP, tn), 0).astype(F32)
    g_score = []
    for g in range(N_EXPERT_GROUPS):
        x = biased[g * PER_GROUP:(g + 1) * PER_GROUP]
        top1, first = _first_max(x, l_idx)
        top2 = jnp.max(jnp.where(l_idx == first, NEG_INF, x), axis=0, keepdims=True)
        g_score.append(top1 + top2)
    pieces = []
    for g in range(N_EXPERT_GROUPS):
        ahead = jnp.zeros((1, tn), F32)
        for o in range(N_EXPERT_GROUPS):
            if o != g:
                wins = (g_score[o] >= g_score[g]) if o < g else (g_score[o] > g_score[g])
                ahead = ahead + jnp.where(wins, 1.0, 0.0)
        pieces.append(jnp.where(ahead < TOPK_GROUPS, biased[g * PER_GROUP:(g + 1) * PER_GROUP], NEG_INF))
    masked = jnp.concatenate(pieces, axis=0)
    chosen = jnp.zeros((N_EXPERTS, tn), F32)
    firsts, picked = [], []
    for _ in range(TOP_K):
        _, first = _first_max(masked, e_idx)
        hit = e_idx == first
        firsts.append(first)
        picked.append(jnp.sum(jnp.where(hit, scores, 0.0), axis=0, keepdims=True))
        chosen = jnp.where(hit, 1.0, chosen)
        masked = jnp.where(hit, NEG_INF, masked)
    total = picked[0]
    for s in picked[1:]:
        total = total + s
    for r in range(TOP_K):
        idx_ref[r:r + 1, :] = firsts[r].astype(I32)
        w_ref[r:r + 1, :] = picked[r] / total * ROUTED_SCALE

    @pl.when(i == 0)
    def _():
        cnt_ref[...] = jnp.zeros(cnt_ref.shape, F32)
    r = lax.broadcasted_iota(I32, (tn, tn), 0)
    c = lax.broadcasted_iota(I32, (tn, tn), 1)
    before = jnp.where(r < c, 1.0, 0.0).astype(BF16)
    prior = _dot(chosen.astype(BF16), before) + cnt_ref[...]
    for r in range(TOP_K):
        pos_ref[r:r + 1, :] = jnp.sum(jnp.where(e_idx == firsts[r], prior, 0.0), axis=0,
                                      keepdims=True).astype(I32)
    cnt_ref[...] = cnt_ref[...] + jnp.sum(chosen, axis=1, keepdims=True)


def _route(logits_t, b_router, tn=512):
    n = logits_t.shape[1]
    slot = pl.BlockSpec((TOP_K, tn), lambda i: (0, i))
    return pl.pallas_call(
        _route_kernel,
        grid=(n // tn,),
        in_specs=[pl.BlockSpec((N_EXPERTS, tn), lambda i: (0, i)),
                  pl.BlockSpec((N_EXPERTS, 1), lambda i: (0, 0))],
        out_specs=[slot, slot, slot, pl.BlockSpec((N_EXPERTS, 1), lambda i: (0, 0))],
        out_shape=[jax.ShapeDtypeStruct((TOP_K, n), I32),
                   jax.ShapeDtypeStruct((TOP_K, n), F32),
                   jax.ShapeDtypeStruct((TOP_K, n), I32),
                   jax.ShapeDtypeStruct((N_EXPERTS, 1), F32)],
        compiler_params=_params("arbitrary"),
        name="route",
    )(logits_t, b_router.reshape(N_EXPERTS, 1))


def _dest_kernel(idx_ref, pos_ref, start_ref, dest_ref):
    tn = idx_ref.shape[1]
    e_idx = lax.broadcasted_iota(I32, (N_EXPERTS, tn), 0)
    start = start_ref[...]
    for r in range(TOP_K):
        first = jnp.sum(jnp.where(e_idx == idx_ref[r:r + 1, :], start, 0.0), axis=0, keepdims=True)
        dest_ref[r:r + 1, :] = first.astype(I32) + pos_ref[r:r + 1, :]


def _dest_rows(idx_t, pos_t, pad_start, tn=512):
    n = idx_t.shape[1]
    slot = pl.BlockSpec((TOP_K, tn), lambda i: (0, i))
    return pl.pallas_call(
        _dest_kernel,
        grid=(n // tn,),
        in_specs=[slot, slot, pl.BlockSpec((N_EXPERTS, 1), lambda i: (0, 0))],
        out_specs=slot,
        out_shape=jax.ShapeDtypeStruct((TOP_K, n), I32),
        compiler_params=_params("arbitrary"),
        name="dest_rows",
    )(idx_t, pos_t, pad_start.astype(F32).reshape(N_EXPERTS, 1))


SC_CORES = 2
SC_SUBCORES = 16
SC_WINDOW = 48


def _scatter_rows(rows_tok, dest_flat, n_rows):
    n, width = rows_tok.shape
    workers = SC_CORES * SC_SUBCORES
    per_worker = n // workers
    assert n % workers == 0 and per_worker % SC_WINDOW == 0 and SC_WINDOW % 8 == 0
    mesh = plsc.VectorSubcoreMesh(core_axis_name="c", subcore_axis_name="s",
                                  num_cores=SC_CORES, num_subcores=SC_SUBCORES)

    def body(rows_hbm, dest_hbm, out_hbm, idx_v, rows_v):
        worker = lax.axis_index("s") * SC_CORES + lax.axis_index("c")
        base = worker * per_worker

        @pl.loop(0, per_worker // SC_WINDOW)
        def _(j):
            t0 = pl.multiple_of(base + j * SC_WINDOW, 8)
            pltpu.sync_copy(rows_hbm.at[pl.ds(t0, SC_WINDOW)], rows_v)
            for r in range(TOP_K):
                pltpu.sync_copy(dest_hbm.at[pl.ds(pl.multiple_of(r * n + t0, 8), SC_WINDOW)], idx_v)
                pltpu.sync_copy(rows_v, out_hbm.at[idx_v])

    return pl.kernel(
        body,
        out_type=jax.ShapeDtypeStruct((n_rows, width), rows_tok.dtype),
        mesh=mesh,
        scratch_types=[pltpu.VMEM((SC_WINDOW,), I32), pltpu.VMEM((SC_WINDOW, width), rows_tok.dtype)],
        name="scatter_rows",
    )(rows_tok, dest_flat)


def _moe_kernel(be_ref, nu_ref, x_ref, wg_ref, wu_ref, wd_ref, o_ref, wg_bf, wu_bf, wd_bf):
    i = pl.program_id(0)
    used = i < nu_ref[0]
    changed = (i == 0) | (be_ref[i] != be_ref[jnp.maximum(i - 1, 0)])

    @pl.when(used & changed)
    def _():
        wg_bf[...] = wg_ref[0, 0].astype(BF16)
        wu_bf[...] = wu_ref[0, 0].astype(BF16)
        wd_bf[...] = wd_ref[0, 0].astype(BF16)

    @pl.when(used)
    def _():
        x = jnp.concatenate(_unpack_rows(x_ref[...]), axis=1).astype(BF16)
        gate = _dot(x, wg_bf[...])
        up = _dot(x, wu_bf[...])
        act = gate * _sigmoid(gate) * up
        o_ref[...] = _pack_rows(_dot(act.astype(BF16), wd_bf[...]))


def _moe_experts(blk_e, n_used, x_sorted, w_gate, w_up, w_down, layer):
    rows = x_sorted.shape[0]
    n_blocks = rows // MOE_ROWS
    x_map = lambda i, be, nu: (jnp.minimum(i, nu[0] - 1), 0)
    w_map = lambda i, be, nu: (layer, be[i], 0, 0)
    return pl.pallas_call(
        _moe_kernel,
        grid_spec=pltpu.PrefetchScalarGridSpec(
            num_scalar_prefetch=2,
            grid=(n_blocks,),
            in_specs=[pl.BlockSpec((MOE_ROWS, PACK_WIDTH), x_map),
                      pl.BlockSpec((1, 1, D_MODEL, D_EXPERT), w_map),
                      pl.BlockSpec((1, 1, D_MODEL, D_EXPERT), w_map),
                      pl.BlockSpec((1, 1, D_EXPERT, D_MODEL), w_map)],
            out_specs=pl.BlockSpec((MOE_ROWS, PACK_WIDTH), x_map),
            scratch_shapes=[pltpu.VMEM((D_MODEL, D_EXPERT), BF16),
                            pltpu.VMEM((D_MODEL, D_EXPERT), BF16),
                            pltpu.VMEM((D_EXPERT, D_MODEL), BF16)],
        ),
        out_shape=jax.ShapeDtypeStruct((rows, PACK_WIDTH), PACK_DTYPE),
        compiler_params=_params("arbitrary"),
        name="moe_experts",
    )(blk_e, n_used, x_sorted, w_gate, w_up, w_down)


def _ffn_out_kernel(h_ref, y_ref, w_ref, x_ref, g_ref, wsg_ref, wsu_ref, wsd_ref, lg_ref, lb_ref, o_ref):
    h = jnp.concatenate(_unpack_rows(h_ref[...]), axis=1).astype(BF16)
    gate = _dot(h, wsg_ref[...])
    up = _dot(h, wsu_ref[...])
    shared = _dot((gate * _sigmoid(gate) * up).astype(BF16), wsd_ref[...])
    w = w_ref[...]
    first, second = None, None
    for r in range(TOP_K):
        ya, yb = _unpack_rows(y_ref[r])
        wr = w[:, r:r + 1]
        first = ya * wr if first is None else first + ya * wr
        second = yb * wr if second is None else second + yb * wr
    ffn = shared + jnp.concatenate([first, second], axis=1)
    y = ALPHA * x_ref[...] + g_ref[0] * ffn
    o_ref[...] = _layer_norm(y, lg_ref[...], lb_ref[...])


def _ffn_out(h, y_tok, w_tok, first_row, x, gate, wsg, wsu, wsd, ln_g, ln_b, tm):
    n = x.shape[0]
    per_mod = n // tm // gate.shape[0]
    off = first_row // tm
    row = pl.BlockSpec((tm, D_MODEL), lambda i: (i, 0))
    mod = pl.BlockSpec((1, gate.shape[1], D_MODEL), lambda i: (i // per_mod, 0, 0))
    vec = pl.BlockSpec((1, D_MODEL), lambda i: (0, 0))
    full = lambda a: pl.BlockSpec(a.shape, lambda i: (0, 0))
    return pl.pallas_call(
        _ffn_out_kernel,
        grid=(n // tm,),
        in_specs=[pl.BlockSpec((tm, PACK_WIDTH), lambda i: (i, 0)),
                  pl.BlockSpec((TOP_K, tm, PACK_WIDTH), lambda i: (0, i + off, 0)),
                  pl.BlockSpec((tm, TOP_K), lambda i: (i + off, 0)),
                  row, mod, full(wsg), full(wsu), full(wsd), vec, vec],
        out_specs=row,
        out_shape=jax.ShapeDtypeStruct((n, D_MODEL), F32),
        compiler_params=_params("arbitrary"),
        name="ffn_out_norm",
    )(h, y_tok, w_tok, x, gate, wsg, wsu, wsd, ln_g.reshape(1, -1), ln_b.reshape(1, -1))


def _dispatch(idx_t, pos_t, counts):
    n = idx_t.shape[1]
    counts = counts.reshape(N_EXPERTS).astype(I32)
    padded = (counts + MOE_ROWS - 1) // MOE_ROWS * MOE_ROWS
    pad_end = jnp.cumsum(padded)
    pad_start = pad_end - padded
    dest = _dest_rows(idx_t, pos_t, pad_start)
    n_blocks = (n * TOP_K + N_EXPERTS * (MOE_ROWS - 1)) // MOE_ROWS
    n_used = pad_end[-1] // MOE_ROWS
    blk = jnp.minimum(jnp.arange(n_blocks, dtype=I32), n_used - 1) * MOE_ROWS
    blk_e = jnp.minimum(jnp.sum((pad_end[None, :] <= blk[:, None]).astype(I32), axis=1), N_EXPERTS - 1)
    return dest, n_blocks * MOE_ROWS, blk_e, n_used.reshape(1).astype(I32)


def _block_diag_rows(x, heads_per_kv):
    db, ds, heads, _ = x.shape
    kv_of_head = jnp.arange(heads) // heads_per_kv
    onehot = (kv_of_head[:, None] == jnp.arange(KV_HEADS)[None, :]).astype(x.dtype)
    out = x[:, :, :, None, :] * onehot[None, None, :, :, None]
    return out.reshape(db, ds * heads, KV_DIM)


def kernel(x_prompt, x_sample, c_prompt, c_sample, cache_moba_k, cache_moba_v, state_swa_k, state_swa_v,
           page_table, w_ada, b_ada, w_qkv, b_qkv, attn_sinks, w_o, b_o, ln_attn_g, ln_attn_b,
           w_router, b_router, w_exp_gate, w_exp_up, w_exp_down, w_sh_gate, w_sh_up, w_sh_down,
           ln_ffn_g, ln_ffn_b):
    bsz, seq, d = x_prompt.shape
    db, ds, _ = x_sample.shape
    n_p, n_s = bsz * seq, db * ds
    past_len = page_table.shape[1] * PAGE_SIZE
    tm_p, tm_s, tm_norm = 512, 256, 256
    depth = w_ada.shape[0]

    mods = _ada_mod(jnp.concatenate([c_prompt, c_sample], 0), w_ada, b_ada)
    tabs_p = _rope_tables(jnp.arange(seq, dtype=I32))
    tabs_s = _rope_tables(jnp.tile(past_len + jnp.arange(ds, dtype=I32), db))

    xp = x_prompt.reshape(n_p, d)
    xs = x_sample.reshape(n_s, d)
    outs = {k: [] for k in ("mkp", "mvp", "mks", "mvs", "skp", "svp", "sks", "svs")}
    for i in range(depth):
        j = i // 2
        m = mods[i].reshape(bsz + db, N_MOD, d)
        mp = [m[:bsz, t][:, None, :] for t in range(N_MOD)]
        ms = [jnp.repeat(m[bsz:, t], ds, axis=0).reshape(n_s // tm_s, tm_s, d)
              for t in range(N_MOD)]
        w_qkv_bf = w_qkv[i].astype(BF16)
        moba = i % 2 == 0
        res_p = _qkv_project(xp, mp[0], mp[1], w_qkv_bf, b_qkv[i], tabs_p, tm_p, moba, q_seq_len=seq)
        res_s = _qkv_project(xs, ms[0], ms[1], w_qkv_bf, b_qkv[i], tabs_s, tm_s, moba)
        qp, kp, vp, kp_bf, vp_bf = res_p[:5]
        qs, ks, vs = res_s[:3]
        blk = MOBA_BLOCK if moba else WINDOW
        q_t = qp.reshape(bsz, N_HEADS, HEAD_DIM, seq)
        k_hm = kp_bf.reshape(bsz, seq, KV_HEADS, HEAD_DIM).transpose(0, 2, 1, 3)
        v_t = vp_bf.reshape(bsz, seq // blk, blk, KV_HEADS, HEAD_DIM).transpose(0, 3, 1, 4, 2)
        qbd = _block_diag_rows(qs.reshape(db, ds, N_HEADS, HEAD_DIM), GROUP)
        pad_rows = ((0, 0), (0, LANES - ds), (0, 0))
        ks3 = jnp.pad(ks.reshape(db, ds, KV_DIM), pad_rows)
        vs3 = jnp.pad(vs.reshape(db, ds, KV_DIM), pad_rows)
        kp5 = kp.reshape(bsz, seq, KV_HEADS, HEAD_DIM)
        vp5 = vp.reshape(bsz, seq, KV_HEADS, HEAD_DIM)
        ks5 = ks.reshape(db, ds, KV_HEADS, HEAD_DIM)
        vs5 = vs.reshape(db, ds, KV_HEADS, HEAD_DIM)
        if moba:
            sel = _moba_select(res_p[5].reshape(bsz, seq, KV_DIM), kp.reshape(bsz, seq, KV_DIM))
            a_t = _moba_prompt(q_t, k_hm, v_t, sel)
            qs_hi, qs_lo = _split_bf16(_block_diag_rows(res_s[5].reshape(db, ds, KV_HEADS, HEAD_DIM), 1))
            n_pool = cache_moba_k.shape[1]
            pool_t = lambda pool: pool[j].transpose(0, 2, 3, 1).reshape(n_pool, KV_DIM, PAGE_SIZE)
            a_s = _moba_sample(page_table, jnp.concatenate([qbd, qs_hi, qs_lo], axis=1), ks3, vs3,
                               pool_t(cache_moba_k), pool_t(cache_moba_v))
            outs["mkp"].append(kp5)
            outs["mvp"].append(vp5)
            outs["mks"].append(ks5)
            outs["mvs"].append(vs5)
        else:
            a_t = _swa_prompt(q_t, k_hm, v_t, attn_sinks[j])
            wb = state_swa_k.shape[2]
            buf_k = state_swa_k[j].reshape(db, wb, KV_DIM)
            buf_v = state_swa_v[j].reshape(db, wb, KV_DIM)
            sink_rows = jnp.tile(attn_sinks[j], ds).reshape(ds * N_HEADS, 1)
            a_s = _swa_sample(qbd, sink_rows, buf_k, buf_v, ks3, vs3)
            wbp = min(WINDOW, seq)
            outs["skp"].append(kp5[:, -wbp:])
            outs["svp"].append(vp5[:, -wbp:])
            outs["sks"].append(jnp.concatenate([state_swa_k[j], ks5], axis=1)[:, -wb:])
            outs["svs"].append(jnp.concatenate([state_swa_v[j], vs5], axis=1)[:, -wb:])
        a_p = a_t.reshape(n_p, Q_DIM)
        a_s = a_s.reshape(n_s, Q_DIM).astype(BF16)

        wo_bf = w_o[i].astype(BF16)
        wr_hi, wr_lo = _split_bf16(w_router[i].T)
        args = (wo_bf, b_o[i], ln_attn_g[i], ln_attn_b[i], wr_hi, wr_lo)
        xp, hp, ltp = _attn_out(a_p, xp, mp[2], mp[3], mp[4], *args, tm_norm)
        xs, hs, lts = _attn_out(a_s, xs, ms[2], ms[3], ms[4], *args, tm_s)

        h_all = jnp.concatenate([hp, hs], axis=0)
        idx_t, w_t, pos_t, counts = _route(jnp.concatenate([ltp, lts], axis=1), b_router[i])
        dest, n_rows, blk_e, n_used = _dispatch(idx_t, pos_t, counts)
        x_sorted = _scatter_rows(h_all, dest.reshape(-1), n_rows)
        y_sorted = _moe_experts(blk_e, n_used, x_sorted, w_exp_gate, w_exp_up, w_exp_down, i)
        y_tok = y_sorted[dest.reshape(-1)].reshape(TOP_K, n_p + n_s, -1)
        w_tok = w_t.T

        ws = (w_sh_gate[i].astype(BF16), w_sh_up[i].astype(BF16), w_sh_down[i].astype(BF16))
        norm = (ln_ffn_g[i], ln_ffn_b[i])
        xp = _ffn_out(hp, y_tok, w_tok, 0, xp, mp[5], *ws, *norm, tm_norm)
        xs = _ffn_out(hs, y_tok, w_tok, n_p, xs, ms[5], *ws, *norm, tm_s)

    st = lambda key: jnp.stack(outs[key])
    return (xp.reshape(bsz, seq, d), xs.reshape(db, ds, d), st("mkp"), st("mvp"), st("mks"), st("mvs"),
            st("skp"), st("svp"), st("sks"), st("svs"))
```

```python
import functools

import jax
import jax.numpy as jnp
from jax import lax
from jax.experimental import pallas as pl
from jax.experimental.pallas import tpu as pltpu
from jax.experimental.pallas import tpu_sc as plsc

F32 = jnp.float32
BF16 = jnp.bfloat16
I32 = jnp.int32

D_MODEL = 1024
N_HEADS = 16
HEAD_DIM = 64
KV_HEADS = 4
GROUP = N_HEADS // KV_HEADS
Q_DIM = N_HEADS * HEAD_DIM
KV_DIM = KV_HEADS * HEAD_DIM
QKV_DIM = Q_DIM + 2 * KV_DIM
ATTN_SCALE = HEAD_DIM ** -0.5
ROT_DIM = HEAD_DIM // 4
ROPE_THETA = 500000.0
PAGE_SIZE = 128
MOBA_BLOCK = 256
MOBA_TOPK = 3
WINDOW = 128
N_EXPERTS = 256
TOP_K = 8
N_EXPERT_GROUPS = 8
TOPK_GROUPS = 4
PER_GROUP = N_EXPERTS // N_EXPERT_GROUPS
D_EXPERT = 256
ROUTED_SCALE = 2.5
N_MOD = 6
DEPTH = 2
ALPHA = (2 * DEPTH) ** 0.25
LN_EPS = 1e-5

LANES = 128
VMEM_LIMIT = 48 * 1024 * 1024
MOE_ROWS = 512
NEG_INF = float("-inf")
BIG_INDEX = 1e9

_NT = (((1,), (1,)), ((), ()))


def _dot(a, b):
    return jnp.dot(a, b, preferred_element_type=F32)


def _dot_nt(a, b):
    return lax.dot_general(a, b, _NT, preferred_element_type=F32)


def _split_bf16(x):
    hi = x.astype(BF16)
    lo = (x - hi.astype(F32)).astype(BF16)
    return hi, lo


def _sigmoid(x):
    return 1.0 / (1.0 + jnp.exp(-x))


PACK_DTYPE = jnp.int32
PACK_WIDTH = D_MODEL // 2


def _pack_rows(x):
    half = x.shape[1] // 2
    hi = lax.bitcast_convert_type(x[:, :half].astype(BF16).astype(F32), jnp.uint32)
    lo = lax.bitcast_convert_type(x[:, half:].astype(BF16).astype(F32), jnp.uint32)
    return lax.bitcast_convert_type(hi | (lo >> 16), PACK_DTYPE)


def _unpack_rows(words):
    u = lax.bitcast_convert_type(words, jnp.uint32)
    hi = lax.bitcast_convert_type(u & jnp.uint32(0xFFFF0000), F32)
    lo = lax.bitcast_convert_type(u << 16, F32)
    return hi, lo


def _params(*sem):
    return pltpu.CompilerParams(dimension_semantics=sem, vmem_limit_bytes=VMEM_LIMIT)


def _layer_norm(y, g, b):
    mu = jnp.mean(y, axis=-1, keepdims=True)
    yc = y - mu
    var = jnp.mean(yc * yc, axis=-1, keepdims=True)
    return yc * lax.rsqrt(var + LN_EPS) * g + b


def _ada_kernel(c_ref, w_ref, b_ref, o_ref):
    c = c_ref[...]
    a_hi, a_lo = _split_bf16(c * _sigmoid(c))
    w_hi, w_lo = _split_bf16(w_ref[0])
    o_ref[0] = _dot(a_hi, w_hi) + _dot(a_lo, w_hi) + _dot(a_hi, w_lo) + b_ref[0]


def _ada_mod(c_all, w_ada, b_ada):
    rows = c_all.shape[0]
    depth = w_ada.shape[0]
    return pl.pallas_call(
        _ada_kernel,
        grid=(depth, N_MOD),
        in_specs=[
            pl.BlockSpec((rows, D_MODEL), lambda l, j: (0, 0)),
            pl.BlockSpec((1, D_MODEL, D_MODEL), lambda l, j: (l, 0, j)),
            pl.BlockSpec((1, 1, D_MODEL), lambda l, j: (l, 0, j)),
        ],
        out_specs=pl.BlockSpec((1, rows, D_MODEL), lambda l, j: (l, 0, j)),
        out_shape=jax.ShapeDtypeStruct((depth, rows, N_MOD * D_MODEL), F32),
        compiler_params=_params("arbitrary", "arbitrary"),
        name="ada_mod",
    )(c_all, w_ada, b_ada.reshape(depth, 1, N_MOD * D_MODEL))


def _qkv_kernel(x_ref, sh_ref, sc_ref, w_ref, b_ref, c_ref, s1_ref, s2_ref,
                q_ref, k_ref, v_ref, kb_ref, vb_ref, *qs_ref, q_transposed):
    u = x_ref[...] * (1.0 + sc_ref[0]) + sh_ref[0]
    qkv = _dot(u.astype(BF16), w_ref[...]) + b_ref[...]
    cos, s1, s2 = c_ref[...], s1_ref[...], s2_ref[...]
    rots = []
    for j in range((Q_DIM + KV_DIM) // LANES):
        blk = qkv[:, j * LANES:(j + 1) * LANES]
        rots.append(blk * cos + pltpu.roll(blk, LANES - ROT_DIM // 2, 1) * s1
                    + pltpu.roll(blk, ROT_DIM // 2, 1) * s2)
    nq = Q_DIM // LANES
    for j in range(nq):
        if q_transposed:
            q_ref[0, j * LANES:(j + 1) * LANES, :] = (rots[j] * ATTN_SCALE).T.astype(BF16)
        else:
            q_ref[:, j * LANES:(j + 1) * LANES] = (rots[j] * ATTN_SCALE).astype(BF16)
    for j in range(KV_DIM // LANES):
        k_ref[:, j * LANES:(j + 1) * LANES] = rots[nq + j]
        kb_ref[:, j * LANES:(j + 1) * LANES] = rots[nq + j].astype(BF16)
    v = qkv[:, Q_DIM + KV_DIM:]
    v_ref[...] = v
    vb_ref[...] = v.astype(BF16)
    if qs_ref:
        lane = lax.broadcasted_iota(I32, rots[0].shape, 1)
        halves = []
        for kv in range(KV_HEADS):
            t = rots[2 * kv] + rots[2 * kv + 1]
            halves.append(t + pltpu.roll(t, HEAD_DIM, 1))
        for j in range(KV_DIM // LANES):
            qs_ref[0][:, j * LANES:(j + 1) * LANES] = jnp.where(
                lane < HEAD_DIM, halves[2 * j], halves[2 * j + 1])


def _qkv_project(x, shift, scale, w_bf, b, tabs, tm, with_qsum, q_seq_len=None):
    n = x.shape[0]
    n_tab = tabs[0].shape[0] // tm
    per_mod = n // tm // shift.shape[0]
    mod_rows = shift.shape[1]
    row_spec = lambda width: pl.BlockSpec((tm, width), lambda i: (i, 0))
    mod_spec = pl.BlockSpec((1, mod_rows, D_MODEL), lambda i: (i // per_mod, 0, 0))
    tab_spec = pl.BlockSpec((tm, LANES), lambda i: (i % n_tab, 0))
    if q_seq_len is None:
        q_shape, q_spec = jax.ShapeDtypeStruct((n, Q_DIM), BF16), row_spec(Q_DIM)
    else:
        per_seq = q_seq_len // tm
        q_shape = jax.ShapeDtypeStruct((n // q_seq_len, Q_DIM, q_seq_len), BF16)
        q_spec = pl.BlockSpec((1, Q_DIM, tm), lambda i: (i // per_seq, 0, i % per_seq))
    out_shape = [
        q_shape,
        jax.ShapeDtypeStruct((n, KV_DIM), F32),
        jax.ShapeDtypeStruct((n, KV_DIM), F32),
        jax.ShapeDtypeStruct((n, KV_DIM), BF16),
        jax.ShapeDtypeStruct((n, KV_DIM), BF16),
    ]
    out_specs = [q_spec] + [row_spec(KV_DIM)] * 4
    if with_qsum:
        out_shape.append(jax.ShapeDtypeStruct((n, KV_DIM), F32))
        out_specs.append(row_spec(KV_DIM))
    return pl.pallas_call(
        functools.partial(_qkv_kernel, q_transposed=q_seq_len is not None),
        grid=(n // tm,),
        in_specs=[
            row_spec(D_MODEL), mod_spec, mod_spec,
            pl.BlockSpec((D_MODEL, QKV_DIM), lambda i: (0, 0)),
            pl.BlockSpec((1, QKV_DIM), lambda i: (0, 0)),
            tab_spec, tab_spec, tab_spec,
        ],
        out_specs=out_specs,
        out_shape=out_shape,
        compiler_params=_params("arbitrary"),
        name="qkv_rope",
    )(x, shift, scale, w_bf, b.reshape(1, QKV_DIM), *tabs)


def _rope_tables(pos):
    half = ROT_DIM // 2
    inv_freq = 1.0 / (ROPE_THETA ** (jnp.arange(0, ROT_DIM, 2, dtype=F32) / ROT_DIM))
    ang = pos.astype(F32)[:, None] * inv_freq[None, :]
    cos, sin = jnp.cos(ang), jnp.sin(ang)
    rest = HEAD_DIM - ROT_DIM
    ones = jnp.ones((pos.shape[0], rest), F32)
    zeros = jnp.zeros((pos.shape[0], rest), F32)
    zh = jnp.zeros_like(sin)
    c = jnp.concatenate([cos, cos, ones], -1)
    s1 = jnp.concatenate([-sin, zh, zeros], -1)
    s2 = jnp.concatenate([zh, sin, zeros], -1)
    reps = LANES // HEAD_DIM
    return tuple(jnp.tile(t, (1, reps)) for t in (c, s1, s2))


def _moba_select_kernel(qs_ref, k_ref, sel_ref):
    t_len = k_ref.shape[1]
    nb = t_len // MOBA_BLOCK
    k = k_ref[0]
    nb_pad = 16
    means = jnp.concatenate(
        [jnp.sum(k[n * MOBA_BLOCK:(n + 1) * MOBA_BLOCK], axis=0, keepdims=True) / MOBA_BLOCK
         for n in range(nb)] + [jnp.zeros((nb_pad - nb, KV_DIM), F32)], axis=0)
    lane_head = lax.broadcasted_iota(I32, (nb_pad, KV_DIM), 1) // HEAD_DIM
    q_hi, q_lo = _split_bf16(qs_ref[0])
    pos_blk = lax.broadcasted_iota(I32, (nb, t_len), 1) // MOBA_BLOCK
    blk = lax.broadcasted_iota(I32, (nb, t_len), 0)
    past = blk < pos_blk
    for kv in range(KV_HEADS):
        m_hi, m_lo = _split_bf16(jnp.where(lane_head == kv, means, 0.0))
        gate = (_dot_nt(m_hi, q_hi) + _dot_nt(m_lo, q_hi) + _dot_nt(m_hi, q_lo))[:nb]
        gate = jnp.where(past, gate, NEG_INF)
        rank = jnp.zeros((nb, t_len), F32)
        for m in range(nb):
            other = gate[m:m + 1, :]
            ahead = (other > gate) | ((other == gate) & (blk > m))
            rank = rank + jnp.where(ahead, 1.0, 0.0)
        sel_ref[0, kv] = jnp.where(past & (rank < MOBA_TOPK), 1.0, 0.0)


def _moba_select(qsum, k):
    b, t, _ = k.shape
    nb = t // MOBA_BLOCK
    spec = pl.BlockSpec((1, t, KV_DIM), lambda i: (i, 0, 0))
    return pl.pallas_call(
        _moba_select_kernel,
        grid=(b,),
        in_specs=[spec, spec],
        out_specs=pl.BlockSpec((1, KV_HEADS, nb, t), lambda i: (i, 0, 0, 0)),
        out_shape=jax.ShapeDtypeStruct((b, KV_HEADS, nb, t), F32),
        compiler_params=_params("arbitrary"),
        name="moba_select",
    )(qsum, k)


def _moba_prompt_kernel(q_ref, k_ref, v_ref, sel_ref, o_ref):
    qi = pl.program_id(2)
    tq = MOBA_BLOCK
    cols = GROUP * tq
    q_t = jnp.concatenate([q_ref[0, g] for g in range(GROUP)], axis=1)

    def scores(n):
        kb = k_ref[0, 0, pl.ds(pl.multiple_of(n * tq, tq), tq), :]
        return _dot(kb, q_t)

    key = lax.broadcasted_iota(I32, (tq, cols), 0)
    qry = lax.broadcasted_iota(I32, (tq, cols), 1) & (tq - 1)
    s = jnp.where(key <= qry, scores(qi), NEG_INF)
    m0 = jnp.max(s, axis=0, keepdims=True)
    p = jnp.exp(s - m0)
    l0 = jnp.sum(p, axis=0, keepdims=True)
    acc0 = _dot(v_ref[0, 0, qi], p.astype(BF16))

    def body(n, carry):
        m, l, acc = carry
        chosen = sel_ref[0, 0, pl.ds(n, 1), :]
        keep = jnp.concatenate([chosen] * GROUP, axis=1) > 0.0
        s = jnp.where(keep, scores(n), NEG_INF)
        m_new = jnp.maximum(m, jnp.max(s, axis=0, keepdims=True))
        alpha = jnp.exp(m - m_new)
        p = jnp.exp(s - m_new)
        l = alpha * l + jnp.sum(p, axis=0, keepdims=True)
        acc = alpha * acc + _dot(v_ref[0, 0, n], p.astype(BF16))
        return m_new, l, acc

    _, l, acc = lax.fori_loop(0, qi, body, (m0, l0, acc0))
    _store_token_major(o_ref, acc / l, tq)


def _store_token_major(o_ref, out_t, tq):
    for pair in range(GROUP // 2):
        two = jnp.concatenate([out_t[:, (2 * pair) * tq:(2 * pair + 1) * tq],
                               out_t[:, (2 * pair + 1) * tq:(2 * pair + 2) * tq]], axis=0)
        o_ref[0, :, pair * 2 * HEAD_DIM:(pair + 1) * 2 * HEAD_DIM] = two.T.astype(BF16)


def _moba_prompt(q_t, k_hm, v_t, sel):
    b, _, _, t = q_t.shape
    nb = t // MOBA_BLOCK
    return pl.pallas_call(
        _moba_prompt_kernel,
        grid=(b, KV_HEADS, nb),
        in_specs=[pl.BlockSpec((1, GROUP, HEAD_DIM, MOBA_BLOCK), lambda i, j, n: (i, j, 0, n)),
                  pl.BlockSpec((1, 1, t, HEAD_DIM), lambda i, j, n: (i, j, 0, 0)),
                  pl.BlockSpec((1, 1, nb, HEAD_DIM, MOBA_BLOCK), lambda i, j, n: (i, j, 0, 0, 0)),
                  pl.BlockSpec((1, 1, nb, MOBA_BLOCK), lambda i, j, n: (i, j, 0, n))],
        out_specs=pl.BlockSpec((1, MOBA_BLOCK, GROUP * HEAD_DIM), lambda i, j, n: (i, n, j)),
        out_shape=jax.ShapeDtypeStruct((b, t, Q_DIM), BF16),
        compiler_params=_params("arbitrary", "arbitrary", "arbitrary"),
        name="moba_prompt",
    )(q_t, k_hm, v_t, sel)


def _swa_prompt_kernel(sink_ref, q_ref, kp_ref, k_ref, vp_ref, v_ref, o_ref):
    j = pl.program_id(1)
    n = pl.program_id(2)
    tq = WINDOW
    cols = GROUP * tq
    q_t = jnp.concatenate([q_ref[0, g] for g in range(GROUP)], axis=1)
    keys = jnp.concatenate([kp_ref[0, 0], k_ref[0, 0]], axis=0)
    vals_t = jnp.concatenate([vp_ref[0, 0, 0], v_ref[0, 0, 0]], axis=1)
    key = lax.broadcasted_iota(I32, (2 * tq, cols), 0)
    qry = lax.broadcasted_iota(I32, (2 * tq, cols), 1) & (tq - 1)
    visible = ((key < tq) & (key >= qry)) | ((key >= tq) & (key - tq <= qry))
    first_block = jnp.where(n > 0, 0.0, NEG_INF)
    s = _dot(keys, q_t)
    s = jnp.where(visible, s + jnp.where(key < tq, first_block, 0.0), NEG_INF)
    sink = jnp.concatenate([jnp.full((1, tq), sink_ref[j * GROUP + g], F32) for g in range(GROUP)], axis=1)
    m = jnp.maximum(jnp.max(s, axis=0, keepdims=True), sink)
    p = jnp.exp(s - m)
    den = jnp.sum(p, axis=0, keepdims=True) + jnp.exp(sink - m)
    _store_token_major(o_ref, _dot(vals_t, p.astype(BF16)) / den, tq)


def _swa_prompt(q_t, k_hm, v_t, sinks):
    b, _, _, t = q_t.shape
    nb = t // WINDOW
    before = lambda n: jnp.maximum(n - 1, 0)
    q_spec = pl.BlockSpec((1, GROUP, HEAD_DIM, WINDOW), lambda i, j, n, s: (i, j, 0, n))
    k_own = pl.BlockSpec((1, 1, WINDOW, HEAD_DIM), lambda i, j, n, s: (i, j, n, 0))
    k_prev = pl.BlockSpec((1, 1, WINDOW, HEAD_DIM), lambda i, j, n, s: (i, j, before(n), 0))
    v_own = pl.BlockSpec((1, 1, 1, HEAD_DIM, WINDOW), lambda i, j, n, s: (i, j, n, 0, 0))
    v_prev = pl.BlockSpec((1, 1, 1, HEAD_DIM, WINDOW), lambda i, j, n, s: (i, j, before(n), 0, 0))
    return pl.pallas_call(
        _swa_prompt_kernel,
        grid_spec=pltpu.PrefetchScalarGridSpec(
            num_scalar_prefetch=1,
            grid=(b, KV_HEADS, nb),
            in_specs=[q_spec, k_prev, k_own, v_prev, v_own],
            out_specs=pl.BlockSpec((1, WINDOW, GROUP * HEAD_DIM), lambda i, j, n, s: (i, n, j)),
        ),
        out_shape=jax.ShapeDtypeStruct((b, t, Q_DIM), BF16),
        compiler_params=_params("arbitrary", "arbitrary", "arbitrary"),
        name="swa_prompt",
    )(sinks, q_t, k_hm, k_hm, v_t, v_t)


def _fold_heads(acc):
    rows = acc.shape[0]
    lane_head = lax.broadcasted_iota(I32, (rows, KV_DIM), 1) // HEAD_DIM
    row_head = (lax.broadcasted_iota(I32, (rows, KV_DIM), 0) // GROUP) % KV_HEADS
    a = jnp.where(lane_head == row_head, acc, 0.0)
    a = a[:, :LANES] + a[:, LANES:]
    a = a + pltpu.roll(a, HEAD_DIM, 1)
    return a[:, :HEAD_DIM]


PREFETCH = 3
SLOTS = PREFETCH + 1


def _moba_sample_kernel(pt_ref, lhs_ref, kn_ref, vn_ref, kpool, vpool, o_ref,
                        buf, sem, s_scr, g_scr, stat_scr, l_scr, acc_scr, *, n_pages, chunk_pages):
    b = pl.program_id(0)
    c = pl.program_id(1)
    n_seq = pl.num_programs(0)
    n_chunks = n_pages // chunk_pages
    steps = 2 * n_chunks
    g = b * steps + c
    slot = g % SLOTS
    rows = o_ref.shape[1]
    gate_rows = (lhs_ref.shape[1] - rows) // 2
    blk_pages = MOBA_BLOCK // PAGE_SIZE
    nb = n_pages // blk_pages

    def page_copy(pool, page, slot_, p):
        return pltpu.make_async_copy(pool.at[page], buf.at[slot_, p], sem.at[slot_])

    def start(step, seq, slot_):
        base = jnp.where(step >= n_chunks, step - n_chunks, step) * chunk_pages
        pages = [pt_ref[seq, base + p] for p in range(chunk_pages)]

        @pl.when(step < n_chunks)
        def _():
            for p in range(chunk_pages):
                page_copy(kpool, pages[p], slot_, p).start(priority=p % 2)

        @pl.when(step >= n_chunks)
        def _():
            for p in range(chunk_pages):
                page_copy(vpool, pages[p], slot_, p).start(priority=p % 2)

    @pl.when(g == 0)
    def _():
        for ahead in range(PREFETCH):
            start(jnp.int32(ahead % steps), jnp.int32(ahead // steps), ahead % SLOTS)

    nxt = g + PREFETCH

    @pl.when(nxt < n_seq * steps)
    def _():
        start(nxt % steps, nxt // steps, nxt % SLOTS)

    for p in range(chunk_pages):
        page_copy(kpool, 0, slot, p).wait()

    lhs = lhs_ref[0]
    qbd = lhs[:rows]

    @pl.when(c < n_chunks)
    def _():
        for p in range(chunk_pages):
            res = _dot(lhs, buf[slot, p].astype(BF16))
            s_scr[c * chunk_pages + p] = res[:rows]
            g_scr[c * chunk_pages + p] = res[rows:]

    @pl.when(c == n_chunks - 1)
    def _():
        lane = lax.broadcasted_iota(I32, (gate_rows, LANES), 1).astype(F32)
        gate = jnp.full((gate_rows, LANES), NEG_INF, F32)
        for n in range(nb):
            part = g_scr[n * blk_pages]
            for j in range(1, blk_pages):
                part = part + g_scr[n * blk_pages + j]
            col = jnp.sum(part[:gate_rows] + part[gate_rows:], axis=-1, keepdims=True) / MOBA_BLOCK
            gate = jnp.where(lane == n, col, gate)
        sel = jnp.zeros(gate.shape, F32)
        for _ in range(min(MOBA_TOPK, nb)):
            top = jnp.max(gate, axis=-1, keepdims=True)
            first = jnp.min(jnp.where(gate == top, lane, BIG_INDEX), axis=-1, keepdims=True)
            hit = lane == first
            sel = jnp.where(hit, 1.0, sel)
            gate = jnp.where(hit, NEG_INF, gate)
        sel_rows = jnp.concatenate(
            [jnp.broadcast_to(sel[r:r + 1], (GROUP, LANES)) for r in range(sel.shape[0])], axis=0)
        s_new = _dot_nt(qbd, kn_ref[0].astype(BF16))
        qrow = lax.broadcasted_iota(I32, s_new.shape, 0) // N_HEADS
        s_new = jnp.where(lax.broadcasted_iota(I32, s_new.shape, 1) <= qrow, s_new, NEG_INF)
        top = s_new
        for n in range(nb):
            keep = sel_rows[:, n:n + 1] > 0.0
            for j in range(blk_pages):
                s = jnp.where(keep, s_scr[n * blk_pages + j], NEG_INF)
                s_scr[n * blk_pages + j] = s
                top = jnp.maximum(top, s)
        m_run = jnp.max(top, axis=-1, keepdims=True)
        p_new = jnp.exp(s_new - m_run)
        stat_scr[:, 0:1] = m_run
        stat_scr[:, 1:2] = jnp.sum(p_new, axis=-1, keepdims=True)
        l_scr[...] = jnp.zeros(l_scr.shape, F32)
        acc_scr[...] = _dot(p_new.astype(BF16), vn_ref[0].astype(BF16))

    @pl.when(c >= n_chunks)
    def _():
        m_run = stat_scr[:, 0:1]
        l = l_scr[...]
        acc = acc_scr[...]
        for p in range(chunk_pages):
            prob = jnp.exp(s_scr[(c - n_chunks) * chunk_pages + p] - m_run)
            l = l + prob
            acc = acc + _dot_nt(prob.astype(BF16), buf[slot, p].astype(BF16))
        l_scr[...] = l
        acc_scr[...] = acc

    @pl.when(c == steps - 1)
    def _():
        den = jnp.sum(l_scr[...], axis=-1, keepdims=True) + stat_scr[:, 1:2]
        o_ref[0] = _fold_heads(acc_scr[...] / den)


def _moba_sample(page_table, lhs, k_new, v_new, kpool_t, vpool_t):
    db = lhs.shape[0]
    rows = lhs.shape[1] * N_HEADS // (N_HEADS + 2 * KV_HEADS)
    n_pages = page_table.shape[1]
    chunk_pages = n_pages // 2
    nb = n_pages * PAGE_SIZE // MOBA_BLOCK
    assert nb <= LANES and chunk_pages % (MOBA_BLOCK // PAGE_SIZE) == 0
    steps = 2 * (n_pages // chunk_pages)
    seq3 = lambda a: pl.BlockSpec((1,) + a.shape[1:], lambda i, c, pt: (i, 0, 0))
    kern = functools.partial(_moba_sample_kernel, n_pages=n_pages, chunk_pages=chunk_pages)
    return pl.pallas_call(
        kern,
        grid_spec=pltpu.PrefetchScalarGridSpec(
            num_scalar_prefetch=1,
            grid=(db, steps),
            in_specs=[seq3(lhs), seq3(k_new), seq3(v_new),
                      pl.BlockSpec(memory_space=pl.ANY), pl.BlockSpec(memory_space=pl.ANY)],
            out_specs=pl.BlockSpec((1, rows, HEAD_DIM), lambda i, c, pt: (i, 0, 0)),
            scratch_shapes=[
                pltpu.VMEM((SLOTS, chunk_pages, KV_DIM, PAGE_SIZE), F32),
                pltpu.SemaphoreType.DMA((SLOTS,)),
                pltpu.VMEM((n_pages, rows, PAGE_SIZE), F32),
                pltpu.VMEM((n_pages, lhs.shape[1] - rows, PAGE_SIZE), F32),
                pltpu.VMEM((rows, LANES), F32),
                pltpu.VMEM((rows, LANES), F32),
                pltpu.VMEM((rows, KV_DIM), F32),
            ],
        ),
        out_shape=jax.ShapeDtypeStruct((db, rows, HEAD_DIM), F32),
        compiler_params=_params("arbitrary", "arbitrary"),
        name="moba_sample",
    )(page_table, lhs, k_new, v_new, kpool_t, vpool_t)


def _swa_sample_kernel(qbd_ref, sink_ref, kb_ref, vb_ref, kn_ref, vn_ref, o_ref):
    rows = qbd_ref.shape[1]
    wb = kb_ref.shape[1]
    ds = kn_ref.shape[1]
    sink = sink_ref[...]
    qrow_b = lax.broadcasted_iota(I32, (rows, wb), 0) // N_HEADS
    buf_ok = lax.broadcasted_iota(I32, (rows, wb), 1) >= qrow_b
    qrow_n = lax.broadcasted_iota(I32, (rows, ds), 0) // N_HEADS
    new_ok = lax.broadcasted_iota(I32, (rows, ds), 1) <= qrow_n
    for s in range(qbd_ref.shape[0]):
        qbd = qbd_ref[s]
        sb = jnp.where(buf_ok, _dot_nt(qbd, kb_ref[s].astype(BF16)), NEG_INF)
        sn = jnp.where(new_ok, _dot_nt(qbd, kn_ref[s].astype(BF16)), NEG_INF)
        m = jnp.maximum(jnp.maximum(jnp.max(sb, axis=-1, keepdims=True),
                                    jnp.max(sn, axis=-1, keepdims=True)), sink)
        pb = jnp.exp(sb - m)
        pn = jnp.exp(sn - m)
        den = (jnp.sum(pb, axis=-1, keepdims=True) + jnp.sum(pn, axis=-1, keepdims=True)
               + jnp.exp(sink - m))
        acc = _dot(pb.astype(BF16), vb_ref[s].astype(BF16)) + _dot(pn.astype(BF16), vn_ref[s].astype(BF16))
        o_ref[s] = _fold_heads(acc / den)


def _swa_sample(qbd, sink_rows, buf_k, buf_v, k_new, v_new, seqs_per_step=8):
    db, rows, _ = qbd.shape
    wb = buf_k.shape[1]
    ds = k_new.shape[1]
    spec = lambda r: pl.BlockSpec((seqs_per_step, r, KV_DIM), lambda i: (i, 0, 0))
    return pl.pallas_call(
        _swa_sample_kernel,
        grid=(db // seqs_per_step,),
        in_specs=[spec(rows), pl.BlockSpec((rows, 1), lambda i: (0, 0)),
                  spec(wb), spec(wb), spec(ds), spec(ds)],
        out_specs=pl.BlockSpec((seqs_per_step, rows, HEAD_DIM), lambda i: (i, 0, 0)),
        out_shape=jax.ShapeDtypeStruct((db, rows, HEAD_DIM), F32),
        compiler_params=_params("arbitrary"),
        name="swa_sample",
    )(qbd, sink_rows, buf_k, buf_v, k_new, v_new)


def _attn_out_kernel(a_ref, x_ref, g_ref, sh_ref, sc_ref, wo_ref, bo_ref, lg_ref, lb_ref,
                     wrh_ref, wrl_ref, xn_ref, h_ref, lt_ref):
    o = _dot(a_ref[...], wo_ref[...]) + bo_ref[...]
    xn = _layer_norm(ALPHA * x_ref[...] + g_ref[0] * o, lg_ref[...], lb_ref[...])
    xn_ref[...] = xn
    h = xn * (1.0 + sc_ref[0]) + sh_ref[0]
    h_ref[...] = _pack_rows(h)
    h_hi, h_lo = _split_bf16(h)
    wrh = wrh_ref[...]
    lt_ref[...] = _dot_nt(wrh, h_hi) + _dot_nt(wrh, h_lo) + _dot_nt(wrl_ref[...], h_hi)


def _attn_out(a, x, gate, shift, scale, wo_bf, bo, ln_g, ln_b, wr_hi, wr_lo, tm):
    n = x.shape[0]
    per_mod = n // tm // gate.shape[0]
    mod_rows = gate.shape[1]
    row = pl.BlockSpec((tm, D_MODEL), lambda i: (i, 0))
    mod = pl.BlockSpec((1, mod_rows, D_MODEL), lambda i: (i // per_mod, 0, 0))
    vec = pl.BlockSpec((1, D_MODEL), lambda i: (0, 0))
    full = lambda r, c: pl.BlockSpec((r, c), lambda i: (0, 0))
    return pl.pallas_call(
        _attn_out_kernel,
        grid=(n // tm,),
        in_specs=[row, row, mod, mod, mod, full(Q_DIM, D_MODEL), vec, vec, vec,
                  full(N_EXPERTS, D_MODEL), full(N_EXPERTS, D_MODEL)],
        out_specs=[row, pl.BlockSpec((tm, PACK_WIDTH), lambda i: (i, 0)),
                   pl.BlockSpec((N_EXPERTS, tm), lambda i: (0, i))],
        out_shape=[jax.ShapeDtypeStruct((n, D_MODEL), F32),
                   jax.ShapeDtypeStruct((n, PACK_WIDTH), PACK_DTYPE),
                   jax.ShapeDtypeStruct((N_EXPERTS, n), F32)],
        compiler_params=_params("arbitrary"),
        name="attn_out_norm",
    )(a, x, gate, shift, scale, wo_bf, bo.reshape(1, -1), ln_g.reshape(1, -1), ln_b.reshape(1, -1),
      wr_hi, wr_lo)


def _first_max(x, idx):
    top = jnp.max(x, axis=0, keepdims=True)
    first = jnp.min(jnp.where(x == top, idx, BIG_INDEX), axis=0, keepdims=True)
    return top, first


def _route_kernel(lt_ref, b_ref, idx_ref, w_ref, pos_ref, cnt_ref):
    i = pl.program_id(0)
    tn = lt_ref.shape[1]
    scores = _sigmoid(lt_ref[...])
    biased = scores + b_ref[...]
    e_idx = lax.broadcasted_iota(I32, (N_EXPERTS, tn), 0).astype(F32)
    l_idx = lax.broadcasted_iota(I32, (PER_GROUP, tn), 0).astype(F32)
    g_score = []
    for g in range(N_EXPERT_GROUPS):
        x = biased[g * PER_GROUP:(g + 1) * PER_GROUP]
        top1, first = _first_max(x, l_idx)
        top2 = jnp.max(jnp.where(l_idx == first, NEG_INF, x), axis=0, keepdims=True)
        g_score.append(top1 + top2)
    pieces = []
    for g in range(N_EXPERT_GROUPS):
        ahead = jnp.zeros((1, tn), F32)
        for o in range(N_EXPERT_GROUPS):
            if o != g:
                wins = (g_score[o] >= g_score[g]) if o < g else (g_score[o] > g_score[g])
                ahead = ahead + jnp.where(wins, 1.0, 0.0)
        pieces.append(jnp.where(ahead < TOPK_GROUPS, biased[g * PER_GROUP:(g + 1) * PER_GROUP], NEG_INF))
    masked = jnp.concatenate(pieces, axis=0)
    chosen = jnp.zeros((N_EXPERTS, tn), F32)
    firsts, picked = [], []
    for _ in range(TOP_K):
        _, first = _first_max(masked, e_idx)
        hit = e_idx == first
        firsts.append(first)
        picked.append(jnp.sum(jnp.where(hit, scores, 0.0), axis=0, keepdims=True))
        chosen = jnp.where(hit, 1.0, chosen)
        masked = jnp.where(hit, NEG_INF, masked)
    total = picked[0]
    for s in picked[1:]:
        total = total + s
    for r in range(TOP_K):
        idx_ref[r:r + 1, :] = firsts[r].astype(I32)
        w_ref[r:r + 1, :] = picked[r] / total * ROUTED_SCALE

    @pl.when(i == 0)
    def _():
        cnt_ref[...] = jnp.zeros(cnt_ref.shape, F32)
    r = lax.broadcasted_iota(I32, (tn, tn), 0)
    c = lax.broadcasted_iota(I32, (tn, tn), 1)
    before = jnp.where(r < c, 1.0, 0.0).astype(BF16)
    prior = _dot(chosen.astype(BF16), before) + cnt_ref[...]
    for r in range(TOP_K):
        pos_ref[r:r + 1, :] = jnp.sum(jnp.where(e_idx == firsts[r], prior, 0.0), axis=0,
                                      keepdims=True).astype(I32)
    cnt_ref[...] = cnt_ref[...] + jnp.sum(chosen, axis=1, keepdims=True)


def _route(logits_t, b_router, tn=512):
    n = logits_t.shape[1]
    slot = pl.BlockSpec((TOP_K, tn), lambda i: (0, i))
    return pl.pallas_call(
        _route_kernel,
        grid=(n // tn,),
        in_specs=[pl.BlockSpec((N_EXPERTS, tn), lambda i: (0, i)),
                  pl.BlockSpec((N_EXPERTS, 1), lambda i: (0, 0))],
        out_specs=[slot, slot, slot, pl.BlockSpec((N_EXPERTS, 1), lambda i: (0, 0))],
        out_shape=[jax.ShapeDtypeStruct((TOP_K, n), I32),
                   jax.ShapeDtypeStruct((TOP_K, n), F32),
                   jax.ShapeDtypeStruct((TOP_K, n), I32),
                   jax.ShapeDtypeStruct((N_EXPERTS, 1), F32)],
        compiler_params=_params("arbitrary"),
        name="route",
    )(logits_t, b_router.reshape(N_EXPERTS, 1))


def _dest_kernel(idx_ref, pos_ref, start_ref, dest_ref):
    tn = idx_ref.shape[1]
    e_idx = lax.broadcasted_iota(I32, (N_EXPERTS, tn), 0)
    start = start_ref[...]
    for r in range(TOP_K):
        first = jnp.sum(jnp.where(e_idx == idx_ref[r:r + 1, :], start, 0.0), axis=0, keepdims=True)
        dest_ref[r:r + 1, :] = first.astype(I32) + pos_ref[r:r + 1, :]


def _dest_rows(idx_t, pos_t, pad_start, tn=512):
    n = idx_t.shape[1]
    slot = pl.BlockSpec((TOP_K, tn), lambda i: (0, i))
    return pl.pallas_call(
        _dest_kernel,
        grid=(n // tn,),
        in_specs=[slot, slot, pl.BlockSpec((N_EXPERTS, 1), lambda i: (0, 0))],
        out_specs=slot,
        out_shape=jax.ShapeDtypeStruct((TOP_K, n), I32),
        compiler_params=_params("arbitrary"),
        name="dest_rows",
    )(idx_t, pos_t, pad_start.astype(F32).reshape(N_EXPERTS, 1))


SC_CORES = 2
SC_SUBCORES = 16
SC_WINDOW = 88


def _scatter_rows(rows_tok, dest_flat, n_rows):
    n, width = rows_tok.shape
    workers = SC_CORES * SC_SUBCORES
    per_worker = n // workers
    assert n % workers == 0 and per_worker % SC_WINDOW == 0 and SC_WINDOW % 8 == 0
    mesh = plsc.VectorSubcoreMesh(core_axis_name="c", subcore_axis_name="s",
                                  num_cores=SC_CORES, num_subcores=SC_SUBCORES)

    def body(rows_hbm, dest_hbm, out_hbm, idx_v, rows_v):
        worker = lax.axis_index("s") * SC_CORES + lax.axis_index("c")
        base = worker * per_worker

        @pl.loop(0, per_worker // SC_WINDOW)
        def _(j):
            t0 = pl.multiple_of(base + j * SC_WINDOW, 8)
            pltpu.sync_copy(rows_hbm.at[pl.ds(t0, SC_WINDOW)], rows_v)
            for r in range(TOP_K):
                pltpu.sync_copy(dest_hbm.at[pl.ds(pl.multiple_of(r * n + t0, 8), SC_WINDOW)], idx_v)
                pltpu.sync_copy(rows_v, out_hbm.at[idx_v])

    return pl.kernel(
        body,
        out_type=jax.ShapeDtypeStruct((n_rows, width), rows_tok.dtype),
        mesh=mesh,
        scratch_types=[pltpu.VMEM((SC_WINDOW,), I32), pltpu.VMEM((SC_WINDOW, width), rows_tok.dtype)],
        name="scatter_rows",
    )(rows_tok, dest_flat)


def _moe_kernel(be_ref, nu_ref, x_ref, wg_ref, wu_ref, wd_ref, o_ref, wg_bf, wu_bf, wd_bf):
    i = pl.program_id(0)
    used = i < nu_ref[0]
    changed = (i == 0) | (be_ref[i] != be_ref[jnp.maximum(i - 1, 0)])

    @pl.when(used & changed)
    def _():
        wg_bf[...] = wg_ref[0, 0].astype(BF16)
        wu_bf[...] = wu_ref[0, 0].astype(BF16)
        wd_bf[...] = wd_ref[0, 0].astype(BF16)

    @pl.when(used)
    def _():
        x = jnp.concatenate(_unpack_rows(x_ref[...]), axis=1).astype(BF16)
        gate = _dot(x, wg_bf[...])
        up = _dot(x, wu_bf[...])
        act = gate * _sigmoid(gate) * up
        o_ref[...] = _pack_rows(_dot(act.astype(BF16), wd_bf[...]))


def _moe_experts(blk_e, n_used, x_sorted, w_gate, w_up, w_down, layer):
    rows = x_sorted.shape[0]
    n_blocks = rows // MOE_ROWS
    x_map = lambda i, be, nu: (jnp.minimum(i, nu[0] - 1), 0)
    w_map = lambda i, be, nu: (layer, be[i], 0, 0)
    return pl.pallas_call(
        _moe_kernel,
        grid_spec=pltpu.PrefetchScalarGridSpec(
            num_scalar_prefetch=2,
            grid=(n_blocks,),
            in_specs=[pl.BlockSpec((MOE_ROWS, PACK_WIDTH), x_map),
                      pl.BlockSpec((1, 1, D_MODEL, D_EXPERT), w_map),
                      pl.BlockSpec((1, 1, D_MODEL, D_EXPERT), w_map),
                      pl.BlockSpec((1, 1, D_EXPERT, D_MODEL), w_map)],
            out_specs=pl.BlockSpec((MOE_ROWS, PACK_WIDTH), x_map),
            scratch_shapes=[pltpu.VMEM((D_MODEL, D_EXPERT), BF16),
                            pltpu.VMEM((D_MODEL, D_EXPERT), BF16),
                            pltpu.VMEM((D_EXPERT, D_MODEL), BF16)],
        ),
        out_shape=jax.ShapeDtypeStruct((rows, PACK_WIDTH), PACK_DTYPE),
        compiler_params=_params("arbitrary"),
        name="moe_experts",
    )(blk_e, n_used, x_sorted, w_gate, w_up, w_down)


def _ffn_out_kernel(h_ref, y_ref, w_ref, x_ref, g_ref, wsg_ref, wsu_ref, wsd_ref, lg_ref, lb_ref, o_ref):
    h = jnp.concatenate(_unpack_rows(h_ref[...]), axis=1).astype(BF16)
    gate = _dot(h, wsg_ref[...])
    up = _dot(h, wsu_ref[...])
    shared = _dot((gate * _sigmoid(gate) * up).astype(BF16), wsd_ref[...])
    w = w_ref[...]
    first, second = None, None
    for r in range(TOP_K):
        ya, yb = _unpack_rows(y_ref[r])
        wr = w[:, r:r + 1]
        first = ya * wr if first is None else first + ya * wr
        second = yb * wr if second is None else second + yb * wr
    ffn = shared + jnp.concatenate([first, second], axis=1)
    y = ALPHA * x_ref[...] + g_ref[0] * ffn
    o_ref[...] = _layer_norm(y, lg_ref[...], lb_ref[...])


def _ffn_out(h, y_tok, w_tok, first_row, x, gate, wsg, wsu, wsd, ln_g, ln_b, tm):
    n = x.shape[0]
    per_mod = n // tm // gate.shape[0]
    off = first_row // tm
    row = pl.BlockSpec((tm, D_MODEL), lambda i: (i, 0))
    mod = pl.BlockSpec((1, gate.shape[1], D_MODEL), lambda i: (i // per_mod, 0, 0))
    vec = pl.BlockSpec((1, D_MODEL), lambda i: (0, 0))
    full = lambda a: pl.BlockSpec(a.shape, lambda i: (0, 0))
    return pl.pallas_call(
        _ffn_out_kernel,
        grid=(n // tm,),
        in_specs=[pl.BlockSpec((tm, PACK_WIDTH), lambda i: (i, 0)),
                  pl.BlockSpec((TOP_K, tm, PACK_WIDTH), lambda i: (0, i + off, 0)),
                  pl.BlockSpec((tm, TOP_K), lambda i: (i + off, 0)),
                  row, mod, full(wsg), full(wsu), full(wsd), vec, vec],
        out_specs=row,
        out_shape=jax.ShapeDtypeStruct((n, D_MODEL), F32),
        compiler_params=_params("arbitrary"),
        name="ffn_out_norm",
    )(h, y_tok, w_tok, x, gate, wsg, wsu, wsd, ln_g.reshape(1, -1), ln_b.reshape(1, -1))


def _dispatch(idx_t, pos_t, counts):
    n = idx_t.shape[1]
    counts = counts.reshape(N_EXPERTS).astype(I32)
    padded = (counts + MOE_ROWS - 1) // MOE_ROWS * MOE_ROWS
    pad_end = jnp.cumsum(padded)
    pad_start = pad_end - padded
    dest = _dest_rows(idx_t, pos_t, pad_start)
    n_blocks = (n * TOP_K + N_EXPERTS * (MOE_ROWS - 1)) // MOE_ROWS
    n_used = pad_end[-1] // MOE_ROWS
    blk = jnp.minimum(jnp.arange(n_blocks, dtype=I32), n_used - 1) * MOE_ROWS
    blk_e = jnp.minimum(jnp.sum((pad_end[None, :] <= blk[:, None]).astype(I32), axis=1), N_EXPERTS - 1)
    return dest, n_blocks * MOE_ROWS, blk_e, n_used.reshape(1).astype(I32)


def _block_diag_rows(x, heads_per_kv):
    db, ds, heads, _ = x.shape
    kv_of_head = jnp.arange(heads) // heads_per_kv
    onehot = (kv_of_head[:, None] == jnp.arange(KV_HEADS)[None, :]).astype(x.dtype)
    out = x[:, :, :, None, :] * onehot[None, None, :, :, None]
    return out.reshape(db, ds * heads, KV_DIM)


def kernel(x_prompt, x_sample, c_prompt, c_sample, cache_moba_k, cache_moba_v, state_swa_k, state_swa_v,
           page_table, w_ada, b_ada, w_qkv, b_qkv, attn_sinks, w_o, b_o, ln_attn_g, ln_attn_b,
           w_router, b_router, w_exp_gate, w_exp_up, w_exp_down, w_sh_gate, w_sh_up, w_sh_down,
           ln_ffn_g, ln_ffn_b):
    bsz, seq, d = x_prompt.shape
    db, ds, _ = x_sample.shape
    n_p, n_s = bsz * seq, db * ds
    past_len = page_table.shape[1] * PAGE_SIZE
    tm_p, tm_s, tm_norm = 512, 256, 512
    depth = w_ada.shape[0]

    mods = _ada_mod(jnp.concatenate([c_prompt, c_sample], 0), w_ada, b_ada)
    tabs_p = _rope_tables(jnp.arange(seq, dtype=I32))
    tabs_s = _rope_tables(jnp.tile(past_len + jnp.arange(ds, dtype=I32), db))

    xp = x_prompt.reshape(n_p, d)
    xs = x_sample.reshape(n_s, d)
    outs = {k: [] for k in ("mkp", "mvp", "mks", "mvs", "skp", "svp", "sks", "svs")}
    for i in range(depth):
        j = i // 2
        m = mods[i].reshape(bsz + db, N_MOD, d)
        mp = [m[:bsz, t][:, None, :] for t in range(N_MOD)]
        ms = [jnp.repeat(m[bsz:, t], ds, axis=0).reshape(n_s // tm_s, tm_s, d)
              for t in range(N_MOD)]
        w_qkv_bf = w_qkv[i].astype(BF16)
        moba = i % 2 == 0
        res_p = _qkv_project(xp, mp[0], mp[1], w_qkv_bf, b_qkv[i], tabs_p, tm_p, moba, q_seq_len=seq)
        res_s = _qkv_project(xs, ms[0], ms[1], w_qkv_bf, b_qkv[i], tabs_s, tm_s, moba)
        qp, kp, vp, kp_bf, vp_bf = res_p[:5]
        qs, ks, vs = res_s[:3]
        blk = MOBA_BLOCK if moba else WINDOW
        q_t = qp.reshape(bsz, N_HEADS, HEAD_DIM, seq)
        k_hm = kp_bf.reshape(bsz, seq, KV_HEADS, HEAD_DIM).transpose(0, 2, 1, 3)
        v_t = vp_bf.reshape(bsz, seq // blk, blk, KV_HEADS, HEAD_DIM).transpose(0, 3, 1, 4, 2)
        qbd = _block_diag_rows(qs.reshape(db, ds, N_HEADS, HEAD_DIM), GROUP)
        pad_rows = ((0, 0), (0, LANES - ds), (0, 0))
        ks3 = jnp.pad(ks.reshape(db, ds, KV_DIM), pad_rows)
        vs3 = jnp.pad(vs.reshape(db, ds, KV_DIM), pad_rows)
        kp5 = kp.reshape(bsz, seq, KV_HEADS, HEAD_DIM)
        vp5 = vp.reshape(bsz, seq, KV_HEADS, HEAD_DIM)
        ks5 = ks.reshape(db, ds, KV_HEADS, HEAD_DIM)
        vs5 = vs.reshape(db, ds, KV_HEADS, HEAD_DIM)
        if moba:
            sel = _moba_select(res_p[5].reshape(bsz, seq, KV_DIM), kp.reshape(bsz, seq, KV_DIM))
            a_t = _moba_prompt(q_t, k_hm, v_t, sel)
            qs_hi, qs_lo = _split_bf16(_block_diag_rows(res_s[5].reshape(db, ds, KV_HEADS, HEAD_DIM), 1))
            n_pool = cache_moba_k.shape[1]
            pool_t = lambda pool: pool[j].transpose(0, 2, 3, 1).reshape(n_pool, KV_DIM, PAGE_SIZE)
            a_s = _moba_sample(page_table, jnp.concatenate([qbd, qs_hi, qs_lo], axis=1), ks3, vs3,
                               pool_t(cache_moba_k), pool_t(cache_moba_v))
            outs["mkp"].append(kp5)
            outs["mvp"].append(vp5)
            outs["mks"].append(ks5)
            outs["mvs"].append(vs5)
        else:
            a_t = _swa_prompt(q_t, k_hm, v_t, attn_sinks[j])
            wb = state_swa_k.shape[2]
            buf_k = state_swa_k[j].reshape(db, wb, KV_DIM)
            buf_v = state_swa_v[j].reshape(db, wb, KV_DIM)
            sink_rows = jnp.tile(attn_sinks[j], ds).reshape(ds * N_HEADS, 1)
            a_s = _swa_sample(qbd, sink_rows, buf_k, buf_v, ks3, vs3)
            wbp = min(WINDOW, seq)
            outs["skp"].append(kp5[:, -wbp:])
            outs["svp"].append(vp5[:, -wbp:])
            outs["sks"].append(jnp.concatenate([state_swa_k[j], ks5], axis=1)[:, -wb:])
            outs["svs"].append(jnp.concatenate([state_swa_v[j], vs5], axis=1)[:, -wb:])
        a_p = a_t.reshape(n_p, Q_DIM)
        a_s = a_s.reshape(n_s, Q_DIM).astype(BF16)

        wo_bf = w_o[i].astype(BF16)
        wr_hi, wr_lo = _split_bf16(w_router[i].T)
        args = (wo_bf, b_o[i], ln_attn_g[i], ln_attn_b[i], wr_hi, wr_lo)
        xp, hp, ltp = _attn_out(a_p, xp, mp[2], mp[3], mp[4], *args, tm_norm)
        xs, hs, lts = _attn_out(a_s, xs, ms[2], ms[3], ms[4], *args, tm_s)

        h_all = jnp.concatenate([hp, hs], axis=0)
        idx_t, w_t, pos_t, counts = _route(jnp.concatenate([ltp, lts], axis=1), b_router[i])
        dest, n_rows, blk_e, n_used = _dispatch(idx_t, pos_t, counts)
        x_sorted = _scatter_rows(h_all, dest.reshape(-1), n_rows)
        y_sorted = _moe_experts(blk_e, n_used, x_sorted, w_exp_gate, w_exp_up, w_exp_down, i)
        y_tok = y_sorted[dest.reshape(-1)].reshape(TOP_K, n_p + n_s, -1)
        w_tok = w_t.T

        ws = (w_sh_gate[i].astype(BF16), w_sh_up[i].astype(BF16), w_sh_down[i].astype(BF16))
        norm = (ln_ffn_g[i], ln_ffn_b[i])
        xp = _ffn_out(hp, y_tok, w_tok, 0, xp, mp[5], *ws, *norm, tm_norm)
        xs = _ffn_out(hs, y_tok, w_tok, n_p, xs, ms[5], *ws, *norm, tm_s)

    st = lambda key: jnp.stack(outs[key])
    return (xp.reshape(bsz, seq, d), xs.reshape(db, ds, d), st("mkp"), st("mvp"), st("mks"), st("mvs"),
            st("skp"), st("svp"), st("sks"), st("svs"))
```
